```python
import math
import jax, jax.numpy as jnp
from jax import lax
import numpy as np

D_MODEL = 1024
BATCH = 8
SEQ = 2048
DEPTH = 1
DEC_BATCH = 128
DEC_SEQ = 8
PAST_LEN = 16384
PAGE_SIZE = 128

N_META = 16
MIX_WIDTH = 2 * D_MODEL
S5_WIDTH = MIX_WIDTH // 2
S5_CH_PER_GROUP = 16
S5_GROUPS = S5_WIDTH // S5_CH_PER_GROUP
S5_STATE = 64
M2_WIDTH = MIX_WIDTH - S5_WIDTH
M2_HEAD_DIM = 64
M2_HEADS = M2_WIDTH // M2_HEAD_DIM
M2_GROUPS = 2
M2_HPG = M2_HEADS // M2_GROUPS
M2_D_STATE = 128
M2_CONV = 4
M2_CHUNK = 128
M2_CONV_DIM = M2_WIDTH + 2 * M2_GROUPS * M2_D_STATE
IN_PROJ = S5_WIDTH + M2_WIDTH + M2_CONV_DIM + M2_HEADS
PEER_HEADS = 8
PEER_N_KEYS = 128
PEER_EXPERTS = PEER_N_KEYS * PEER_N_KEYS
PEER_KEY_DIM = 256
PEER_HALF = PEER_KEY_DIM // 2
PEER_TOPK = 16
PEER_BLOCK = 128
EPS = 1e-6

kernel_name = "hymba_s5_mamba2_peer_step"

F32 = jnp.float32


def rmsnorm(x, w):
    xf = x.astype(F32)
    y = xf * lax.rsqrt(jnp.mean(xf * xf, axis=-1, keepdims=True) + EPS) * w.astype(F32)
    return y.astype(x.dtype)


def _complex_affine_combine(e1, e2):
    a1r, a1i, b1r, b1i = e1
    a2r, a2i, b2r, b2i = e2
    return (a2r * a1r - a2i * a1i, a2r * a1i + a2i * a1r,
            a2r * b1r - a2i * b1i + b2r, a2r * b1i + a2i * b1r + b2i)


def s5_mixer(u, h0_re, h0_im, lam_re, lam_im, log_step, b_re, b_im, c_re, c_im, d_skip, w_glu):
    bsz, L, _ = u.shape
    uf = u.astype(F32)
    ug = uf.reshape(bsz, L, S5_GROUPS, S5_CH_PER_GROUP)
    step = jnp.exp(log_step.astype(F32))[:, None]
    lr = lam_re.astype(F32)
    li = lam_im.astype(F32)
    mag = jnp.exp(lr * step)
    ab_re = mag * jnp.cos(li * step)
    ab_im = mag * jnp.sin(li * step)
    den = lr * lr + li * li
    f_re = ((ab_re - 1.0) * lr + ab_im * li) / den
    f_im = (ab_im * lr - (ab_re - 1.0) * li) / den
    br = b_re.astype(F32)
    bi = b_im.astype(F32)
    bb_re = f_re[..., None] * br - f_im[..., None] * bi
    bb_im = f_re[..., None] * bi + f_im[..., None] * br
    bu_re = jnp.einsum('gpc,blgc->blgp', bb_re, ug)
    bu_im = jnp.einsum('gpc,blgc->blgp', bb_im, ug)
    a_re = jnp.broadcast_to(ab_re, (1, L, S5_GROUPS, S5_STATE))
    a_im = jnp.broadcast_to(ab_im, (1, L, S5_GROUPS, S5_STATE))
    acr, aci, hr, hi = lax.associative_scan(_complex_affine_combine, (a_re, a_im, bu_re, bu_im), axis=1)
    h0r = h0_re.astype(F32)[:, None]
    h0i = h0_im.astype(F32)[:, None]
    hr = hr + acr * h0r - aci * h0i
    hi = hi + acr * h0i + aci * h0r
    y = (jnp.einsum('gcp,blgp->blgc', c_re.astype(F32), hr)
         - jnp.einsum('gcp,blgp->blgc', c_im.astype(F32), hi))
    y = y.reshape(bsz, L, S5_WIDTH) + d_skip.astype(F32) * uf
    g = jax.nn.gelu(y)
    out = g * jax.nn.sigmoid(g @ w_glu.astype(F32))
    return out.astype(u.dtype), hr[:, -1].astype(h0_re.dtype), hi[:, -1].astype(h0_im.dtype)


def causal_dwconv(xbc, buf, w, b):
    full = jnp.concatenate([buf.astype(xbc.dtype), xbc], axis=1)
    y = lax.conv_general_dilated(full, w.astype(full.dtype)[:, None, :], window_strides=(1,),
                                 padding='VALID', dimension_numbers=('NWC', 'WIO', 'NWC'),
                                 feature_group_count=M2_CONV_DIM)
    return jax.nn.silu(y + b.astype(y.dtype)), full[:, full.shape[1] - (M2_CONV - 1):]


def ssd_chunked(x, dt, A, B, C, h0):
    bsz, L = x.shape[0], x.shape[1]
    chunk = min(M2_CHUNK, L)
    pad = (-L) % chunk
    if pad:
        padf = lambda t: jnp.pad(t, [(0, 0), (0, pad)] + [(0, 0)] * (t.ndim - 2))
        x, dt, B, C = padf(x), padf(dt), padf(B), padf(C)
    nc = (L + pad) // chunk
    xc = x.reshape(bsz, nc, chunk, M2_GROUPS, M2_HPG, M2_HEAD_DIM)
    dtc = dt.reshape(bsz, nc, chunk, M2_GROUPS, M2_HPG)
    Bc = B.reshape(bsz, nc, chunk, M2_GROUPS, M2_D_STATE)
    Cc = C.reshape(bsz, nc, chunk, M2_GROUPS, M2_D_STATE)
    acs = jnp.cumsum(dtc * A, axis=2)
    seg = acs[:, :, :, None] - acs[:, :, None, :]
    causal = jnp.tril(jnp.ones((chunk, chunk), bool))[:, :, None, None]
    decay = jnp.exp(jnp.where(causal, seg, -jnp.inf))
    cb = jnp.einsum('bclgn,bcsgn->bclsg', Cc, Bc)
    wts = cb[..., None] * decay * dtc[:, :, None]
    y_diag = jnp.einsum('bclsgr,bcsgrp->bclgrp', wts, xc)
    decay_st = jnp.exp(acs[:, :, -1:] - acs) * dtc
    st = jnp.einsum('bclgn,bclgr,bclgrp->bcgrpn', Bc, decay_st, xc)
    chunk_decay = jnp.exp(acs[:, :, -1])

    def step(h, inp):
        st_c, dec_c = inp
        return dec_c[..., None, None] * h + st_c, h

    h_last, h_in = lax.scan(step, h0, (jnp.moveaxis(st, 1, 0), jnp.moveaxis(chunk_decay, 1, 0)))
    h_in = jnp.moveaxis(h_in, 0, 1)
    y_off = jnp.einsum('bclgn,bcgrpn,bclgr->bclgrp', Cc, h_in, jnp.exp(acs))
    y = (y_diag + y_off).reshape(bsz, nc * chunk, M2_GROUPS, M2_HPG, M2_HEAD_DIM)[:, :L]
    return y, h_last


def mamba2_mixer(z, xbc, dt_raw, conv_buf, ssm0, segs, conv_w, conv_b, dt_bias, a_log, d_skip, norm_w):
    bsz, L, _ = z.shape
    xbc, new_buf = causal_dwconv(xbc, conv_buf, conv_w, conv_b)
    xbc = xbc.astype(F32)
    gn = M2_GROUPS * M2_D_STATE
    xs = xbc[..., :M2_WIDTH].reshape(bsz, L, M2_GROUPS, M2_HPG, M2_HEAD_DIM)
    Bm = xbc[..., M2_WIDTH:M2_WIDTH + gn].reshape(bsz, L, M2_GROUPS, M2_D_STATE)
    Cm = xbc[..., M2_WIDTH + gn:].reshape(bsz, L, M2_GROUPS, M2_D_STATE)
    dt = jax.nn.softplus(dt_raw.astype(F32) + dt_bias.astype(F32)).reshape(bsz, L, M2_GROUPS, M2_HPG)
    A = -jnp.exp(a_log.astype(F32)).reshape(M2_GROUPS, M2_HPG)
    h = ssm0.astype(F32).reshape(bsz, M2_GROUPS, M2_HPG, M2_HEAD_DIM, M2_D_STATE)
    ys = []
    start = 0
    for seg_len in segs:
        y_seg, h = ssd_chunked(xs[:, start:start + seg_len], dt[:, start:start + seg_len], A,
                               Bm[:, start:start + seg_len], Cm[:, start:start + seg_len], h)
        ys.append(y_seg)
        start += seg_len
    y = jnp.concatenate(ys, axis=1)
    y = y + d_skip.astype(F32).reshape(M2_GROUPS, M2_HPG)[:, :, None] * xs
    y = y.reshape(bsz, L, M2_WIDTH) * jax.nn.silu(z.astype(F32))
    yg = y.reshape(bsz, L, M2_GROUPS, M2_WIDTH // M2_GROUPS)
    yg = yg * lax.rsqrt(jnp.mean(yg * yg, axis=-1, keepdims=True) + EPS)
    y = yg.reshape(bsz, L, M2_WIDTH) * norm_w.astype(F32)
    h_out = h.reshape(bsz, M2_HEADS, M2_HEAD_DIM, M2_D_STATE).astype(ssm0.dtype)
    return y.astype(z.dtype), h_out, new_buf.astype(conv_buf.dtype)


def peer_ffn(h, w_q, sub_keys, u_tab, v_tab):
    bsz, L, D = h.shape
    T = bsz * L
    xt = h.reshape(T, D)
    q = (xt @ w_q).astype(F32).reshape(T, PEER_HEADS, 2, PEER_HALF)
    s = jnp.einsum('thid,hikd->thik', q, sub_keys.astype(F32))
    sv, si = lax.top_k(s, PEER_TOPK)
    cand = sv[:, :, 0, :, None] + sv[:, :, 1, None, :]
    cv, cpos = lax.top_k(cand.reshape(T, PEER_HEADS, PEER_TOPK * PEER_TOPK), PEER_TOPK)
    i1 = jnp.take_along_axis(si[:, :, 0], cpos // PEER_TOPK, axis=-1)
    i2 = jnp.take_along_axis(si[:, :, 1], cpos % PEER_TOPK, axis=-1)
    idx = i1 * PEER_N_KEYS + i2
    g = jax.nn.softmax(cv, axis=-1)
    pad = (-T) % PEER_BLOCK
    xt_p = jnp.pad(xt, ((0, pad), (0, 0)))
    idx_p = jnp.pad(idx, ((0, pad), (0, 0), (0, 0)))
    g_p = jnp.pad(g, ((0, pad), (0, 0), (0, 0)))
    nb = (T + pad) // PEER_BLOCK

    def block(args):
        xb, ib, gb = args
        u = u_tab[ib]
        act = jax.nn.gelu(jnp.einsum('thkd,td->thk', u, xb).astype(F32))
        v = v_tab[ib]
        return jnp.einsum('thk,thkd->td', (gb * act).astype(v.dtype), v)

    out = lax.map(block, (xt_p.reshape(nb, PEER_BLOCK, D),
                          idx_p.reshape(nb, PEER_BLOCK, PEER_HEADS, PEER_TOPK),
                          g_p.reshape(nb, PEER_BLOCK, PEER_HEADS, PEER_TOPK)))
    return out.reshape(nb * PEER_BLOCK, D)[:T].reshape(bsz, L, D).astype(h.dtype)


def decoder_layer(x, s5_re, s5_im, ssm0, conv_buf, segs, l, prm):
    hn = rmsnorm(x, prm['norm_mix_w'][l])
    proj = hn @ prm['w_in'][l]
    o1 = S5_WIDTH
    o2 = o1 + M2_WIDTH
    o3 = o2 + M2_CONV_DIM
    u = proj[..., :o1]
    z = proj[..., o1:o2]
    xbc = proj[..., o2:o3]
    dt_raw = proj[..., o3:]
    y_s5, s5_re_n, s5_im_n = s5_mixer(u, s5_re, s5_im, prm['s5_lambda_re'][l], prm['s5_lambda_im'][l],
                                      prm['s5_log_step'][l], prm['s5_b_re'][l], prm['s5_b_im'][l],
                                      prm['s5_c_re'][l], prm['s5_c_im'][l], prm['s5_d'][l], prm['s5_w_glu'][l])
    y_s5 = rmsnorm(y_s5, prm['s5_norm_w'][l])
    y_m2, ssm_n, conv_n = mamba2_mixer(z, xbc, dt_raw, conv_buf, ssm0, segs, prm['m2_conv_w'][l],
                                       prm['m2_conv_b'][l], prm['m2_dt_bias'][l], prm['m2_a_log'][l],
                                       prm['m2_d'][l], prm['m2_norm_w'][l])
    x = x + jnp.concatenate([y_s5, y_m2], axis=-1) @ prm['w_out'][l]
    x = x + peer_ffn(rmsnorm(x, prm['norm_ffn_w'][l]), prm['peer_w_q'][l], prm['peer_sub_keys'][l],
                     prm['peer_u'][l], prm['peer_v'][l])
    return x, (s5_re_n, s5_im_n, ssm_n, conv_n)


def run_trunk(x, init_states, segs, prm):
    bsz = x.shape[0]
    new = ([], [], [], [])
    for l in range(DEPTH):
        if init_states is None:
            s5r = jnp.zeros((bsz, S5_GROUPS, S5_STATE), x.dtype)
            s5i = jnp.zeros((bsz, S5_GROUPS, S5_STATE), x.dtype)
            ssm0 = jnp.zeros((bsz, M2_HEADS, M2_HEAD_DIM, M2_D_STATE), x.dtype)
            cbuf = jnp.zeros((bsz, M2_CONV - 1, M2_CONV_DIM), x.dtype)
        else:
            s5r = init_states[0][l]
            s5i = init_states[1][l]
            ssm0 = init_states[2][l]
            cbuf = init_states[3][l]
        x, st = decoder_layer(x, s5r, s5i, ssm0, cbuf, segs, l, prm)
        for lst, s in zip(new, st):
            lst.append(s)
    y = rmsnorm(x, prm['final_norm_w'])
    return y, [jnp.stack(lst) for lst in new]


def setup_inputs(seed: int = 0) -> dict:
    key = jax.random.key(seed)
    ks = jax.random.split(key, 32)

    def nrm(k, shape, scale):
        return scale * jax.random.normal(k, shape, F32)

    n_idx = jnp.arange(S5_STATE, dtype=F32)
    gp = (DEPTH, S5_GROUPS, S5_STATE)
    dt0 = jnp.exp(jax.random.uniform(ks[17], (DEPTH, M2_HEADS), F32, math.log(1e-3), math.log(1e-1)))
    return {
        'x_prompt': nrm(ks[0], (BATCH, SEQ, D_MODEL), 1.0),
        'x_sample': nrm(ks[1], (DEC_BATCH, DEC_SEQ, D_MODEL), 1.0),
        'state_s5_re': nrm(ks[2], (DEPTH, DEC_BATCH, S5_GROUPS, S5_STATE), 0.1),
        'state_s5_im': nrm(ks[3], (DEPTH, DEC_BATCH, S5_GROUPS, S5_STATE), 0.1),
        'state_ssm': nrm(ks[4], (DEPTH, DEC_BATCH, M2_HEADS, M2_HEAD_DIM, M2_D_STATE), 0.1),
        'state_conv': nrm(ks[5], (DEPTH, DEC_BATCH, M2_CONV - 1, M2_CONV_DIM), 1.0),
        'meta_tokens': nrm(ks[6], (N_META, D_MODEL), 1.0),
        'norm_mix_w': 1.0 + nrm(ks[7], (DEPTH, D_MODEL), 0.02),
        'w_in': nrm(ks[8], (DEPTH, D_MODEL, IN_PROJ), D_MODEL ** -0.5),
        's5_lambda_re': -0.5 * jnp.exp(nrm(ks[9], gp, 0.05)),
        's5_lambda_im': jnp.pi * n_idx + nrm(ks[10], gp, 0.01),
        's5_log_step': jax.random.uniform(ks[11], (DEPTH, S5_GROUPS), F32, math.log(1e-3), math.log(1e-1)),
        's5_b_re': nrm(ks[12], (DEPTH, S5_GROUPS, S5_STATE, S5_CH_PER_GROUP), (2 * S5_CH_PER_GROUP) ** -0.5),
        's5_b_im': nrm(ks[13], (DEPTH, S5_GROUPS, S5_STATE, S5_CH_PER_GROUP), (2 * S5_CH_PER_GROUP) ** -0.5),
        's5_c_re': nrm(ks[14], (DEPTH, S5_GROUPS, S5_CH_PER_GROUP, S5_STATE), (2 * S5_STATE) ** -0.5),
        's5_c_im': nrm(ks[15], (DEPTH, S5_GROUPS, S5_CH_PER_GROUP, S5_STATE), (2 * S5_STATE) ** -0.5),
        's5_d': nrm(ks[16], (DEPTH, S5_WIDTH), 1.0),
        's5_w_glu': nrm(ks[18], (DEPTH, S5_WIDTH, S5_WIDTH), S5_WIDTH ** -0.5),
        's5_norm_w': 1.0 + nrm(ks[19], (DEPTH, S5_WIDTH), 0.02),
        'm2_conv_w': nrm(ks[20], (DEPTH, M2_CONV, M2_CONV_DIM), 0.5),
        'm2_conv_b': nrm(ks[21], (DEPTH, M2_CONV_DIM), 0.01),
        'm2_dt_bias': dt0 + jnp.log(-jnp.expm1(-dt0)),
        'm2_a_log': jnp.log(jax.random.uniform(ks[22], (DEPTH, M2_HEADS), F32, 1.0, 16.0)),
        'm2_d': 1.0 + nrm(ks[23], (DEPTH, M2_HEADS), 0.1),
        'm2_norm_w': 1.0 + nrm(ks[24], (DEPTH, M2_WIDTH), 0.02),
        'w_out': nrm(ks[25], (DEPTH, MIX_WIDTH, D_MODEL), MIX_WIDTH ** -0.5),
        'norm_ffn_w': 1.0 + nrm(ks[26], (DEPTH, D_MODEL), 0.02),
        'peer_w_q': nrm(ks[27], (DEPTH, D_MODEL, PEER_HEADS * PEER_KEY_DIM), D_MODEL ** -0.5),
        'peer_sub_keys': nrm(ks[28], (DEPTH, PEER_HEADS, 2, PEER_N_KEYS, PEER_HALF), PEER_HALF ** -0.5),
        'peer_u': nrm(ks[29], (DEPTH, PEER_EXPERTS, D_MODEL), D_MODEL ** -0.5),
        'peer_v': nrm(ks[30], (DEPTH, PEER_EXPERTS, D_MODEL), PEER_HEADS ** -0.5),
        'final_norm_w': 1.0 + nrm(ks[31], (D_MODEL,), 0.02),
    }


def reference(x_prompt, x_sample, state_s5_re, state_s5_im, state_ssm, state_conv, meta_tokens,
              norm_mix_w, w_in, s5_lambda_re, s5_lambda_im, s5_log_step, s5_b_re, s5_b_im, s5_c_re,
              s5_c_im, s5_d, s5_w_glu, s5_norm_w, m2_conv_w, m2_conv_b, m2_dt_bias, m2_a_log, m2_d,
              m2_norm_w, w_out, norm_ffn_w, peer_w_q, peer_sub_keys, peer_u, peer_v, final_norm_w):
    prm = {
        'norm_mix_w': norm_mix_w, 'w_in': w_in, 's5_lambda_re': s5_lambda_re, 's5_lambda_im': s5_lambda_im,
        's5_log_step': s5_log_step, 's5_b_re': s5_b_re, 's5_b_im': s5_b_im, 's5_c_re': s5_c_re,
        's5_c_im': s5_c_im, 's5_d': s5_d, 's5_w_glu': s5_w_glu, 's5_norm_w': s5_norm_w,
        'm2_conv_w': m2_conv_w, 'm2_conv_b': m2_conv_b, 'm2_dt_bias': m2_dt_bias, 'm2_a_log': m2_a_log,
        'm2_d': m2_d, 'm2_norm_w': m2_norm_w, 'w_out': w_out, 'norm_ffn_w': norm_ffn_w,
        'peer_w_q': peer_w_q, 'peer_sub_keys': peer_sub_keys, 'peer_u': peer_u, 'peer_v': peer_v,
        'final_norm_w': final_norm_w,
    }
    bp, sp = x_prompt.shape[0], x_prompt.shape[1]
    meta = jnp.broadcast_to(meta_tokens.astype(x_prompt.dtype)[None], (bp, N_META, D_MODEL))
    xp = jnp.concatenate([meta, x_prompt], axis=1)
    yp, pst = run_trunk(xp, None, (N_META, sp), prm)
    y_prompt = yp[:, N_META:]
    y_sample, sst = run_trunk(x_sample, (state_s5_re, state_s5_im, state_ssm, state_conv),
                              (x_sample.shape[1],), prm)
    return (y_prompt, y_sample, pst[0], pst[1], pst[2], pst[3], sst[0], sst[1], sst[2], sst[3])
```

```python
import functools
import math

import jax
import jax.numpy as jnp
from jax import lax
from jax.experimental import pallas as pl
from jax.experimental.pallas import tpu as pltpu

F32 = jnp.float32
BF16 = jnp.bfloat16
HIGHEST = lax.Precision.HIGHEST

D_MODEL = 1024
N_META = 16
S5_WIDTH = 1024
S5_CH = 16
S5_GROUPS = S5_WIDTH // S5_CH
S5_STATE = 64
M2_WIDTH = 1024
M2_HEAD_DIM = 64
M2_HEADS = M2_WIDTH // M2_HEAD_DIM
M2_GROUPS = 2
M2_D_STATE = 128
M2_CONV = 4
M2_CONV_DIM = M2_WIDTH + 2 * M2_GROUPS * M2_D_STATE
PEER_HEADS = 8
PEER_N_KEYS = 128
PEER_EXPERTS = PEER_N_KEYS * PEER_N_KEYS
PEER_HALF = 128
PEER_TOPK = 16
EPS = 1e-6

LANES = 128
SSD_CHUNK = 128
CONV_HALO = 8
VMEM_LIMIT = 56 * 1024 * 1024


def _cparams(*sem):
    return pltpu.CompilerParams(dimension_semantics=sem, vmem_limit_bytes=VMEM_LIMIT)


def _sigmoid(x):
    return 1.0 / (1.0 + jnp.exp(-x))


def _gelu_tanh(x):
    c = math.sqrt(2.0 / math.pi)
    return 0.5 * x * (1.0 + jnp.tanh(c * (x + 0.044715 * (x * x * x))))


def _rms(x, w):
    return x * lax.rsqrt(jnp.mean(x * x, axis=-1, keepdims=True) + EPS) * w


def _bdot(a, b):
    return jnp.dot(a.astype(BF16), b.astype(BF16), preferred_element_type=F32)


def _in_proj_kernel(x_ref, nw_ref, wu_ref, wz_ref, wx_ref, wd_ref, u_ref, z_ref, xbc_ref, dt_ref):
    hb = _rms(x_ref[...], nw_ref[...]).astype(BF16)
    u_ref[...] = jnp.dot(hb, wu_ref[...], preferred_element_type=F32)
    z_ref[...] = jnp.dot(hb, wz_ref[...], preferred_element_type=F32)
    xbc_ref[...] = jnp.dot(hb, wx_ref[...], preferred_element_type=F32)
    dt_ref[...] = jnp.dot(hb, wd_ref[...], preferred_element_type=F32)


def _in_proj(x, nw, wu, wz, wx, wd, tm):
    t = x.shape[0]
    row = lambda i: (i, 0)
    fix = lambda i: (0, 0)
    widths = (S5_WIDTH, M2_WIDTH, M2_CONV_DIM, LANES)
    return pl.pallas_call(
        _in_proj_kernel,
        grid=(t // tm,),
        in_specs=[pl.BlockSpec((tm, D_MODEL), row), pl.BlockSpec((1, D_MODEL), fix)]
        + [pl.BlockSpec((D_MODEL, w), fix) for w in widths],
        out_specs=[pl.BlockSpec((tm, w), row) for w in widths],
        out_shape=[jax.ShapeDtypeStruct((t, w), F32) for w in widths],
        compiler_params=_cparams("parallel"),
        name="in_proj",
    )(x, nw, wu, wz, wx, wd)


def _s5_prep_kernel(lrc_ref, lic_ref, lrr_ref, lir_ref, ls_ref, btr_ref, bti_ref, ctr_ref, cti_ref,
                    kt_ref, wout_ref, wsr_ref, wsi_ref, al_ref, *, lc):
    k = lc * S5_CH
    step = jnp.exp(ls_ref[0])

    def disc(lr, li):
        lg = lr * step
        th = li * step
        mag = jnp.exp(lg)
        ab_re = mag * jnp.cos(th)
        ab_im = mag * jnp.sin(th)
        den = lr * lr + li * li
        f_re = ((ab_re - 1.0) * lr + ab_im * li) / den
        f_im = (ab_im * lr - (ab_re - 1.0) * li) / den
        return lg, th, f_re, f_im

    lg, th, f_re, f_im = disc(lrc_ref[0], lic_ref[0])
    tau = lax.shift_right_logical(lax.broadcasted_iota(jnp.int32, (S5_STATE, k), 1), 4).astype(F32)
    c_re = ctr_ref[0]
    c_im = cti_ref[0]

    def powers(t, lg_, th_):
        mag = jnp.exp(lg_ * t)
        return mag * jnp.cos(th_ * t), mag * jnp.sin(th_ * t)

    p_re, p_im = powers(tau, lg, th)
    fa_re = f_re * p_re - f_im * p_im
    fa_im = f_re * p_im + f_im * p_re
    fca_re = fa_re * c_re - fa_im * c_im
    fca_im = fa_re * c_im + fa_im * c_re
    kt_ref[0] = (jnp.dot(btr_ref[0], fca_re, precision=HIGHEST, preferred_element_type=F32)
                 - jnp.dot(bti_ref[0], fca_im, precision=HIGHEST, preferred_element_type=F32))
    q_re, q_im = powers(tau + 1.0, lg, th)
    wout_ref[0, 0:S5_STATE, :] = q_re * c_re - q_im * c_im
    wout_ref[0, S5_STATE:2 * S5_STATE, :] = -(q_re * c_im + q_im * c_re)

    lgr, thr, fr_re, fr_im = disc(lrr_ref[0], lir_ref[0])
    kk = lax.broadcasted_iota(jnp.int32, (lc, S5_STATE), 0).astype(F32)
    r_re, r_im = powers(kk, lgr, thr)
    g_re = fr_re * r_re - fr_im * r_im
    g_im = fr_re * r_im + fr_im * r_re
    bt_re = btr_ref[0]
    bt_im = bti_ref[0]
    for s in range(lc):
        kpow = lc - 1 - s
        w_re = g_re[kpow:kpow + 1, :]
        w_im = g_im[kpow:kpow + 1, :]
        wsr_ref[0, s * S5_CH:(s + 1) * S5_CH, :] = bt_re * w_re - bt_im * w_im
        wsi_ref[0, s * S5_CH:(s + 1) * S5_CH, :] = bt_re * w_im + bt_im * w_re
    a_re, a_im = powers(float(lc), lgr, thr)
    al_ref[0, 0:1, :] = a_re
    al_ref[0, 1:2, :] = a_im


def _s5_prep(lam_re, lam_im, log_step, b_re, b_im, c_re, c_im, lc):
    g, p, k = S5_GROUPS, S5_STATE, lc * S5_CH
    spec3 = lambda a, b: pl.BlockSpec((1, a, b), lambda i: (i, 0, 0))
    ins = [lam_re.reshape(g, p, 1), lam_im.reshape(g, p, 1), lam_re.reshape(g, 1, p), lam_im.reshape(g, 1, p),
           log_step.reshape(g, 1, 1),
           jnp.swapaxes(b_re, 1, 2), jnp.swapaxes(b_im, 1, 2),
           jnp.tile(jnp.swapaxes(c_re, 1, 2), (1, 1, lc)), jnp.tile(jnp.swapaxes(c_im, 1, 2), (1, 1, lc))]
    kt, wout, wsr, wsi, al = pl.pallas_call(
        functools.partial(_s5_prep_kernel, lc=lc),
        grid=(g,),
        in_specs=[spec3(p, 1), spec3(p, 1), spec3(1, p), spec3(1, p), spec3(1, 1),
                  spec3(S5_CH, p), spec3(S5_CH, p), spec3(p, k), spec3(p, k)],
        out_specs=[spec3(S5_CH, k), spec3(2 * p, k), spec3(k, p), spec3(k, p), spec3(2, p)],
        out_shape=[jax.ShapeDtypeStruct((g, S5_CH, k), F32), jax.ShapeDtypeStruct((g, 2 * p, k), F32),
                   jax.ShapeDtypeStruct((g, k, p), F32), jax.ShapeDtypeStruct((g, k, p), F32),
                   jax.ShapeDtypeStruct((g, 2, p), F32)],
        compiler_params=_cparams("parallel"),
        name="s5_prep",
    )(*ins)
    lag = jnp.arange(lc)[None, :] - jnp.arange(lc)[:, None]
    kt4 = kt.reshape(g, S5_CH, lc, S5_CH)
    toep = kt4[:, :, jnp.maximum(lag, 0), :]
    toep = jnp.where((lag >= 0)[None, None, :, :, None], toep, 0.0)
    toep = toep.transpose(0, 2, 1, 3, 4).reshape(g, k, k)
    wst = jnp.concatenate([wsr, wsi, wsi, wsr], axis=-1)
    a_re, a_im = al[:, 0:1, :], al[:, 1:2, :]
    acoef = jnp.concatenate([jnp.concatenate([a_re, a_re], -1), jnp.concatenate([-a_im, a_im], -1),
                             jnp.concatenate([a_im, -a_im], -1)], axis=1)
    return toep.astype(BF16), wst.astype(BF16), wout.astype(BF16), acoef


def _s5_kernel(u_ref, h0_ref, h0s_ref, wst_ref, toep_ref, wout_ref, a_ref, y_ref, hf_ref, s_scr, hin_scr,
               *, n_chunks, rows):
    ub = u_ref[0].astype(BF16)
    s_scr[...] = jnp.dot(ub, wst_ref[0], preferred_element_type=F32)
    a1 = a_ref[0, 0:1, :]
    a2 = a_ref[0, 1:2, :]
    a2s = a_ref[0, 2:3, :]

    def body(n, carry):
        h, hs = carry
        r0 = pl.multiple_of(n * rows, rows)
        hin_scr[pl.ds(r0, rows), :] = h
        s = s_scr[pl.ds(r0, rows), :]
        return (a1 * h + a2 * hs + s[:, :LANES], a1 * hs + a2s * h + s[:, LANES:])

    h, _ = lax.fori_loop(0, n_chunks, body, (h0_ref[0], h0s_ref[0]))
    hf_ref[0] = h
    y_ref[0] = (jnp.dot(ub, toep_ref[0], preferred_element_type=F32)
                + jnp.dot(hin_scr[...].astype(BF16), wout_ref[0], preferred_element_type=F32))


def _s5(ur, h0, ops, n_chunks, rows):
    toep, wst, wout, acoef = ops
    g, nr, k = ur.shape
    spec3 = lambda a, b: pl.BlockSpec((1, a, b), lambda i: (i, 0, 0))
    return pl.pallas_call(
        functools.partial(_s5_kernel, n_chunks=n_chunks, rows=rows),
        grid=(g,),
        in_specs=[spec3(nr, k), spec3(rows, LANES), spec3(rows, LANES), spec3(k, 2 * LANES), spec3(k, k),
                  spec3(LANES, k), spec3(3, LANES)],
        out_specs=[spec3(nr, k), spec3(rows, LANES)],
        out_shape=[jax.ShapeDtypeStruct((g, nr, k), F32), jax.ShapeDtypeStruct((g, rows, LANES), F32)],
        scratch_shapes=[pltpu.VMEM((nr, 2 * LANES), F32), pltpu.VMEM((nr, LANES), F32)],
        compiler_params=_cparams("parallel"),
        name="s5_scan",
    )(ur, h0, jnp.roll(h0, S5_STATE, axis=-1), wst, toep, wout, acoef)


def _ssd_kernel(xbc_ref, z_ref, dt_ref, conv0_ref, h0_ref, cw_ref, cb_ref, dtb_ref, alog_ref, dexp_ref, nw_ref,
                y_ref, convn_ref, hn_ref, buf_scr, dt_scr, h_scr, *, q_in, n_chunks):
    q = SSD_CHUNK
    pad = q - q_in
    first = CONV_HALO + pad
    c = pl.program_id(1)

    @pl.when(c == 0)
    def _():
        buf_scr[0:first, :] = jnp.zeros((first, M2_CONV_DIM), F32)
        buf_scr[first - (M2_CONV - 1):first, :] = conv0_ref[0]
        h_scr[...] = h0_ref[0]

    buf_scr[first:CONV_HALO + q, :] = xbc_ref[...]
    conv = cb_ref[...]
    for kk in range(M2_CONV):
        lo = CONV_HALO - (M2_CONV - 1) + kk
        conv = conv + cw_ref[kk:kk + 1, :] * buf_scr[lo:lo + q, :]
    convn_ref[0] = buf_scr[CONV_HALO + q - (M2_CONV - 1):CONV_HALO + q, :]
    if n_chunks > 1:
        buf_scr[0:CONV_HALO, :] = buf_scr[q:q + CONV_HALO, :]
    act = conv * _sigmoid(conv)
    xs = act[:, :M2_WIDTH]

    dtv = dt_ref[...] + dtb_ref[...]
    dt_real = jnp.maximum(dtv, 0.0) + jnp.log1p(jnp.exp(-jnp.abs(dtv)))
    if pad:
        dt_scr[0:pad, :] = jnp.zeros((pad, LANES), F32)
        dt_scr[pad:q, :] = dt_real
        dt = dt_scr[...]
    else:
        dt = dt_real
    a_neg = -jnp.exp(alog_ref[...])
    ri = lax.broadcasted_iota(jnp.int32, (q, q), 0)
    ci = lax.broadcasted_iota(jnp.int32, (q, q), 1)
    tril = ri >= ci
    acs = jnp.dot(tril.astype(F32), dt * a_neg, precision=HIGHEST, preferred_element_type=F32)
    acs_t = acs.T
    dt_t = dt.T
    acs_last = acs[q - 1:q, :]
    eacs = jnp.exp(acs)
    wdec = jnp.exp(acs_last - acs) * dt
    dec_last = jnp.exp(acs_last)
    low = lax.broadcasted_iota(jnp.int32, (q, LANES), 1) < M2_HEAD_DIM
    low1 = low[0:1, :]

    def pair_cols(m, h0):
        return jnp.where(low[0:m.shape[0], :], m[:, h0:h0 + 1], m[:, h0 + 1:h0 + 2])

    y_parts = []
    hpg = M2_HEADS // M2_GROUPS
    gw = hpg * M2_HEAD_DIM
    for g in range(M2_GROUPS):
        bg = act[:, M2_WIDTH + g * M2_D_STATE:M2_WIDTH + (g + 1) * M2_D_STATE]
        cg = act[:, M2_WIDTH + (M2_GROUPS + g) * M2_D_STATE:M2_WIDTH + (M2_GROUPS + g + 1) * M2_D_STATE]
        cgb = cg.astype(BF16)
        cb = lax.dot_general(cgb, bg.astype(BF16), (((1,), (1,)), ((), ())), preferred_element_type=F32)
        hg = h_scr[:, g * gw:(g + 1) * gw]
        yoff = jnp.dot(cgb, hg.astype(BF16), preferred_element_type=F32)
        xw_parts = []
        dec_parts = []
        for pp in range(hpg // 2):
            h0 = g * hpg + 2 * pp
            lanes = slice(h0 * M2_HEAD_DIM, (h0 + 2) * M2_HEAD_DIM)
            w_pair = []
            for h in (h0, h0 + 1):
                seg = acs[:, h:h + 1] - acs_t[h:h + 1, :]
                dec = jnp.exp(jnp.where(tril, seg, -jnp.inf))
                w_pair.append((cb * dec * dt_t[h:h + 1, :]).astype(BF16))
            xp = xs[:, lanes]
            xbd = jnp.concatenate([jnp.where(low, xp, 0.0), jnp.where(low, 0.0, xp)], axis=0).astype(BF16)
            yd = jnp.dot(jnp.concatenate(w_pair, axis=1), xbd, preferred_element_type=F32)
            yo = yoff[:, 2 * pp * M2_HEAD_DIM:(2 * pp + 2) * M2_HEAD_DIM]
            y_parts.append(yd + yo * pair_cols(eacs, h0) + dexp_ref[:, lanes] * xp)
            xw_parts.append(xp * pair_cols(wdec, h0))
            dec_parts.append(jnp.where(low1, dec_last[:, h0:h0 + 1], dec_last[:, h0 + 1:h0 + 2]))
        xw = jnp.concatenate(xw_parts, axis=1).astype(BF16)
        st = jnp.dot(bg.T.astype(BF16), xw, preferred_element_type=F32)
        h_scr[:, g * gw:(g + 1) * gw] = hg * jnp.concatenate(dec_parts, axis=1) + st

    @pl.when(c == n_chunks - 1)
    def _():
        hn_ref[0] = h_scr[...]

    y = jnp.concatenate(y_parts, axis=1)[pad:, :]
    zz = z_ref[...]
    y = y * (zz * _sigmoid(zz))
    outs = []
    for g in range(M2_GROUPS):
        yg = y[:, g * gw:(g + 1) * gw]
        outs.append(yg * lax.rsqrt(jnp.mean(yg * yg, axis=-1, keepdims=True) + EPS))
    y_ref[...] = jnp.concatenate(outs, axis=1) * nw_ref[...]


def _ssd(xbc, z, dt, conv0, h0, prm, n_seq, n_chunks, q_in, shared_init):
    assert q_in == SSD_CHUNK or n_chunks == 1
    assert q_in >= CONV_HALO and q_in % CONV_HALO == 0
    t = xbc.shape[0]
    row = lambda b, c: (b * n_chunks + c, 0)
    fix = lambda b, c: (0, 0)
    init = (lambda b, c: (0, 0, 0)) if shared_init else (lambda b, c: (b, 0, 0))
    per_seq = lambda b, c: (b, 0, 0)
    return pl.pallas_call(
        functools.partial(_ssd_kernel, q_in=q_in, n_chunks=n_chunks),
        grid=(n_seq, n_chunks),
        in_specs=[pl.BlockSpec((q_in, M2_CONV_DIM), row), pl.BlockSpec((q_in, M2_WIDTH), row),
                  pl.BlockSpec((q_in, LANES), row),
                  pl.BlockSpec((1, M2_CONV - 1, M2_CONV_DIM), init),
                  pl.BlockSpec((1, M2_D_STATE, M2_WIDTH), init),
                  pl.BlockSpec((M2_CONV, M2_CONV_DIM), fix), pl.BlockSpec((1, M2_CONV_DIM), fix),
                  pl.BlockSpec((1, LANES), fix), pl.BlockSpec((1, LANES), fix),
                  pl.BlockSpec((1, M2_WIDTH), fix), pl.BlockSpec((1, M2_WIDTH), fix)],
        out_specs=[pl.BlockSpec((q_in, M2_WIDTH), row),
                   pl.BlockSpec((1, M2_CONV - 1, M2_CONV_DIM), per_seq),
                   pl.BlockSpec((1, M2_D_STATE, M2_WIDTH), per_seq)],
        out_shape=[jax.ShapeDtypeStruct((t, M2_WIDTH), F32),
                   jax.ShapeDtypeStruct((n_seq, M2_CONV - 1, M2_CONV_DIM), F32),
                   jax.ShapeDtypeStruct((n_seq, M2_D_STATE, M2_WIDTH), F32)],
        scratch_shapes=[pltpu.VMEM((CONV_HALO + SSD_CHUNK, M2_CONV_DIM), F32),
                        pltpu.VMEM((SSD_CHUNK, LANES), F32),
                        pltpu.VMEM((M2_D_STATE, M2_WIDTH), F32)],
        compiler_params=_cparams("parallel", "arbitrary"),
        name="ssd",
    )(xbc, z, dt, conv0, h0, *prm)


def _post_kernel(y5_ref, u_ref, ym_ref, x_ref, d_ref, wglu_ref, s5nw_ref, wo1_ref, wo2_ref, fnw_ref, wqt_ref,
                 sk_ref, x1_ref, xt_ref, st_ref):
    g = _gelu_tanh(y5_ref[...] + d_ref[...] * u_ref[...])
    o = g * _sigmoid(jnp.dot(g.astype(BF16), wglu_ref[...], preferred_element_type=F32))
    y5n = _rms(o, s5nw_ref[...])
    x1 = (x_ref[...] + jnp.dot(y5n.astype(BF16), wo1_ref[...], preferred_element_type=F32)
          + jnp.dot(ym_ref[...].astype(BF16), wo2_ref[...], preferred_element_type=F32))
    x1_ref[...] = x1
    hn_t = _rms(x1, fnw_ref[...]).T.astype(BF16)
    xt_ref[...] = hn_t
    q_t = jnp.dot(wqt_ref[...], hn_t, preferred_element_type=F32)
    for k in range(2 * PEER_HEADS):
        qk = q_t[k * PEER_HALF:(k + 1) * PEER_HALF, :].astype(BF16)
        st_ref[k] = jnp.dot(sk_ref[k], qk, preferred_element_type=F32)


def _post(y5, u, ym, x, prm, tm):
    t = x.shape[0]
    row = lambda i: (i, 0)
    fix = lambda i: (0, 0)
    nk = 2 * PEER_HEADS
    return pl.pallas_call(
        _post_kernel,
        grid=(t // tm,),
        in_specs=[pl.BlockSpec((tm, D_MODEL), row)] * 4
        + [pl.BlockSpec((1, D_MODEL), fix), pl.BlockSpec((S5_WIDTH, S5_WIDTH), fix), pl.BlockSpec((1, D_MODEL), fix),
           pl.BlockSpec((S5_WIDTH, D_MODEL), fix), pl.BlockSpec((M2_WIDTH, D_MODEL), fix),
           pl.BlockSpec((1, D_MODEL), fix), pl.BlockSpec((nk * PEER_HALF, D_MODEL), fix),
           pl.BlockSpec((nk, PEER_N_KEYS, PEER_HALF), lambda i: (0, 0, 0))],
        out_specs=[pl.BlockSpec((tm, D_MODEL), row), pl.BlockSpec((D_MODEL, tm), lambda i: (0, i)),
                   pl.BlockSpec((nk, PEER_N_KEYS, tm), lambda i: (0, 0, i))],
        out_shape=[jax.ShapeDtypeStruct((t, D_MODEL), F32), jax.ShapeDtypeStruct((D_MODEL, t), BF16),
                   jax.ShapeDtypeStruct((nk, PEER_N_KEYS, t), F32)],
        compiler_params=_cparams("parallel"),
        name="post",
    )(y5, u, ym, x, *prm)


def _top_values(s):
    rows = lax.broadcasted_iota(jnp.int32, (PEER_TOPK, s.shape[1]), 0)
    out = jnp.zeros((PEER_TOPK, s.shape[1]), F32)
    cur = s
    for r in range(PEER_TOPK):
        mx = jnp.max(cur, axis=0, keepdims=True)
        out = jnp.where(rows == r, mx, out)
        cur = jnp.where(cur == mx, -jnp.inf, cur)
    return out


def _route_kernel(st_ref, s1n_ref, s2n_ref, thr_ref):
    def head(h, carry):
        s1 = st_ref[2 * h]
        s2 = st_ref[2 * h + 1]
        v1 = _top_values(s1)
        v2 = _top_values(s2)
        cand = jnp.concatenate([v1[a:a + 1, :] + v2 for a in range(PEER_TOPK)], axis=0)
        cur = cand
        theta = None
        for _ in range(PEER_TOPK):
            theta = jnp.max(cur, axis=0, keepdims=True)
            cur = jnp.where(cur == theta, -jnp.inf, cur)
        sel = cand >= theta
        m = v1[0:1, :] + v2[0:1, :]
        zsum = jnp.sum(jnp.where(sel, jnp.exp(cand - m), 0.0), axis=0, keepdims=True)
        off = m + jnp.log(zsum)
        s1n_ref[h] = s1 - off
        s2n_ref[h] = s2
        v1n = v1 - off
        candn = jnp.concatenate([v1n[a:a + 1, :] + v2 for a in range(PEER_TOPK)], axis=0)
        thr_ref[h] = jnp.min(jnp.where(sel, candn, jnp.inf), axis=0, keepdims=True)
        return carry

    lax.fori_loop(0, PEER_HEADS, head, 0)


def _route(st, tl):
    nk, keys, t = st.shape
    spec = lambda n: pl.BlockSpec((n, keys, tl), lambda i: (0, 0, i))
    return pl.pallas_call(
        _route_kernel,
        grid=(t // tl,),
        in_specs=[spec(nk)],
        out_specs=[spec(PEER_HEADS), spec(PEER_HEADS), pl.BlockSpec((PEER_HEADS, 1, tl), lambda i: (0, 0, i))],
        out_shape=[jax.ShapeDtypeStruct((PEER_HEADS, keys, t), F32), jax.ShapeDtypeStruct((PEER_HEADS, keys, t), F32),
                   jax.ShapeDtypeStruct((PEER_HEADS, 1, t), F32)],
        compiler_params=_cparams("parallel"),
        name="route",
    )(st)


PEER_E1_BLK = 8


def _peer_kernel(u_ref, vt_ref, xt_ref, s2n_ref, s1n_ref, thr_ref, x1_ref, fnw_ref, y_ref, a_scr, wg_scr, acc_scr,
                 *, tm):
    k = pl.program_id(1)

    @pl.when(k == 0)
    def _():
        acc_scr[...] = jnp.zeros_like(acc_scr)

    a_scr[...] = jnp.dot(u_ref[...], xt_ref[...], preferred_element_type=F32)
    for r in range(PEER_E1_BLK):
        rows = slice(r * PEER_N_KEYS, (r + 1) * PEER_N_KEYS)
        for c in range(tm // LANES):
            cols = slice(c * LANES, (c + 1) * LANES)
            w = jnp.zeros((PEER_N_KEYS, LANES), F32)
            for h in range(PEER_HEADS):
                arg = s2n_ref[h, :, cols] + s1n_ref[h, r:r + 1, cols]
                w = w + jnp.where(arg >= thr_ref[h, :, cols], jnp.exp(arg), 0.0)
            wg_scr[rows, cols] = (w * _gelu_tanh(a_scr[rows, cols])).astype(BF16)
    acc_scr[...] += jnp.dot(vt_ref[...], wg_scr[...], preferred_element_type=F32)

    @pl.when(k == pl.num_programs(1) - 1)
    def _():
        y_ref[...] = _rms(x1_ref[...] + acc_scr[...].T, fnw_ref[...])


def _peer(u_bf, vt_bf, xt, s2n, s1n, thr, x1, fnw, tm):
    t = x1.shape[0]
    eb = PEER_E1_BLK * PEER_N_KEYS
    return pl.pallas_call(
        functools.partial(_peer_kernel, tm=tm),
        grid=(t // tm, PEER_N_KEYS // PEER_E1_BLK),
        in_specs=[pl.BlockSpec((eb, D_MODEL), lambda j, k: (k, 0)),
                  pl.BlockSpec((D_MODEL, eb), lambda j, k: (0, k)),
                  pl.BlockSpec((D_MODEL, tm), lambda j, k: (0, j)),
                  pl.BlockSpec((PEER_HEADS, PEER_N_KEYS, tm), lambda j, k: (0, 0, j)),
                  pl.BlockSpec((PEER_HEADS, PEER_E1_BLK, tm), lambda j, k: (0, k, j)),
                  pl.BlockSpec((PEER_HEADS, 1, tm), lambda j, k: (0, 0, j)),
                  pl.BlockSpec((tm, D_MODEL), lambda j, k: (j, 0)),
                  pl.BlockSpec((1, D_MODEL), lambda j, k: (0, 0))],
        out_specs=pl.BlockSpec((tm, D_MODEL), lambda j, k: (j, 0)),
        out_shape=jax.ShapeDtypeStruct((t, D_MODEL), F32),
        scratch_shapes=[pltpu.VMEM((eb, tm), F32), pltpu.VMEM((eb, tm), BF16), pltpu.VMEM((D_MODEL, tm), F32)],
        compiler_params=_cparams("parallel", "arbitrary"),
        name="peer",
    )(u_bf, vt_bf, xt, s2n, s1n, thr, x1, fnw)


def _tokens_tail(x, in_w, post_prm, peer_prm, y5, u, ym):
    t = x.shape[0]
    x1, xt, st = _post(y5, u, ym, x, post_prm, tm=256)
    s1n, s2n, thr = _route(st, tl=LANES)
    u_bf, vt_bf, fnw = peer_prm
    return _peer(u_bf, vt_bf, xt, s2n, s1n, thr, x1, fnw, tm=512)


def kernel(x_prompt, x_sample, state_s5_re, state_s5_im, state_ssm, state_conv, meta_tokens, norm_mix_w, w_in,
           s5_lambda_re, s5_lambda_im, s5_log_step, s5_b_re, s5_b_im, s5_c_re, s5_c_im, s5_d, s5_w_glu, s5_norm_w,
           m2_conv_w, m2_conv_b, m2_dt_bias, m2_a_log, m2_d, m2_norm_w, w_out, norm_ffn_w, peer_w_q, peer_sub_keys,
           peer_u, peer_v, final_norm_w):
    bp, sp, _ = x_prompt.shape
    bs, ss, _ = x_sample.shape
    g, p = S5_GROUPS, S5_STATE

    w = w_in[0]
    o1, o2, o3 = S5_WIDTH, S5_WIDTH + M2_WIDTH, S5_WIDTH + M2_WIDTH + M2_CONV_DIM
    wu, wz, wx = w[:, :o1].astype(BF16), w[:, o1:o2].astype(BF16), w[:, o2:o3].astype(BF16)
    wd = jnp.pad(w[:, o3:], ((0, 0), (0, LANES - M2_HEADS))).astype(BF16)
    nmw = norm_mix_w[0][None, :]
    pad_h = lambda v: jnp.pad(v, (0, LANES - M2_HEADS))[None, :]
    ssd_prm = (m2_conv_w[0], m2_conv_b[0][None, :], pad_h(m2_dt_bias[0]), pad_h(m2_a_log[0]),
               jnp.repeat(m2_d[0], M2_HEAD_DIM)[None, :], m2_norm_w[0][None, :])
    post_prm = (s5_d[0][None, :], s5_w_glu[0].astype(BF16), s5_norm_w[0][None, :],
                w_out[0][:S5_WIDTH].astype(BF16), w_out[0][S5_WIDTH:].astype(BF16), norm_ffn_w[0][None, :],
                peer_w_q[0].T.astype(BF16),
                peer_sub_keys[0].reshape(2 * PEER_HEADS, PEER_N_KEYS, PEER_HALF).astype(BF16))
    peer_prm = (peer_u[0].astype(BF16), peer_v[0].T.astype(BF16), final_norm_w[None, :])
    s5_args = (s5_lambda_re[0], s5_lambda_im[0], s5_log_step[0], s5_b_re[0], s5_b_im[0], s5_c_re[0], s5_c_im[0])
    ops16 = _s5_prep(*s5_args, lc=N_META)
    ops8 = _s5_prep(*s5_args, lc=ss)

    lc = N_META
    um, _, xbcm, dtm = _in_proj(meta_tokens, nmw, wu, wz, wx, wd, tm=N_META)
    urm = jnp.broadcast_to(um.reshape(lc, g, S5_CH).transpose(1, 0, 2).reshape(g, 1, lc * S5_CH), (g, 8, lc * S5_CH))
    _, h5m = _s5(urm, jnp.zeros((g, 8, 2 * p), F32), ops16, n_chunks=1, rows=8)
    _, convm, hm = _ssd(xbcm, jnp.zeros((N_META, M2_WIDTH), F32), dtm,
                        jnp.zeros((1, M2_CONV - 1, M2_CONV_DIM), F32), jnp.zeros((1, M2_D_STATE, M2_WIDTH), F32),
                        ssd_prm, n_seq=1, n_chunks=1, q_in=N_META, shared_init=True)

    xp = x_prompt.reshape(bp * sp, D_MODEL)
    nc = sp // lc
    up, zp, xbcp, dtp = _in_proj(xp, nmw, wu, wz, wx, wd, tm=512)
    urp = up.reshape(bp, nc, lc, g, S5_CH).transpose(3, 1, 0, 2, 4).reshape(g, nc * bp, lc * S5_CH)
    y5p, h5p = _s5(urp, h5m, ops16, n_chunks=nc, rows=bp)
    y5p = y5p.reshape(g, nc, bp, lc, S5_CH).transpose(2, 1, 3, 0, 4).reshape(bp * sp, S5_WIDTH)
    ymp, convp, hp = _ssd(xbcp, zp, dtp, convm, hm, ssd_prm, n_seq=bp, n_chunks=sp // SSD_CHUNK, q_in=SSD_CHUNK,
                          shared_init=True)
    y_prompt = _tokens_tail(xp, None, post_prm, peer_prm, y5p, up, ymp).reshape(bp, sp, D_MODEL)

    xs = x_sample.reshape(bs * ss, D_MODEL)
    us, zs, xbcs, dts = _in_proj(xs, nmw, wu, wz, wx, wd, tm=512)
    urs = us.reshape(bs, ss, g, S5_CH).transpose(2, 0, 1, 3).reshape(g, bs, ss * S5_CH)
    h5s0 = jnp.concatenate([state_s5_re[0], state_s5_im[0]], axis=-1).transpose(1, 0, 2)
    y5s, h5s = _s5(urs, h5s0, ops8, n_chunks=1, rows=bs)
    y5s = y5s.reshape(g, bs, ss, S5_CH).transpose(1, 2, 0, 3).reshape(bs * ss, S5_WIDTH)
    hs0 = state_ssm[0].reshape(bs, M2_WIDTH, M2_D_STATE).transpose(0, 2, 1)
    yms, convs, hs = _ssd(xbcs, zs, dts, state_conv[0], hs0, ssd_prm, n_seq=bs, n_chunks=1, q_in=ss,
                          shared_init=False)
    y_sample = _tokens_tail(xs, None, post_prm, peer_prm, y5s, us, yms).reshape(bs, ss, D_MODEL)

    def s5_state(hf):
        hf = hf.transpose(1, 0, 2)
        return hf[None, :, :, :p], hf[None, :, :, p:]

    def ssm_state(ht):
        return ht.transpose(0, 2, 1).reshape(1, ht.shape[0], M2_HEADS, M2_HEAD_DIM, M2_D_STATE)

    p5r, p5i = s5_state(h5p)
    s5r, s5i = s5_state(h5s)
    return (y_prompt, y_sample, p5r, p5i, ssm_state(hp), convp[None], s5r, s5i, ssm_state(hs), convs[None])
```

```python
import functools
import math

import jax
import jax.numpy as jnp
from jax import lax
from jax.experimental import pallas as pl
from jax.experimental.pallas import tpu as pltpu

F32 = jnp.float32
BF16 = jnp.bfloat16
HIGHEST = lax.Precision.HIGHEST

D_MODEL = 1024
N_META = 16
S5_WIDTH = 1024
S5_CH = 16
S5_GROUPS = S5_WIDTH // S5_CH
S5_STATE = 64
M2_WIDTH = 1024
M2_HEAD_DIM = 64
M2_HEADS = M2_WIDTH // M2_HEAD_DIM
M2_GROUPS = 2
M2_D_STATE = 128
M2_CONV = 4
M2_CONV_DIM = M2_WIDTH + 2 * M2_GROUPS * M2_D_STATE
PEER_HEADS = 8
PEER_N_KEYS = 128
PEER_EXPERTS = PEER_N_KEYS * PEER_N_KEYS
PEER_HALF = 128
PEER_TOPK = 16
EPS = 1e-6

LANES = 128
SSD_CHUNK = 128
CONV_HALO = 8
VMEM_LIMIT = 56 * 1024 * 1024


def _cparams(*sem):
    return pltpu.CompilerParams(dimension_semantics=sem, vmem_limit_bytes=VMEM_LIMIT)


def _sigmoid(x):
    return 1.0 / (1.0 + jnp.exp(-x))


def _gelu_tanh(x):
    c = math.sqrt(2.0 / math.pi)
    return 0.5 * x * (1.0 + jnp.tanh(c * (x + 0.044715 * (x * x * x))))


def _rms(x, w):
    return x * lax.rsqrt(jnp.mean(x * x, axis=-1, keepdims=True) + EPS) * w


def _bdot(a, b):
    return jnp.dot(a.astype(BF16), b.astype(BF16), preferred_element_type=F32)


def _in_proj_kernel(x_ref, nw_ref, wu_ref, wz_ref, wx_ref, wd_ref, u_ref, z_ref, xbc_ref, dt_ref):
    hb = _rms(x_ref[...], nw_ref[...]).astype(BF16)
    u_ref[...] = jnp.dot(hb, wu_ref[...], preferred_element_type=F32)
    z_ref[...] = jnp.dot(hb, wz_ref[...], preferred_element_type=F32)
    xbc_ref[...] = jnp.dot(hb, wx_ref[...], preferred_element_type=F32)
    dt_ref[...] = jnp.dot(hb, wd_ref[...], preferred_element_type=F32)


def _in_proj(x, nw, wu, wz, wx, wd, tm):
    t = x.shape[0]
    row = lambda i: (i, 0)
    fix = lambda i: (0, 0)
    widths = (S5_WIDTH, M2_WIDTH, M2_CONV_DIM, LANES)
    return pl.pallas_call(
        _in_proj_kernel,
        grid=(t // tm,),
        in_specs=[pl.BlockSpec((tm, D_MODEL), row), pl.BlockSpec((1, D_MODEL), fix)]
        + [pl.BlockSpec((D_MODEL, w), fix) for w in widths],
        out_specs=[pl.BlockSpec((tm, w), row) for w in widths],
        out_shape=[jax.ShapeDtypeStruct((t, w), F32) for w in widths],
        compiler_params=_cparams("parallel"),
        name="in_proj",
    )(x, nw, wu, wz, wx, wd)


def _s5_prep_kernel(lrc_ref, lic_ref, lrr_ref, lir_ref, ls_ref, btr_ref, bti_ref, ctr_ref, cti_ref,
                    kt_ref, wout_ref, wsr_ref, wsi_ref, al_ref, *, lc):
    k = lc * S5_CH
    step = jnp.exp(ls_ref[0])

    def disc(lr, li):
        lg = lr * step
        th = li * step
        mag = jnp.exp(lg)
        ab_re = mag * jnp.cos(th)
        ab_im = mag * jnp.sin(th)
        den = lr * lr + li * li
        f_re = ((ab_re - 1.0) * lr + ab_im * li) / den
        f_im = (ab_im * lr - (ab_re - 1.0) * li) / den
        return lg, th, f_re, f_im

    lg, th, f_re, f_im = disc(lrc_ref[0], lic_ref[0])
    tau = lax.shift_right_logical(lax.broadcasted_iota(jnp.int32, (S5_STATE, k), 1), 4).astype(F32)
    c_re = ctr_ref[0]
    c_im = cti_ref[0]

    def powers(t, lg_, th_):
        mag = jnp.exp(lg_ * t)
        return mag * jnp.cos(th_ * t), mag * jnp.sin(th_ * t)

    p_re, p_im = powers(tau, lg, th)
    fa_re = f_re * p_re - f_im * p_im
    fa_im = f_re * p_im + f_im * p_re
    fca_re = fa_re * c_re - fa_im * c_im
    fca_im = fa_re * c_im + fa_im * c_re
    kt_ref[0] = (jnp.dot(btr_ref[0], fca_re, precision=HIGHEST, preferred_element_type=F32)
                 - jnp.dot(bti_ref[0], fca_im, precision=HIGHEST, preferred_element_type=F32))
    q_re, q_im = powers(tau + 1.0, lg, th)
    wout_ref[0, 0:S5_STATE, :] = q_re * c_re - q_im * c_im
    wout_ref[0, S5_STATE:2 * S5_STATE, :] = -(q_re * c_im + q_im * c_re)

    lgr, thr, fr_re, fr_im = disc(lrr_ref[0], lir_ref[0])
    kk = lax.broadcasted_iota(jnp.int32, (lc, S5_STATE), 0).astype(F32)
    r_re, r_im = powers(kk, lgr, thr)
    g_re = fr_re * r_re - fr_im * r_im
    g_im = fr_re * r_im + fr_im * r_re
    bt_re = btr_ref[0]
    bt_im = bti_ref[0]
    for s in range(lc):
        kpow = lc - 1 - s
        w_re = g_re[kpow:kpow + 1, :]
        w_im = g_im[kpow:kpow + 1, :]
        wsr_ref[0, s * S5_CH:(s + 1) * S5_CH, :] = bt_re * w_re - bt_im * w_im
        wsi_ref[0, s * S5_CH:(s + 1) * S5_CH, :] = bt_re * w_im + bt_im * w_re
    a_re, a_im = powers(float(lc), lgr, thr)
    al_ref[0, 0:1, :] = a_re
    al_ref[0, 1:2, :] = a_im


def _s5_prep(lam_re, lam_im, log_step, b_re, b_im, c_re, c_im, lc):
    g, p, k = S5_GROUPS, S5_STATE, lc * S5_CH
    spec3 = lambda a, b: pl.BlockSpec((1, a, b), lambda i: (i, 0, 0))
    ins = [lam_re.reshape(g, p, 1), lam_im.reshape(g, p, 1), lam_re.reshape(g, 1, p), lam_im.reshape(g, 1, p),
           log_step.reshape(g, 1, 1),
           jnp.swapaxes(b_re, 1, 2), jnp.swapaxes(b_im, 1, 2),
           jnp.tile(jnp.swapaxes(c_re, 1, 2), (1, 1, lc)), jnp.tile(jnp.swapaxes(c_im, 1, 2), (1, 1, lc))]
    kt, wout, wsr, wsi, al = pl.pallas_call(
        functools.partial(_s5_prep_kernel, lc=lc),
        grid=(g,),
        in_specs=[spec3(p, 1), spec3(p, 1), spec3(1, p), spec3(1, p), spec3(1, 1),
                  spec3(S5_CH, p), spec3(S5_CH, p), spec3(p, k), spec3(p, k)],
        out_specs=[spec3(S5_CH, k), spec3(2 * p, k), spec3(k, p), spec3(k, p), spec3(2, p)],
        out_shape=[jax.ShapeDtypeStruct((g, S5_CH, k), F32), jax.ShapeDtypeStruct((g, 2 * p, k), F32),
                   jax.ShapeDtypeStruct((g, k, p), F32), jax.ShapeDtypeStruct((g, k, p), F32),
                   jax.ShapeDtypeStruct((g, 2, p), F32)],
        compiler_params=_cparams("parallel"),
        name="s5_prep",
    )(*ins)
    lag = jnp.arange(lc)[None, :] - jnp.arange(lc)[:, None]
    kt4 = kt.reshape(g, S5_CH, lc, S5_CH)
    toep = kt4[:, :, jnp.maximum(lag, 0), :]
    toep = jnp.where((lag >= 0)[None, None, :, :, None], toep, 0.0)
    toep = toep.transpose(0, 2, 1, 3, 4).reshape(g, k, k)
    wst = jnp.concatenate([wsr, wsi, wsi, wsr], axis=-1)
    a_re, a_im = al[:, 0:1, :], al[:, 1:2, :]
    acoef = jnp.concatenate([jnp.concatenate([a_re, a_re], -1), jnp.concatenate([-a_im, a_im], -1),
                             jnp.concatenate([a_im, -a_im], -1)], axis=1)
    return toep.astype(BF16), wst.astype(BF16), wout.astype(BF16), acoef


S5_GBLK = LANES // S5_CH
S5_ROW_TILE = 256


def _s5_kernel(u_ref, h0_ref, h0s_ref, wst_ref, toep_ref, wout_ref, a_ref, y_ref, hf_ref,
               s_scr, hin_scr, h_scr, hs_scr, *, lc, rows, chunks):
    i = pl.program_id(1)

    @pl.when(i == 0)
    def _():
        h_scr[...] = h0_ref[...]
        hs_scr[...] = h0s_ref[...]

    x = u_ref[0]
    ubs = []
    for j in range(S5_GBLK):
        ug = jnp.concatenate([x[:, s * LANES + j * S5_CH:s * LANES + (j + 1) * S5_CH] for s in range(lc)], axis=1)
        ubs.append(ug.astype(BF16))
        s_scr[j] = jnp.dot(ubs[j], wst_ref[j], preferred_element_type=F32)

    def step(r0, carry):
        new = []
        for j in range(S5_GBLK):
            h, hs = carry[2 * j], carry[2 * j + 1]
            hin_scr[j, pl.ds(r0, rows), :] = h
            s = s_scr[j, pl.ds(r0, rows), :]
            a1, a2, a2s = a_ref[j, 0:1, :], a_ref[j, 1:2, :], a_ref[j, 2:3, :]
            new.append(a1 * h + a2 * hs + s[:, :LANES])
            new.append(a1 * hs + a2s * h + s[:, LANES:])
        return tuple(new)

    carry = tuple(ref[j] for j in range(S5_GBLK) for ref in (h_scr, hs_scr))
    if chunks == 1:
        carry = step(0, carry)
    else:
        carry = lax.fori_loop(0, chunks, lambda n, c: step(pl.multiple_of(n * rows, rows), c), carry)
    for j in range(S5_GBLK):
        h_scr[j] = carry[2 * j]
        hs_scr[j] = carry[2 * j + 1]
    hf_ref[...] = h_scr[...]

    ys = [jnp.dot(ubs[j], toep_ref[j], preferred_element_type=F32)
          + jnp.dot(hin_scr[j].astype(BF16), wout_ref[j], preferred_element_type=F32) for j in range(S5_GBLK)]
    for t in range(lc):
        y_ref[0, :, t * LANES:(t + 1) * LANES] = jnp.concatenate(
            [ys[j][:, t * S5_CH:(t + 1) * S5_CH] for j in range(S5_GBLK)], axis=1)


def _s5(ub, h0, ops, lc, rows, chunks):
    toep, wst, wout, acoef = ops
    nb, nr, width = ub.shape
    k = lc * S5_CH
    rt = rows * chunks
    blk = lambda a, b: pl.BlockSpec((S5_GBLK, a, b), lambda gb, i: (gb, 0, 0))
    return pl.pallas_call(
        functools.partial(_s5_kernel, lc=lc, rows=rows, chunks=chunks),
        grid=(nb, nr // rt),
        in_specs=[pl.BlockSpec((1, rt, width), lambda gb, i: (gb, i, 0)),
                  blk(rows, LANES), blk(rows, LANES), blk(k, 2 * LANES), blk(k, k), blk(LANES, k), blk(3, LANES)],
        out_specs=[pl.BlockSpec((1, rt, width), lambda gb, i: (gb, i, 0)), blk(rows, LANES)],
        out_shape=[jax.ShapeDtypeStruct((nb, nr, width), F32), jax.ShapeDtypeStruct((S5_GROUPS, rows, LANES), F32)],
        scratch_shapes=[pltpu.VMEM((S5_GBLK, rt, 2 * LANES), F32), pltpu.VMEM((S5_GBLK, rt, LANES), F32),
                        pltpu.VMEM((S5_GBLK, rows, LANES), F32), pltpu.VMEM((S5_GBLK, rows, LANES), F32)],
        compiler_params=_cparams("parallel", "arbitrary"),
        name="s5_scan",
    )(ub, h0, jnp.roll(h0, S5_STATE, axis=-1), wst, toep, wout, acoef)


def _ssd_kernel(xbc_ref, z_ref, dt_ref, conv0_ref, h0_ref, cw_ref, cb_ref, dtb_ref, alog_ref, dexp_ref, nw_ref,
                y_ref, convn_ref, hn_ref, buf_scr, dt_scr, h_scr, *, q_in, n_chunks):
    q = SSD_CHUNK
    pad = q - q_in
    first = CONV_HALO + pad
    c = pl.program_id(1)

    @pl.when(c == 0)
    def _():
        buf_scr[0:first, :] = jnp.zeros((first, M2_CONV_DIM), F32)
        buf_scr[first - (M2_CONV - 1):first, :] = conv0_ref[0]
        h_scr[...] = h0_ref[0]

    buf_scr[first:CONV_HALO + q, :] = xbc_ref[...]
    conv = cb_ref[...]
    for kk in range(M2_CONV):
        lo = CONV_HALO - (M2_CONV - 1) + kk
        conv = conv + cw_ref[kk:kk + 1, :] * buf_scr[lo:lo + q, :]
    convn_ref[0] = buf_scr[CONV_HALO + q - (M2_CONV - 1):CONV_HALO + q, :]
    if n_chunks > 1:
        buf_scr[0:CONV_HALO, :] = buf_scr[q:q + CONV_HALO, :]
    act = conv * _sigmoid(conv)
    xs = act[:, :M2_WIDTH]

    dtv = dt_ref[...] + dtb_ref[...]
    dt_real = jnp.maximum(dtv, 0.0) + jnp.log1p(jnp.exp(-jnp.abs(dtv)))
    if pad:
        dt_scr[0:pad, :] = jnp.zeros((pad, LANES), F32)
        dt_scr[pad:q, :] = dt_real
        dt = dt_scr[...]
    else:
        dt = dt_real
    a_neg = -jnp.exp(alog_ref[...])
    ri = lax.broadcasted_iota(jnp.int32, (q, q), 0)
    ci = lax.broadcasted_iota(jnp.int32, (q, q), 1)
    tril = ri >= ci
    acs = jnp.dot(tril.astype(F32), dt * a_neg, precision=HIGHEST, preferred_element_type=F32)
    acs_t = acs.T
    dt_t = dt.T
    acs_last = acs[q - 1:q, :]
    eacs = jnp.exp(acs)
    wdec = jnp.exp(acs_last - acs) * dt
    dec_last = jnp.exp(acs_last)
    low = lax.broadcasted_iota(jnp.int32, (q, LANES), 1) < M2_HEAD_DIM
    low1 = low[0:1, :]

    def pair_cols(m, h0):
        return jnp.where(low[0:m.shape[0], :], m[:, h0:h0 + 1], m[:, h0 + 1:h0 + 2])

    y_parts = []
    hpg = M2_HEADS // M2_GROUPS
    gw = hpg * M2_HEAD_DIM
    for g in range(M2_GROUPS):
        bg = act[:, M2_WIDTH + g * M2_D_STATE:M2_WIDTH + (g + 1) * M2_D_STATE]
        cg = act[:, M2_WIDTH + (M2_GROUPS + g) * M2_D_STATE:M2_WIDTH + (M2_GROUPS + g + 1) * M2_D_STATE]
        cgb = cg.astype(BF16)
        cb = lax.dot_general(cgb, bg.astype(BF16), (((1,), (1,)), ((), ())), preferred_element_type=F32)
        hg = h_scr[:, g * gw:(g + 1) * gw]
        yoff = jnp.dot(cgb, hg.astype(BF16), preferred_element_type=F32)
        xw_parts = []
        dec_parts = []
        for pp in range(hpg // 2):
            h0 = g * hpg + 2 * pp
            lanes = slice(h0 * M2_HEAD_DIM, (h0 + 2) * M2_HEAD_DIM)
            w_pair = []
            for h in (h0, h0 + 1):
                seg = acs[:, h:h + 1] - acs_t[h:h + 1, :]
                dec = jnp.exp(jnp.where(tril, seg, -jnp.inf))
                w_pair.append((cb * dec * dt_t[h:h + 1, :]).astype(BF16))
            xp = xs[:, lanes]
            xbd = jnp.concatenate([jnp.where(low, xp, 0.0), jnp.where(low, 0.0, xp)], axis=0).astype(BF16)
            yd = jnp.dot(jnp.concatenate(w_pair, axis=1), xbd, preferred_element_type=F32)
            yo = yoff[:, 2 * pp * M2_HEAD_DIM:(2 * pp + 2) * M2_HEAD_DIM]
            y_parts.append(yd + yo * pair_cols(eacs, h0) + dexp_ref[:, lanes] * xp)
            xw_parts.append(xp * pair_cols(wdec, h0))
            dec_parts.append(jnp.where(low1, dec_last[:, h0:h0 + 1], dec_last[:, h0 + 1:h0 + 2]))
        xw = jnp.concatenate(xw_parts, axis=1).astype(BF16)
        st = jnp.dot(bg.T.astype(BF16), xw, preferred_element_type=F32)
        h_scr[:, g * gw:(g + 1) * gw] = hg * jnp.concatenate(dec_parts, axis=1) + st

    @pl.when(c == n_chunks - 1)
    def _():
        hn_ref[0] = h_scr[...]

    y = jnp.concatenate(y_parts, axis=1)[pad:, :]
    zz = z_ref[...]
    y = y * (zz * _sigmoid(zz))
    outs = []
    for g in range(M2_GROUPS):
        yg = y[:, g * gw:(g + 1) * gw]
        outs.append(yg * lax.rsqrt(jnp.mean(yg * yg, axis=-1, keepdims=True) + EPS))
    y_ref[...] = jnp.concatenate(outs, axis=1) * nw_ref[...]


def _ssd(xbc, z, dt, conv0, h0, prm, n_seq, n_chunks, q_in, shared_init):
    assert q_in == SSD_CHUNK or n_chunks == 1
    assert q_in >= CONV_HALO and q_in % CONV_HALO == 0
    t = xbc.shape[0]
    row = lambda b, c: (b * n_chunks + c, 0)
    fix = lambda b, c: (0, 0)
    init = (lambda b, c: (0, 0, 0)) if shared_init else (lambda b, c: (b, 0, 0))
    per_seq = lambda b, c: (b, 0, 0)
    return pl.pallas_call(
        functools.partial(_ssd_kernel, q_in=q_in, n_chunks=n_chunks),
        grid=(n_seq, n_chunks),
        in_specs=[pl.BlockSpec((q_in, M2_CONV_DIM), row), pl.BlockSpec((q_in, M2_WIDTH), row),
                  pl.BlockSpec((q_in, LANES), row),
                  pl.BlockSpec((1, M2_CONV - 1, M2_CONV_DIM), init),
                  pl.BlockSpec((1, M2_D_STATE, M2_WIDTH), init),
                  pl.BlockSpec((M2_CONV, M2_CONV_DIM), fix), pl.BlockSpec((1, M2_CONV_DIM), fix),
                  pl.BlockSpec((1, LANES), fix), pl.BlockSpec((1, LANES), fix),
                  pl.BlockSpec((1, M2_WIDTH), fix), pl.BlockSpec((1, M2_WIDTH), fix)],
        out_specs=[pl.BlockSpec((q_in, M2_WIDTH), row),
                   pl.BlockSpec((1, M2_CONV - 1, M2_CONV_DIM), per_seq),
                   pl.BlockSpec((1, M2_D_STATE, M2_WIDTH), per_seq)],
        out_shape=[jax.ShapeDtypeStruct((t, M2_WIDTH), F32),
                   jax.ShapeDtypeStruct((n_seq, M2_CONV - 1, M2_CONV_DIM), F32),
                   jax.ShapeDtypeStruct((n_seq, M2_D_STATE, M2_WIDTH), F32)],
        scratch_shapes=[pltpu.VMEM((CONV_HALO + SSD_CHUNK, M2_CONV_DIM), F32),
                        pltpu.VMEM((SSD_CHUNK, LANES), F32),
                        pltpu.VMEM((M2_D_STATE, M2_WIDTH), F32)],
        compiler_params=_cparams("parallel", "arbitrary"),
        name="ssd",
    )(xbc, z, dt, conv0, h0, *prm)


def _post_kernel(y5_ref, u_ref, ym_ref, x_ref, d_ref, wglu_ref, s5nw_ref, wo1_ref, wo2_ref, fnw_ref, wqt_ref,
                 sk_ref, x1_ref, xt_ref, st_ref):
    g = _gelu_tanh(y5_ref[...] + d_ref[...] * u_ref[...])
    o = g * _sigmoid(jnp.dot(g.astype(BF16), wglu_ref[...], preferred_element_type=F32))
    y5n = _rms(o, s5nw_ref[...])
    x1 = (x_ref[...] + jnp.dot(y5n.astype(BF16), wo1_ref[...], preferred_element_type=F32)
          + jnp.dot(ym_ref[...].astype(BF16), wo2_ref[...], preferred_element_type=F32))
    x1_ref[...] = x1
    hn_t = _rms(x1, fnw_ref[...]).T.astype(BF16)
    xt_ref[...] = hn_t
    q_t = jnp.dot(wqt_ref[...], hn_t, preferred_element_type=F32)
    for k in range(2 * PEER_HEADS):
        qk = q_t[k * PEER_HALF:(k + 1) * PEER_HALF, :].astype(BF16)
        st_ref[k] = jnp.dot(sk_ref[k], qk, preferred_element_type=F32) * LOG2E


def _post(y5, u, ym, x, prm, tm):
    t = x.shape[0]
    row = lambda i: (i, 0)
    fix = lambda i: (0, 0)
    nk = 2 * PEER_HEADS
    return pl.pallas_call(
        _post_kernel,
        grid=(t // tm,),
        in_specs=[pl.BlockSpec((tm, D_MODEL), row)] * 4
        + [pl.BlockSpec((1, D_MODEL), fix), pl.BlockSpec((S5_WIDTH, S5_WIDTH), fix), pl.BlockSpec((1, D_MODEL), fix),
           pl.BlockSpec((S5_WIDTH, D_MODEL), fix), pl.BlockSpec((M2_WIDTH, D_MODEL), fix),
           pl.BlockSpec((1, D_MODEL), fix), pl.BlockSpec((nk * PEER_HALF, D_MODEL), fix),
           pl.BlockSpec((nk, PEER_N_KEYS, PEER_HALF), lambda i: (0, 0, 0))],
        out_specs=[pl.BlockSpec((tm, D_MODEL), row), pl.BlockSpec((D_MODEL, tm), lambda i: (0, i)),
                   pl.BlockSpec((nk, PEER_N_KEYS, tm), lambda i: (0, 0, i))],
        out_shape=[jax.ShapeDtypeStruct((t, D_MODEL), F32), jax.ShapeDtypeStruct((D_MODEL, t), BF16),
                   jax.ShapeDtypeStruct((nk, PEER_N_KEYS, t), F32)],
        compiler_params=_cparams("parallel"),
        name="post",
    )(y5, u, ym, x, *prm)


def _sorting_network(n):
    pairs = []
    p = 1
    while p < n:
        k = p
        while k >= 1:
            for j in range(k % p, n - k, 2 * k):
                for i in range(min(k, n - j - k)):
                    if (i + j) // (2 * p) == (i + j + k) // (2 * p):
                        pairs.append((i + j, i + j + k))
            k //= 2
        p *= 2
    return pairs


SUBLANES = 8
_NET16 = _sorting_network(PEER_N_KEYS // SUBLANES)
LOG2E = math.log2(math.e)


def _top16(tiles):
    n = len(tiles)
    tiles = list(tiles)

    def exchange(i, j):
        tiles[i], tiles[j] = jnp.maximum(tiles[i], tiles[j]), jnp.minimum(tiles[i], tiles[j])

    for i, j in _NET16:
        exchange(i, j)
    shift = SUBLANES // 2
    while shift >= 1:
        other = [pltpu.roll(t, shift, axis=0) for t in tiles]
        tiles = [jnp.maximum(tiles[i], other[n - 1 - i]) for i in range(n)]
        dist = n // 2
        while dist >= 1:
            for i in range(n):
                if i & dist == 0:
                    exchange(i, i + dist)
            dist //= 2
        shift //= 2
    return tiles


def _top_values(s):
    n = PEER_N_KEYS // SUBLANES
    tiles = _top16([s[i * SUBLANES:(i + 1) * SUBLANES, :] for i in range(n)])
    rows = lax.broadcasted_iota(jnp.int32, (PEER_TOPK, s.shape[1]), 0)
    out = jnp.zeros((PEER_TOPK, s.shape[1]), F32)
    for r in range(PEER_TOPK):
        out = jnp.where(rows == r, jnp.concatenate([tiles[r], tiles[r]], axis=0), out)
    return out


def _pair_candidates(v1, v2):
    r8 = lax.broadcasted_iota(jnp.int32, (SUBLANES, v1.shape[1]), 0)
    r16 = lax.broadcasted_iota(jnp.int32, (PEER_TOPK, v1.shape[1]), 0)
    neg = -jnp.inf
    lo2 = v2[0:SUBLANES, :]
    return jnp.concatenate([
        v1[0:1, :] + v2,
        v1[1:2, :] + lo2,
        jnp.where(r16 >= 2, v1 + v2[0:1, :], neg),
        jnp.where(r8 >= 2, v1[0:SUBLANES, :] + v2[1:2, :], neg),
        jnp.where((r8 >= 2) & (r8 <= 4), v1[2:3, :] + lo2, neg),
        jnp.where((r8 >= 2) & (r8 <= 3), v1[3:4, :] + lo2, neg),
        jnp.where(r8 == 2, v1[4:5, :] + lo2, neg),
    ], axis=0)


def _route_kernel(st_ref, s1n_ref, s2n_ref, thr_ref):
    def head(h, carry):
        s1 = st_ref[2 * h]
        s2 = st_ref[2 * h + 1]
        v1 = _top_values(s1)
        v2 = _top_values(s2)
        cand = _pair_candidates(v1, v2)
        n_cand = cand.shape[0] // SUBLANES
        pad = [jnp.full((SUBLANES, cand.shape[1]), -jnp.inf, F32)] * (PEER_N_KEYS // SUBLANES - n_cand)
        ranked = _top16([cand[i * SUBLANES:(i + 1) * SUBLANES, :] for i in range(n_cand)] + pad)
        theta = ranked[PEER_TOPK - 1][0:1, :]
        sel = cand >= theta
        m = v1[0:1, :] + v2[0:1, :]
        zsum = jnp.sum(jnp.where(sel, jnp.exp2(cand - m), 0.0), axis=0, keepdims=True)
        off = m + jnp.log(zsum) * LOG2E + 1.0
        s1n_ref[h] = s1 - off
        s2n_ref[h] = s2
        candn = _pair_candidates(v1 - off, v2)
        thr_ref[h] = jnp.min(jnp.where(sel, candn, jnp.inf), axis=0, keepdims=True)
        return carry

    lax.fori_loop(0, PEER_HEADS, head, 0)


def _route(st, tl):
    nk, keys, t = st.shape
    spec = lambda n: pl.BlockSpec((n, keys, tl), lambda i: (0, 0, i))
    return pl.pallas_call(
        _route_kernel,
        grid=(t // tl,),
        in_specs=[spec(nk)],
        out_specs=[spec(PEER_HEADS), spec(PEER_HEADS), pl.BlockSpec((PEER_HEADS, 1, tl), lambda i: (0, 0, i))],
        out_shape=[jax.ShapeDtypeStruct((PEER_HEADS, keys, t), F32), jax.ShapeDtypeStruct((PEER_HEADS, keys, t), F32),
                   jax.ShapeDtypeStruct((PEER_HEADS, 1, t), F32)],
        compiler_params=_cparams("parallel"),
        name="route",
    )(st)


PEER_E1_BLK = 8
MXU_DEPTH = 256


def _peer_kernel(u_ref, vt_ref, xt_ref, s2n_ref, s1n_ref, thr_ref, x1_ref, fnw_ref, y_ref, acc_scr, *, tm):
    k = pl.program_id(1)
    c0 = math.sqrt(2.0 / math.pi)
    c1 = c0 * 0.044715
    per_dot = MXU_DEPTH // PEER_N_KEYS
    n_dots = PEER_E1_BLK // per_dot

    def scores(q):
        return jnp.dot(u_ref[q * MXU_DEPTH:(q + 1) * MXU_DEPTH, :], xt_ref[...], preferred_element_type=F32)

    def gated(q, a):
        wg_rows = []
        for i in range(per_dot):
            r = q * per_dot + i
            rows = slice(i * PEER_N_KEYS, (i + 1) * PEER_N_KEYS)
            wg_cols = []
            for c in range(tm // LANES):
                cols = slice(c * LANES, (c + 1) * LANES)
                w = None
                for h in range(PEER_HEADS):
                    arg = s2n_ref[h, :, cols] + s1n_ref[h, r:r + 1, cols]
                    term = jnp.where(arg >= thr_ref[h, :, cols], jnp.exp2(arg), 0.0)
                    w = term if w is None else w + term
                x = a[rows, cols]
                g = x + x * jnp.tanh(x * (c0 + c1 * (x * x)))
                wg_cols.append((w * g).astype(BF16))
            wg_rows.append(jnp.concatenate(wg_cols, axis=1))
        return jnp.concatenate(wg_rows, axis=0)

    def mixed(q, wg):
        return jnp.dot(vt_ref[0, :, q * MXU_DEPTH:(q + 1) * MXU_DEPTH], wg, preferred_element_type=F32)

    total = None
    a_next = scores(0)
    wg_prev = None
    for q in range(n_dots):
        a_cur = a_next
        if q + 1 < n_dots:
            a_next = scores(q + 1)
        wg = gated(q, a_cur)
        if wg_prev is not None:
            d = mixed(q - 1, wg_prev)
            total = d if total is None else total + d
        wg_prev = wg
    total = total + mixed(n_dots - 1, wg_prev)

    @pl.when(k == 0)
    def _():
        acc_scr[...] = total

    @pl.when(k > 0)
    def _():
        acc_scr[...] += total

    @pl.when(k == pl.num_programs(1) - 1)
    def _():
        y_ref[...] = _rms(x1_ref[...] + acc_scr[...].T, fnw_ref[...])


def _peer(u_bf, vt_bf, xt, s2n, s1n, thr, x1, fnw, tm):
    t = x1.shape[0]
    eb = PEER_E1_BLK * PEER_N_KEYS
    return pl.pallas_call(
        functools.partial(_peer_kernel, tm=tm),
        grid=(t // tm, PEER_N_KEYS // PEER_E1_BLK),
        in_specs=[pl.BlockSpec((eb, D_MODEL), lambda j, k: (k, 0)),
                  pl.BlockSpec((1, D_MODEL, eb), lambda j, k: (k, 0, 0)),
                  pl.BlockSpec((D_MODEL, tm), lambda j, k: (0, j)),
                  pl.BlockSpec((PEER_HEADS, PEER_N_KEYS, tm), lambda j, k: (0, 0, j)),
                  pl.BlockSpec((PEER_HEADS, PEER_E1_BLK, tm), lambda j, k: (0, k, j)),
                  pl.BlockSpec((PEER_HEADS, 1, tm), lambda j, k: (0, 0, j)),
                  pl.BlockSpec((tm, D_MODEL), lambda j, k: (j, 0)),
                  pl.BlockSpec((1, D_MODEL), lambda j, k: (0, 0))],
        out_specs=pl.BlockSpec((tm, D_MODEL), lambda j, k: (j, 0)),
        out_shape=jax.ShapeDtypeStruct((t, D_MODEL), F32),
        scratch_shapes=[pltpu.VMEM((D_MODEL, tm), F32)],
        compiler_params=_cparams("parallel", "arbitrary"),
        name="peer",
    )(u_bf, vt_bf, xt, s2n, s1n, thr, x1, fnw)


def _tokens_tail(x, post_prm, peer_prm, y5, u, ym):
    x1, xt, st = _post(y5, u, ym, x, post_prm, tm=256)
    s1n, s2n, thr = _route(st, tl=LANES)
    u_bf, vt_bf, fnw = peer_prm
    return _peer(u_bf, vt_bf, xt, s2n, s1n, thr, x1, fnw, tm=512)


def kernel(x_prompt, x_sample, state_s5_re, state_s5_im, state_ssm, state_conv, meta_tokens, norm_mix_w, w_in,
           s5_lambda_re, s5_lambda_im, s5_log_step, s5_b_re, s5_b_im, s5_c_re, s5_c_im, s5_d, s5_w_glu, s5_norm_w,
           m2_conv_w, m2_conv_b, m2_dt_bias, m2_a_log, m2_d, m2_norm_w, w_out, norm_ffn_w, peer_w_q, peer_sub_keys,
           peer_u, peer_v, final_norm_w):
    bp, sp, _ = x_prompt.shape
    bs, ss, _ = x_sample.shape
    g, p = S5_GROUPS, S5_STATE

    w = w_in[0]
    o1, o2, o3 = S5_WIDTH, S5_WIDTH + M2_WIDTH, S5_WIDTH + M2_WIDTH + M2_CONV_DIM
    wu, wz, wx = w[:, :o1].astype(BF16), w[:, o1:o2].astype(BF16), w[:, o2:o3].astype(BF16)
    wd = jnp.pad(w[:, o3:], ((0, 0), (0, LANES - M2_HEADS))).astype(BF16)
    nmw = norm_mix_w[0][None, :]
    pad_h = lambda v: jnp.pad(v, (0, LANES - M2_HEADS))[None, :]
    ssd_prm = (m2_conv_w[0], m2_conv_b[0][None, :], pad_h(m2_dt_bias[0]), pad_h(m2_a_log[0]),
               jnp.repeat(m2_d[0], M2_HEAD_DIM)[None, :], m2_norm_w[0][None, :])
    post_prm = (s5_d[0][None, :], s5_w_glu[0].astype(BF16), s5_norm_w[0][None, :],
                w_out[0][:S5_WIDTH].astype(BF16), w_out[0][S5_WIDTH:].astype(BF16), norm_ffn_w[0][None, :],
                peer_w_q[0].T.astype(BF16),
                peer_sub_keys[0].reshape(2 * PEER_HEADS, PEER_N_KEYS, PEER_HALF).astype(BF16))
    eb = PEER_E1_BLK * PEER_N_KEYS
    vt_blocks = peer_v[0].astype(BF16).reshape(PEER_EXPERTS // eb, eb, D_MODEL).swapaxes(1, 2)
    peer_prm = (peer_u[0].astype(BF16), vt_blocks, final_norm_w[None, :])
    s5_args = (s5_lambda_re[0], s5_lambda_im[0], s5_log_step[0], s5_b_re[0], s5_b_im[0], s5_c_re[0], s5_c_im[0])
    ops16 = _s5_prep(*s5_args, lc=N_META)
    ops8 = _s5_prep(*s5_args, lc=ss)

    lc = N_META
    um, _, xbcm, dtm = _in_proj(meta_tokens, nmw, wu, wz, wx, wd, tm=N_META)
    nb = S5_WIDTH // LANES
    urm = jnp.broadcast_to(um.reshape(lc, nb, LANES).transpose(1, 0, 2).reshape(nb, 1, lc * LANES),
                           (nb, SUBLANES, lc * LANES))
    _, h5m = _s5(urm, jnp.zeros((g, SUBLANES, 2 * p), F32), ops16, lc=lc, rows=SUBLANES, chunks=1)
    _, convm, hm = _ssd(xbcm, jnp.zeros((N_META, M2_WIDTH), F32), dtm,
                        jnp.zeros((1, M2_CONV - 1, M2_CONV_DIM), F32), jnp.zeros((1, M2_D_STATE, M2_WIDTH), F32),
                        ssd_prm, n_seq=1, n_chunks=1, q_in=N_META, shared_init=True)

    xp = x_prompt.reshape(bp * sp, D_MODEL)
    nc = sp // lc
    up, zp, xbcp, dtp = _in_proj(xp, nmw, wu, wz, wx, wd, tm=512)
    urp = up.reshape(bp, nc, lc, nb, LANES).transpose(3, 1, 0, 2, 4).reshape(nb, nc * bp, lc * LANES)
    y5p, h5p = _s5(urp, h5m, ops16, lc=lc, rows=bp, chunks=S5_ROW_TILE // bp)
    y5p = y5p.reshape(nb, nc, bp, lc, LANES).transpose(2, 1, 3, 0, 4).reshape(bp * sp, S5_WIDTH)
    ymp, convp, hp = _ssd(xbcp, zp, dtp, convm, hm, ssd_prm, n_seq=bp, n_chunks=sp // SSD_CHUNK, q_in=SSD_CHUNK,
                          shared_init=True)
    y_prompt = _tokens_tail(xp, post_prm, peer_prm, y5p, up, ymp).reshape(bp, sp, D_MODEL)

    xs = x_sample.reshape(bs * ss, D_MODEL)
    us, zs, xbcs, dts = _in_proj(xs, nmw, wu, wz, wx, wd, tm=512)
    urs = us.reshape(bs, ss, nb, LANES).transpose(2, 0, 1, 3).reshape(nb, bs, ss * LANES)
    h5s0 = jnp.concatenate([state_s5_re[0], state_s5_im[0]], axis=-1).transpose(1, 0, 2)
    y5s, h5s = _s5(urs, h5s0, ops8, lc=ss, rows=bs, chunks=1)
    y5s = y5s.reshape(nb, bs, ss, LANES).transpose(1, 2, 0, 3).reshape(bs * ss, S5_WIDTH)
    hs0 = state_ssm[0].reshape(bs, M2_WIDTH, M2_D_STATE).transpose(0, 2, 1)
    yms, convs, hs = _ssd(xbcs, zs, dts, state_conv[0], hs0, ssd_prm, n_seq=bs, n_chunks=1, q_in=ss,
                          shared_init=False)
    y_sample = _tokens_tail(xs, post_prm, peer_prm, y5s, us, yms).reshape(bs, ss, D_MODEL)

    def s5_state(hf):
        hf = hf.transpose(1, 0, 2)
        return hf[None, :, :, :p], hf[None, :, :, p:]

    def ssm_state(ht):
        return ht.transpose(0, 2, 1).reshape(1, ht.shape[0], M2_HEADS, M2_HEAD_DIM, M2_D_STATE)

    p5r, p5i = s5_state(h5p)
    s5r, s5i = s5_state(h5s)
    return (y_prompt, y_sample, p5r, p5i, ssm_state(hp), convp[None], s5r, s5i, ssm_state(hs), convs[None])
```

```python
import functools
import math

import jax
import jax.numpy as jnp
from jax import lax
from jax.experimental import pallas as pl
from jax.experimental.pallas import tpu as pltpu

F32 = jnp.float32
BF16 = jnp.bfloat16
HIGHEST = lax.Precision.HIGHEST

D_MODEL = 1024
N_META = 16
S5_WIDTH = 1024
S5_CH = 16
S5_GROUPS = S5_WIDTH // S5_CH
S5_STATE = 64
M2_WIDTH = 1024
M2_HEAD_DIM = 64
M2_HEADS = M2_WIDTH // M2_HEAD_DIM
M2_GROUPS = 2
M2_D_STATE = 128
M2_CONV = 4
M2_CONV_DIM = M2_WIDTH + 2 * M2_GROUPS * M2_D_STATE
PEER_HEADS = 8
PEER_N_KEYS = 128
PEER_EXPERTS = PEER_N_KEYS * PEER_N_KEYS
PEER_HALF = 128
PEER_TOPK = 16
EPS = 1e-6

LANES = 128
SSD_CHUNK = 128
CONV_HALO = 8
VMEM_LIMIT = 56 * 1024 * 1024


def _cparams(*sem):
    return pltpu.CompilerParams(dimension_semantics=sem, vmem_limit_bytes=VMEM_LIMIT)


def _sigmoid(x):
    return 1.0 / (1.0 + jnp.exp(-x))


def _gelu_tanh(x):
    c = math.sqrt(2.0 / math.pi)
    return 0.5 * x * (1.0 + jnp.tanh(c * (x + 0.044715 * (x * x * x))))


def _rms(x, w):
    return x * lax.rsqrt(jnp.mean(x * x, axis=-1, keepdims=True) + EPS) * w


def _bdot(a, b):
    return jnp.dot(a.astype(BF16), b.astype(BF16), preferred_element_type=F32)


def _in_proj_kernel(x_ref, nw_ref, wu_ref, wz_ref, wx_ref, wd_ref, u_ref, z_ref, xbc_ref, dt_ref):
    hb = _rms(x_ref[...], nw_ref[...]).astype(BF16)
    u_ref[...] = jnp.dot(hb, wu_ref[...], preferred_element_type=F32)
    z_ref[...] = jnp.dot(hb, wz_ref[...], preferred_element_type=F32)
    xbc_ref[...] = jnp.dot(hb, wx_ref[...], preferred_element_type=F32)
    dt_ref[...] = jnp.dot(hb, wd_ref[...], preferred_element_type=F32)


def _in_proj(x, nw, wu, wz, wx, wd, tm):
    t = x.shape[0]
    row = lambda i: (i, 0)
    fix = lambda i: (0, 0)
    widths = (S5_WIDTH, M2_WIDTH, M2_CONV_DIM, LANES)
    return pl.pallas_call(
        _in_proj_kernel,
        grid=(t // tm,),
        in_specs=[pl.BlockSpec((tm, D_MODEL), row), pl.BlockSpec((1, D_MODEL), fix)]
        + [pl.BlockSpec((D_MODEL, w), fix) for w in widths],
        out_specs=[pl.BlockSpec((tm, w), row) for w in widths],
        out_shape=[jax.ShapeDtypeStruct((t, w), F32) for w in widths],
        compiler_params=_cparams("parallel"),
        name="in_proj",
    )(x, nw, wu, wz, wx, wd)


def _s5_prep_kernel(lrc_ref, lic_ref, lrr_ref, lir_ref, ls_ref, btr_ref, bti_ref, ctr_ref, cti_ref,
                    kt_ref, wout_ref, wsr_ref, wsi_ref, al_ref, *, lc):
    for j in range(S5_GBLK):
        _s5_prep_group(j, lc, lrc_ref, lic_ref, lrr_ref, lir_ref, ls_ref, btr_ref, bti_ref, ctr_ref, cti_ref,
                       kt_ref, wout_ref, wsr_ref, wsi_ref, al_ref)


def _s5_prep_group(j, lc, lrc_ref, lic_ref, lrr_ref, lir_ref, ls_ref, btr_ref, bti_ref, ctr_ref, cti_ref,
                   kt_ref, wout_ref, wsr_ref, wsi_ref, al_ref):
    k = lc * S5_CH
    step = jnp.exp(ls_ref[j])

    def disc(lr, li):
        lg = lr * step
        th = li * step
        mag = jnp.exp(lg)
        ab_re = mag * jnp.cos(th)
        ab_im = mag * jnp.sin(th)
        den = lr * lr + li * li
        f_re = ((ab_re - 1.0) * lr + ab_im * li) / den
        f_im = (ab_im * lr - (ab_re - 1.0) * li) / den
        return lg, th, f_re, f_im

    lg, th, f_re, f_im = disc(lrc_ref[j], lic_ref[j])
    tau = lax.shift_right_logical(lax.broadcasted_iota(jnp.int32, (S5_STATE, k), 1), 4).astype(F32)
    c_re = ctr_ref[j]
    c_im = cti_ref[j]

    def powers(t, lg_, th_):
        mag = jnp.exp(lg_ * t)
        return mag * jnp.cos(th_ * t), mag * jnp.sin(th_ * t)

    p_re, p_im = powers(tau, lg, th)
    fa_re = f_re * p_re - f_im * p_im
    fa_im = f_re * p_im + f_im * p_re
    fca_re = fa_re * c_re - fa_im * c_im
    fca_im = fa_re * c_im + fa_im * c_re
    kt_ref[j] = (jnp.dot(btr_ref[j], fca_re, precision=HIGHEST, preferred_element_type=F32)
                 - jnp.dot(bti_ref[j], fca_im, precision=HIGHEST, preferred_element_type=F32))
    q_re, q_im = powers(tau + 1.0, lg, th)
    wout_ref[j, 0:S5_STATE, :] = q_re * c_re - q_im * c_im
    wout_ref[j, S5_STATE:2 * S5_STATE, :] = -(q_re * c_im + q_im * c_re)

    lgr, thr, fr_re, fr_im = disc(lrr_ref[j], lir_ref[j])
    kk = lax.broadcasted_iota(jnp.int32, (lc, S5_STATE), 0).astype(F32)
    r_re, r_im = powers(kk, lgr, thr)
    g_re = fr_re * r_re - fr_im * r_im
    g_im = fr_re * r_im + fr_im * r_re
    bt_re = btr_ref[j]
    bt_im = bti_ref[j]
    for s in range(lc):
        kpow = lc - 1 - s
        w_re = g_re[kpow:kpow + 1, :]
        w_im = g_im[kpow:kpow + 1, :]
        wsr_ref[j, s * S5_CH:(s + 1) * S5_CH, :] = bt_re * w_re - bt_im * w_im
        wsi_ref[j, s * S5_CH:(s + 1) * S5_CH, :] = bt_re * w_im + bt_im * w_re
    a_re, a_im = powers(float(lc), lgr, thr)
    al_ref[j, 0:1, :] = a_re
    al_ref[j, 1:2, :] = a_im


def _s5_prep(lam_re, lam_im, log_step, b_re, b_im, c_re, c_im, lc):
    g, p, k = S5_GROUPS, S5_STATE, lc * S5_CH
    spec3 = lambda a, b: pl.BlockSpec((S5_GBLK, a, b), lambda i: (i, 0, 0))
    ins = [lam_re.reshape(g, p, 1), lam_im.reshape(g, p, 1), lam_re.reshape(g, 1, p), lam_im.reshape(g, 1, p),
           log_step.reshape(g, 1, 1),
           jnp.swapaxes(b_re, 1, 2), jnp.swapaxes(b_im, 1, 2),
           jnp.tile(jnp.swapaxes(c_re, 1, 2), (1, 1, lc)), jnp.tile(jnp.swapaxes(c_im, 1, 2), (1, 1, lc))]
    kt, wout, wsr, wsi, al = pl.pallas_call(
        functools.partial(_s5_prep_kernel, lc=lc),
        grid=(g // S5_GBLK,),
        in_specs=[spec3(p, 1), spec3(p, 1), spec3(1, p), spec3(1, p), spec3(1, 1),
                  spec3(S5_CH, p), spec3(S5_CH, p), spec3(p, k), spec3(p, k)],
        out_specs=[spec3(S5_CH, k), spec3(2 * p, k), spec3(k, p), spec3(k, p), spec3(2, p)],
        out_shape=[jax.ShapeDtypeStruct((g, S5_CH, k), F32), jax.ShapeDtypeStruct((g, 2 * p, k), F32),
                   jax.ShapeDtypeStruct((g, k, p), F32), jax.ShapeDtypeStruct((g, k, p), F32),
                   jax.ShapeDtypeStruct((g, 2, p), F32)],
        compiler_params=_cparams("parallel"),
        name="s5_prep",
    )(*ins)
    lag = jnp.arange(lc)[None, :] - jnp.arange(lc)[:, None]
    kt4 = kt.reshape(g, S5_CH, lc, S5_CH)
    toep = kt4[:, :, jnp.maximum(lag, 0), :]
    toep = jnp.where((lag >= 0)[None, None, :, :, None], toep, 0.0)
    toep = toep.transpose(0, 2, 1, 3, 4).reshape(g, k, k)
    wst = jnp.concatenate([wsr, wsi, wsi, wsr], axis=-1)
    a_re, a_im = al[:, 0:1, :], al[:, 1:2, :]
    acoef = jnp.concatenate([jnp.concatenate([a_re, a_re], -1), jnp.concatenate([-a_im, a_im], -1),
                             jnp.concatenate([a_im, -a_im], -1)], axis=1)
    return toep.astype(BF16), wst.astype(BF16), wout.astype(BF16), acoef


S5_GBLK = LANES // S5_CH
S5_ROW_TILE = 256


def _s5_kernel(u_ref, h0_ref, h0s_ref, wst_ref, toep_ref, wout_ref, a_ref, y_ref, hf_ref,
               s_scr, hin_scr, h_scr, hs_scr, *, lc, rows, chunks):
    i = pl.program_id(1)

    @pl.when(i == 0)
    def _():
        h_scr[...] = h0_ref[...]
        hs_scr[...] = h0s_ref[...]

    x = u_ref[0]
    ubs = []
    for j in range(S5_GBLK):
        ug = jnp.concatenate([x[:, s * LANES + j * S5_CH:s * LANES + (j + 1) * S5_CH] for s in range(lc)], axis=1)
        ubs.append(ug.astype(BF16))
        s_scr[j] = jnp.dot(ubs[j], wst_ref[j], preferred_element_type=F32)

    def step(r0, carry):
        new = []
        for j in range(S5_GBLK):
            h, hs = carry[2 * j], carry[2 * j + 1]
            hin_scr[j, pl.ds(r0, rows), :] = h
            s = s_scr[j, pl.ds(r0, rows), :]
            a1, a2, a2s = a_ref[j, 0:1, :], a_ref[j, 1:2, :], a_ref[j, 2:3, :]
            new.append(a1 * h + a2 * hs + s[:, :LANES])
            new.append(a1 * hs + a2s * h + s[:, LANES:])
        return tuple(new)

    carry = tuple(ref[j] for j in range(S5_GBLK) for ref in (h_scr, hs_scr))
    if chunks == 1:
        carry = step(0, carry)
    else:
        carry = lax.fori_loop(0, chunks, lambda n, c: step(pl.multiple_of(n * rows, rows), c), carry)
    for j in range(S5_GBLK):
        h_scr[j] = carry[2 * j]
        hs_scr[j] = carry[2 * j + 1]
    hf_ref[...] = h_scr[...]

    ys = [jnp.dot(ubs[j], toep_ref[j], preferred_element_type=F32)
          + jnp.dot(hin_scr[j].astype(BF16), wout_ref[j], preferred_element_type=F32) for j in range(S5_GBLK)]
    for t in range(lc):
        y_ref[0, :, t * LANES:(t + 1) * LANES] = jnp.concatenate(
            [ys[j][:, t * S5_CH:(t + 1) * S5_CH] for j in range(S5_GBLK)], axis=1)


def _s5(ub, h0, ops, lc, rows, chunks):
    toep, wst, wout, acoef = ops
    nb, nr, width = ub.shape
    k = lc * S5_CH
    rt = rows * chunks
    blk = lambda a, b: pl.BlockSpec((S5_GBLK, a, b), lambda gb, i: (gb, 0, 0))
    return pl.pallas_call(
        functools.partial(_s5_kernel, lc=lc, rows=rows, chunks=chunks),
        grid=(nb, nr // rt),
        in_specs=[pl.BlockSpec((1, rt, width), lambda gb, i: (gb, i, 0)),
                  blk(rows, LANES), blk(rows, LANES), blk(k, 2 * LANES), blk(k, k), blk(LANES, k), blk(3, LANES)],
        out_specs=[pl.BlockSpec((1, rt, width), lambda gb, i: (gb, i, 0)), blk(rows, LANES)],
        out_shape=[jax.ShapeDtypeStruct((nb, nr, width), F32), jax.ShapeDtypeStruct((S5_GROUPS, rows, LANES), F32)],
        scratch_shapes=[pltpu.VMEM((S5_GBLK, rt, 2 * LANES), F32), pltpu.VMEM((S5_GBLK, rt, LANES), F32),
                        pltpu.VMEM((S5_GBLK, rows, LANES), F32), pltpu.VMEM((S5_GBLK, rows, LANES), F32)],
        compiler_params=_cparams("parallel", "arbitrary"),
        name="s5_scan",
    )(ub, h0, jnp.roll(h0, S5_STATE, axis=-1), wst, toep, wout, acoef)


def _ssd_kernel(xbc_ref, z_ref, dt_ref, conv0_ref, h0_ref, cw_ref, cb_ref, dtb_ref, alog_ref, dexp_ref, nw_ref,
                y_ref, convn_ref, hn_ref, buf_scr, dt_scr, h_scr, *, q_in, n_chunks):
    q = SSD_CHUNK
    pad = q - q_in
    first = CONV_HALO + pad
    c = pl.program_id(1)

    @pl.when(c == 0)
    def _():
        buf_scr[0:first, :] = jnp.zeros((first, M2_CONV_DIM), F32)
        buf_scr[first - (M2_CONV - 1):first, :] = conv0_ref[0]
        h_scr[...] = h0_ref[0]

    buf_scr[first:CONV_HALO + q, :] = xbc_ref[...]
    conv = cb_ref[...]
    for kk in range(M2_CONV):
        lo = CONV_HALO - (M2_CONV - 1) + kk
        conv = conv + cw_ref[kk:kk + 1, :] * buf_scr[lo:lo + q, :]
    convn_ref[0] = buf_scr[CONV_HALO + q - (M2_CONV - 1):CONV_HALO + q, :]
    if n_chunks > 1:
        buf_scr[0:CONV_HALO, :] = buf_scr[q:q + CONV_HALO, :]
    act = conv * _sigmoid(conv)
    xs = act[:, :M2_WIDTH]

    dtv = dt_ref[...] + dtb_ref[...]
    dt_real = jnp.maximum(dtv, 0.0) + jnp.log1p(jnp.exp(-jnp.abs(dtv)))
    if pad:
        dt_scr[0:pad, :] = jnp.zeros((pad, LANES), F32)
        dt_scr[pad:q, :] = dt_real
        dt = dt_scr[...]
    else:
        dt = dt_real
    a_neg = -jnp.exp(alog_ref[...])
    ri = lax.broadcasted_iota(jnp.int32, (q, q), 0)
    ci = lax.broadcasted_iota(jnp.int32, (q, q), 1)
    tril = ri >= ci
    acs = jnp.dot(tril.astype(F32), dt * a_neg, precision=HIGHEST, preferred_element_type=F32)
    acs_t = acs.T
    dt_t = dt.T
    acs_last = acs[q - 1:q, :]
    eacs = jnp.exp(acs)
    wdec = jnp.exp(acs_last - acs) * dt
    dec_last = jnp.exp(acs_last)
    low = lax.broadcasted_iota(jnp.int32, (q, LANES), 1) < M2_HEAD_DIM
    low1 = low[0:1, :]

    def pair_cols(m, h0):
        return jnp.where(low[0:m.shape[0], :], m[:, h0:h0 + 1], m[:, h0 + 1:h0 + 2])

    y_parts = []
    hpg = M2_HEADS // M2_GROUPS
    gw = hpg * M2_HEAD_DIM
    for g in range(M2_GROUPS):
        bg = act[:, M2_WIDTH + g * M2_D_STATE:M2_WIDTH + (g + 1) * M2_D_STATE]
        cg = act[:, M2_WIDTH + (M2_GROUPS + g) * M2_D_STATE:M2_WIDTH + (M2_GROUPS + g + 1) * M2_D_STATE]
        cgb = cg.astype(BF16)
        cb = lax.dot_general(cgb, bg.astype(BF16), (((1,), (1,)), ((), ())), preferred_element_type=F32)
        hg = h_scr[:, g * gw:(g + 1) * gw]
        yoff = jnp.dot(cgb, hg.astype(BF16), preferred_element_type=F32)
        xw_parts = []
        dec_parts = []
        for pp in range(hpg // 2):
            h0 = g * hpg + 2 * pp
            lanes = slice(h0 * M2_HEAD_DIM, (h0 + 2) * M2_HEAD_DIM)
            w_pair = []
            for h in (h0, h0 + 1):
                seg = acs[:, h:h + 1] - acs_t[h:h + 1, :]
                dec = jnp.exp(jnp.where(tril, seg, -jnp.inf))
                w_pair.append((cb * dec * dt_t[h:h + 1, :]).astype(BF16))
            xp = xs[:, lanes]
            xbd = jnp.concatenate([jnp.where(low, xp, 0.0), jnp.where(low, 0.0, xp)], axis=0).astype(BF16)
            yd = jnp.dot(jnp.concatenate(w_pair, axis=1), xbd, preferred_element_type=F32)
            yo = yoff[:, 2 * pp * M2_HEAD_DIM:(2 * pp + 2) * M2_HEAD_DIM]
            y_parts.append(yd + yo * pair_cols(eacs, h0) + dexp_ref[:, lanes] * xp)
            xw_parts.append(xp * pair_cols(wdec, h0))
            dec_parts.append(jnp.where(low1, dec_last[:, h0:h0 + 1], dec_last[:, h0 + 1:h0 + 2]))
        xw = jnp.concatenate(xw_parts, axis=1).astype(BF16)
        st = jnp.dot(bg.T.astype(BF16), xw, preferred_element_type=F32)
        h_scr[:, g * gw:(g + 1) * gw] = hg * jnp.concatenate(dec_parts, axis=1) + st

    @pl.when(c == n_chunks - 1)
    def _():
        hn_ref[0] = h_scr[...]

    y = jnp.concatenate(y_parts, axis=1)[pad:, :]
    zz = z_ref[...]
    y = y * (zz * _sigmoid(zz))
    outs = []
    for g in range(M2_GROUPS):
        yg = y[:, g * gw:(g + 1) * gw]
        outs.append(yg * lax.rsqrt(jnp.mean(yg * yg, axis=-1, keepdims=True) + EPS))
    y_ref[...] = jnp.concatenate(outs, axis=1) * nw_ref[...]


def _ssd(xbc, z, dt, conv0, h0, prm, n_seq, n_chunks, q_in, shared_init):
    assert q_in == SSD_CHUNK or n_chunks == 1
    assert q_in >= CONV_HALO and q_in % CONV_HALO == 0
    t = xbc.shape[0]
    row = lambda b, c: (b * n_chunks + c, 0)
    fix = lambda b, c: (0, 0)
    init = (lambda b, c: (0, 0, 0)) if shared_init else (lambda b, c: (b, 0, 0))
    per_seq = lambda b, c: (b, 0, 0)
    return pl.pallas_call(
        functools.partial(_ssd_kernel, q_in=q_in, n_chunks=n_chunks),
        grid=(n_seq, n_chunks),
        in_specs=[pl.BlockSpec((q_in, M2_CONV_DIM), row), pl.BlockSpec((q_in, M2_WIDTH), row),
                  pl.BlockSpec((q_in, LANES), row),
                  pl.BlockSpec((1, M2_CONV - 1, M2_CONV_DIM), init),
                  pl.BlockSpec((1, M2_D_STATE, M2_WIDTH), init),
                  pl.BlockSpec((M2_CONV, M2_CONV_DIM), fix), pl.BlockSpec((1, M2_CONV_DIM), fix),
                  pl.BlockSpec((1, LANES), fix), pl.BlockSpec((1, LANES), fix),
                  pl.BlockSpec((1, M2_WIDTH), fix), pl.BlockSpec((1, M2_WIDTH), fix)],
        out_specs=[pl.BlockSpec((q_in, M2_WIDTH), row),
                   pl.BlockSpec((1, M2_CONV - 1, M2_CONV_DIM), per_seq),
                   pl.BlockSpec((1, M2_D_STATE, M2_WIDTH), per_seq)],
        out_shape=[jax.ShapeDtypeStruct((t, M2_WIDTH), F32),
                   jax.ShapeDtypeStruct((n_seq, M2_CONV - 1, M2_CONV_DIM), F32),
                   jax.ShapeDtypeStruct((n_seq, M2_D_STATE, M2_WIDTH), F32)],
        scratch_shapes=[pltpu.VMEM((CONV_HALO + SSD_CHUNK, M2_CONV_DIM), F32),
                        pltpu.VMEM((SSD_CHUNK, LANES), F32),
                        pltpu.VMEM((M2_D_STATE, M2_WIDTH), F32)],
        compiler_params=_cparams("parallel", "arbitrary"),
        name="ssd",
    )(xbc, z, dt, conv0, h0, *prm)


def _post_kernel(y5_ref, u_ref, ym_ref, x_ref, d_ref, wglu_ref, s5nw_ref, wo1_ref, wo2_ref, fnw_ref, wqt_ref,
                 sk_ref, x1_ref, xt_ref, st_ref):
    g = _gelu_tanh(y5_ref[...] + d_ref[...] * u_ref[...])
    o = g * _sigmoid(jnp.dot(g.astype(BF16), wglu_ref[...], preferred_element_type=F32))
    y5n = _rms(o, s5nw_ref[...])
    x1 = (x_ref[...] + jnp.dot(y5n.astype(BF16), wo1_ref[...], preferred_element_type=F32)
          + jnp.dot(ym_ref[...].astype(BF16), wo2_ref[...], preferred_element_type=F32))
    x1_ref[...] = x1
    hn_t = _rms(x1, fnw_ref[...]).T.astype(BF16)
    xt_ref[...] = hn_t
    q_t = jnp.dot(wqt_ref[...], hn_t, preferred_element_type=F32)
    for k in range(2 * PEER_HEADS):
        qk = q_t[k * PEER_HALF:(k + 1) * PEER_HALF, :].astype(BF16)
        st_ref[k] = jnp.dot(sk_ref[k], qk, preferred_element_type=F32) * LOG2E


def _post(y5, u, ym, x, prm, tm):
    t = x.shape[0]
    row = lambda i: (i, 0)
    fix = lambda i: (0, 0)
    nk = 2 * PEER_HEADS
    return pl.pallas_call(
        _post_kernel,
        grid=(t // tm,),
        in_specs=[pl.BlockSpec((tm, D_MODEL), row)] * 4
        + [pl.BlockSpec((1, D_MODEL), fix), pl.BlockSpec((S5_WIDTH, S5_WIDTH), fix), pl.BlockSpec((1, D_MODEL), fix),
           pl.BlockSpec((S5_WIDTH, D_MODEL), fix), pl.BlockSpec((M2_WIDTH, D_MODEL), fix),
           pl.BlockSpec((1, D_MODEL), fix), pl.BlockSpec((nk * PEER_HALF, D_MODEL), fix),
           pl.BlockSpec((nk, PEER_N_KEYS, PEER_HALF), lambda i: (0, 0, 0))],
        out_specs=[pl.BlockSpec((tm, D_MODEL), row), pl.BlockSpec((D_MODEL, tm), lambda i: (0, i)),
                   pl.BlockSpec((nk, PEER_N_KEYS, tm), lambda i: (0, 0, i))],
        out_shape=[jax.ShapeDtypeStruct((t, D_MODEL), F32), jax.ShapeDtypeStruct((D_MODEL, t), BF16),
                   jax.ShapeDtypeStruct((nk, PEER_N_KEYS, t), F32)],
        compiler_params=_cparams("parallel"),
        name="post",
    )(y5, u, ym, x, *prm)


def _sorting_network(n):
    pairs = []
    p = 1
    while p < n:
        k = p
        while k >= 1:
            for j in range(k % p, n - k, 2 * k):
                for i in range(min(k, n - j - k)):
                    if (i + j) // (2 * p) == (i + j + k) // (2 * p):
                        pairs.append((i + j, i + j + k))
            k //= 2
        p *= 2
    return pairs


SUBLANES = 8
_NET16 = _sorting_network(PEER_N_KEYS // SUBLANES)
LOG2E = math.log2(math.e)


def _top16(tiles):
    n = len(tiles)
    tiles = list(tiles)

    def exchange(i, j):
        tiles[i], tiles[j] = jnp.maximum(tiles[i], tiles[j]), jnp.minimum(tiles[i], tiles[j])

    for i, j in _NET16:
        exchange(i, j)
    shift = SUBLANES // 2
    while shift >= 1:
        other = [pltpu.roll(t, shift, axis=0) for t in tiles]
        tiles = [jnp.maximum(tiles[i], other[n - 1 - i]) for i in range(n)]
        dist = n // 2
        while dist >= 1:
            for i in range(n):
                if i & dist == 0:
                    exchange(i, i + dist)
            dist //= 2
        shift //= 2
    return tiles


def _top_values(s):
    n = PEER_N_KEYS // SUBLANES
    tiles = _top16([s[i * SUBLANES:(i + 1) * SUBLANES, :] for i in range(n)])
    rows = lax.broadcasted_iota(jnp.int32, (PEER_TOPK, s.shape[1]), 0)
    out = jnp.zeros((PEER_TOPK, s.shape[1]), F32)
    for r in range(PEER_TOPK):
        out = jnp.where(rows == r, jnp.concatenate([tiles[r], tiles[r]], axis=0), out)
    return out


def _pair_candidates(v1, v2):
    r8 = lax.broadcasted_iota(jnp.int32, (SUBLANES, v1.shape[1]), 0)
    r16 = lax.broadcasted_iota(jnp.int32, (PEER_TOPK, v1.shape[1]), 0)
    neg = -jnp.inf
    lo2 = v2[0:SUBLANES, :]
    return jnp.concatenate([
        v1[0:1, :] + v2,
        v1[1:2, :] + lo2,
        jnp.where(r16 >= 2, v1 + v2[0:1, :], neg),
        jnp.where(r8 >= 2, v1[0:SUBLANES, :] + v2[1:2, :], neg),
        jnp.where((r8 >= 2) & (r8 <= 4), v1[2:3, :] + lo2, neg),
        jnp.where((r8 >= 2) & (r8 <= 3), v1[3:4, :] + lo2, neg),
        jnp.where(r8 == 2, v1[4:5, :] + lo2, neg),
    ], axis=0)


def _route_kernel(st_ref, s1n_ref, s2n_ref, thr_ref):
    def head(h, carry):
        s1 = st_ref[2 * h]
        s2 = st_ref[2 * h + 1]
        v1 = _top_values(s1)
        v2 = _top_values(s2)
        cand = _pair_candidates(v1, v2)
        n_cand = cand.shape[0] // SUBLANES
        pad = [jnp.full((SUBLANES, cand.shape[1]), -jnp.inf, F32)] * (PEER_N_KEYS // SUBLANES - n_cand)
        ranked = _top16([cand[i * SUBLANES:(i + 1) * SUBLANES, :] for i in range(n_cand)] + pad)
        theta = ranked[PEER_TOPK - 1][0:1, :]
        sel = cand >= theta
        m = v1[0:1, :] + v2[0:1, :]
        zsum = jnp.sum(jnp.where(sel, jnp.exp2(cand - m), 0.0), axis=0, keepdims=True)
        off = m + jnp.log(zsum) * LOG2E + 1.0
        s1n_ref[h] = s1 - off
        s2n_ref[h] = s2
        candn = _pair_candidates(v1 - off, v2)
        thr_ref[h] = jnp.min(jnp.where(sel, candn, jnp.inf), axis=0, keepdims=True)
        return carry

    lax.fori_loop(0, PEER_HEADS, head, 0)


def _route(st, tl):
    nk, keys, t = st.shape
    spec = lambda n: pl.BlockSpec((n, keys, tl), lambda i: (0, 0, i))
    return pl.pallas_call(
        _route_kernel,
        grid=(t // tl,),
        in_specs=[spec(nk)],
        out_specs=[spec(PEER_HEADS), spec(PEER_HEADS), pl.BlockSpec((PEER_HEADS, 1, tl), lambda i: (0, 0, i))],
        out_shape=[jax.ShapeDtypeStruct((PEER_HEADS, keys, t), F32), jax.ShapeDtypeStruct((PEER_HEADS, keys, t), F32),
                   jax.ShapeDtypeStruct((PEER_HEADS, 1, t), F32)],
        compiler_params=_cparams("parallel"),
        name="route",
    )(st)


PEER_E1_BLK = 16
MXU_DEPTH = 256


def _peer_kernel(u_ref, vt_ref, xt_ref, s2n_ref, s1n_ref, thr_ref, x1_ref, fnw_ref, y_ref, acc_scr, *, tm):
    k = pl.program_id(1)
    c0 = math.sqrt(2.0 / math.pi)
    c1 = c0 * 0.044715
    per_dot = MXU_DEPTH // PEER_N_KEYS
    n_dots = PEER_E1_BLK // per_dot

    def scores(q):
        return jnp.dot(u_ref[q * MXU_DEPTH:(q + 1) * MXU_DEPTH, :], xt_ref[...], preferred_element_type=F32)

    def gated(q, a):
        wg_rows = []
        for i in range(per_dot):
            r = q * per_dot + i
            rows = slice(i * PEER_N_KEYS, (i + 1) * PEER_N_KEYS)
            wg_cols = []
            for c in range(tm // LANES):
                cols = slice(c * LANES, (c + 1) * LANES)
                w = None
                for h in range(PEER_HEADS):
                    arg = s2n_ref[h, :, cols] + s1n_ref[h, r:r + 1, cols]
                    term = jnp.where(arg >= thr_ref[h, :, cols], jnp.exp2(arg), 0.0)
                    w = term if w is None else w + term
                x = a[rows, cols]
                g = x + x * jnp.tanh(x * (c0 + c1 * (x * x)))
                wg_cols.append((w * g).astype(BF16))
            wg_rows.append(jnp.concatenate(wg_cols, axis=1))
        return jnp.concatenate(wg_rows, axis=0)

    def mixed(q, wg):
        return jnp.dot(vt_ref[0, :, q * MXU_DEPTH:(q + 1) * MXU_DEPTH], wg, preferred_element_type=F32)

    total = None
    a_next = scores(0)
    wg_prev = None
    for q in range(n_dots):
        a_cur = a_next
        if q + 1 < n_dots:
            a_next = scores(q + 1)
        wg = gated(q, a_cur)
        if wg_prev is not None:
            d = mixed(q - 1, wg_prev)
            total = d if total is None else total + d
        wg_prev = wg
    total = total + mixed(n_dots - 1, wg_prev)

    @pl.when(k == 0)
    def _():
        acc_scr[...] = total

    @pl.when(k > 0)
    def _():
        acc_scr[...] += total

    @pl.when(k == pl.num_programs(1) - 1)
    def _():
        y_ref[...] = _rms(x1_ref[...] + acc_scr[...].T, fnw_ref[...])


def _peer(u_bf, vt_bf, xt, route, x1, fnw, tm):
    s1n, s2n, thr = route
    t = x1.shape[0]
    eb = PEER_E1_BLK * PEER_N_KEYS
    per_key = pl.BlockSpec((PEER_HEADS, PEER_N_KEYS, tm), lambda j, k: (0, 0, j))
    per_blk = pl.BlockSpec((PEER_HEADS, PEER_E1_BLK, tm), lambda j, k: (0, k, j))
    return pl.pallas_call(
        functools.partial(_peer_kernel, tm=tm),
        grid=(t // tm, PEER_N_KEYS // PEER_E1_BLK),
        in_specs=[pl.BlockSpec((eb, D_MODEL), lambda j, k: (k, 0)),
                  pl.BlockSpec((1, D_MODEL, eb), lambda j, k: (k, 0, 0)),
                  pl.BlockSpec((D_MODEL, tm), lambda j, k: (0, j)),
                  per_key, per_blk, pl.BlockSpec((PEER_HEADS, 1, tm), lambda j, k: (0, 0, j)),
                  pl.BlockSpec((tm, D_MODEL), lambda j, k: (j, 0)),
                  pl.BlockSpec((1, D_MODEL), lambda j, k: (0, 0))],
        out_specs=pl.BlockSpec((tm, D_MODEL), lambda j, k: (j, 0)),
        out_shape=jax.ShapeDtypeStruct((t, D_MODEL), F32),
        scratch_shapes=[pltpu.VMEM((D_MODEL, tm), F32)],
        compiler_params=_cparams("parallel", "arbitrary"),
        name="peer",
    )(u_bf, vt_bf, xt, s2n, s1n, thr, x1, fnw)


def _tokens_tail(x, post_prm, peer_prm, y5, u, ym):
    x1, xt, st = _post(y5, u, ym, x, post_prm, tm=256)
    u_bf, vt_bf, fnw = peer_prm
    return _peer(u_bf, vt_bf, xt, _route(st, tl=LANES), x1, fnw, tm=512)


def kernel(x_prompt, x_sample, state_s5_re, state_s5_im, state_ssm, state_conv, meta_tokens, norm_mix_w, w_in,
           s5_lambda_re, s5_lambda_im, s5_log_step, s5_b_re, s5_b_im, s5_c_re, s5_c_im, s5_d, s5_w_glu, s5_norm_w,
           m2_conv_w, m2_conv_b, m2_dt_bias, m2_a_log, m2_d, m2_norm_w, w_out, norm_ffn_w, peer_w_q, peer_sub_keys,
           peer_u, peer_v, final_norm_w):
    bp, sp, _ = x_prompt.shape
    bs, ss, _ = x_sample.shape
    g, p = S5_GROUPS, S5_STATE

    w = w_in[0]
    o1, o2, o3 = S5_WIDTH, S5_WIDTH + M2_WIDTH, S5_WIDTH + M2_WIDTH + M2_CONV_DIM
    wu, wz, wx = w[:, :o1].astype(BF16), w[:, o1:o2].astype(BF16), w[:, o2:o3].astype(BF16)
    wd = jnp.pad(w[:, o3:], ((0, 0), (0, LANES - M2_HEADS))).astype(BF16)
    nmw = norm_mix_w[0][None, :]
    pad_h = lambda v: jnp.pad(v, (0, LANES - M2_HEADS))[None, :]
    ssd_prm = (m2_conv_w[0], m2_conv_b[0][None, :], pad_h(m2_dt_bias[0]), pad_h(m2_a_log[0]),
               jnp.repeat(m2_d[0], M2_HEAD_DIM)[None, :], m2_norm_w[0][None, :])
    post_prm = (s5_d[0][None, :], s5_w_glu[0].astype(BF16), s5_norm_w[0][None, :],
                w_out[0][:S5_WIDTH].astype(BF16), w_out[0][S5_WIDTH:].astype(BF16), norm_ffn_w[0][None, :],
                peer_w_q[0].T.astype(BF16),
                peer_sub_keys[0].reshape(2 * PEER_HEADS, PEER_N_KEYS, PEER_HALF).astype(BF16))
    eb = PEER_E1_BLK * PEER_N_KEYS
    vt_blocks = peer_v[0].astype(BF16).reshape(PEER_EXPERTS // eb, eb, D_MODEL).swapaxes(1, 2)
    peer_prm = (peer_u[0].astype(BF16), vt_blocks, final_norm_w[None, :])
    s5_args = (s5_lambda_re[0], s5_lambda_im[0], s5_log_step[0], s5_b_re[0], s5_b_im[0], s5_c_re[0], s5_c_im[0])
    ops16 = _s5_prep(*s5_args, lc=N_META)
    ops8 = _s5_prep(*s5_args, lc=ss)

    lc = N_META
    um, _, xbcm, dtm = _in_proj(meta_tokens, nmw, wu, wz, wx, wd, tm=N_META)
    nb = S5_WIDTH // LANES
    urm = jnp.broadcast_to(um.reshape(lc, nb, LANES).transpose(1, 0, 2).reshape(nb, 1, lc * LANES),
                           (nb, SUBLANES, lc * LANES))
    _, h5m = _s5(urm, jnp.zeros((g, SUBLANES, 2 * p), F32), ops16, lc=lc, rows=SUBLANES, chunks=1)
    _, convm, hm = _ssd(xbcm, jnp.zeros((N_META, M2_WIDTH), F32), dtm,
                        jnp.zeros((1, M2_CONV - 1, M2_CONV_DIM), F32), jnp.zeros((1, M2_D_STATE, M2_WIDTH), F32),
                        ssd_prm, n_seq=1, n_chunks=1, q_in=N_META, shared_init=True)

    xp = x_prompt.reshape(bp * sp, D_MODEL)
    nc = sp // lc
    up, zp, xbcp, dtp = _in_proj(xp, nmw, wu, wz, wx, wd, tm=512)
    urp = up.reshape(bp, nc, lc, nb, LANES).transpose(3, 1, 0, 2, 4).reshape(nb, nc * bp, lc * LANES)
    y5p, h5p = _s5(urp, h5m, ops16, lc=lc, rows=bp, chunks=S5_ROW_TILE // bp)
    y5p = y5p.reshape(nb, nc, bp, lc, LANES).transpose(2, 1, 3, 0, 4).reshape(bp * sp, S5_WIDTH)
    ymp, convp, hp = _ssd(xbcp, zp, dtp, convm, hm, ssd_prm, n_seq=bp, n_chunks=sp // SSD_CHUNK, q_in=SSD_CHUNK,
                          shared_init=True)
    y_prompt = _tokens_tail(xp, post_prm, peer_prm, y5p, up, ymp).reshape(bp, sp, D_MODEL)

    xs = x_sample.reshape(bs * ss, D_MODEL)
    us, zs, xbcs, dts = _in_proj(xs, nmw, wu, wz, wx, wd, tm=512)
    urs = us.reshape(bs, ss, nb, LANES).transpose(2, 0, 1, 3).reshape(nb, bs, ss * LANES)
    h5s0 = jnp.concatenate([state_s5_re[0], state_s5_im[0]], axis=-1).transpose(1, 0, 2)
    y5s, h5s = _s5(urs, h5s0, ops8, lc=ss, rows=bs, chunks=1)
    y5s = y5s.reshape(nb, bs, ss, LANES).transpose(1, 2, 0, 3).reshape(bs * ss, S5_WIDTH)
    hs0 = state_ssm[0].reshape(bs, M2_WIDTH, M2_D_STATE).transpose(0, 2, 1)
    yms, convs, hs = _ssd(xbcs, zs, dts, state_conv[0], hs0, ssd_prm, n_seq=bs, n_chunks=1, q_in=ss,
                          shared_init=False)
    y_sample = _tokens_tail(xs, post_prm, peer_prm, y5s, us, yms).reshape(bs, ss, D_MODEL)

    def s5_state(hf):
        hf = hf.transpose(1, 0, 2)
        return hf[None, :, :, :p], hf[None, :, :, p:]

    def ssm_state(ht):
        return ht.transpose(0, 2, 1).reshape(1, ht.shape[0], M2_HEADS, M2_HEAD_DIM, M2_D_STATE)

    p5r, p5i = s5_state(h5p)
    s5r, s5i = s5_state(h5s)
    return (y_prompt, y_sample, p5r, p5i, ssm_state(hp), convp[None], s5r, s5i, ssm_state(hs), convs[None])
```

```python
import functools
import math

import jax
import jax.numpy as jnp
from jax import lax
from jax.experimental import pallas as pl
from jax.experimental.pallas import tpu as pltpu

F32 = jnp.float32
BF16 = jnp.bfloat16
HIGHEST = lax.Precision.HIGHEST

D_MODEL = 1024
N_META = 16
S5_WIDTH = 1024
S5_CH = 16
S5_GROUPS = S5_WIDTH // S5_CH
S5_STATE = 64
M2_WIDTH = 1024
M2_HEAD_DIM = 64
M2_HEADS = M2_WIDTH // M2_HEAD_DIM
M2_GROUPS = 2
M2_D_STATE = 128
M2_CONV = 4
M2_CONV_DIM = M2_WIDTH + 2 * M2_GROUPS * M2_D_STATE
PEER_HEADS = 8
PEER_N_KEYS = 128
PEER_EXPERTS = PEER_N_KEYS * PEER_N_KEYS
PEER_HALF = 128
PEER_TOPK = 16
EPS = 1e-6

LANES = 128
SSD_CHUNK = 128
CONV_HALO = 8
VMEM_LIMIT = 56 * 1024 * 1024


def _cparams(*sem):
    return pltpu.CompilerParams(dimension_semantics=sem, vmem_limit_bytes=VMEM_LIMIT)


def _sigmoid(x):
    return 1.0 / (1.0 + jnp.exp(-x))


def _gelu_tanh(x):
    c = math.sqrt(2.0 / math.pi)
    return 0.5 * x * (1.0 + jnp.tanh(c * (x + 0.044715 * (x * x * x))))


def _rms(x, w):
    return x * lax.rsqrt(jnp.mean(x * x, axis=-1, keepdims=True) + EPS) * w


def _bdot(a, b):
    return jnp.dot(a.astype(BF16), b.astype(BF16), preferred_element_type=F32)


def _in_proj_kernel(x_ref, nw_ref, wu_ref, wz_ref, wx_ref, wd_ref, u_ref, z_ref, xbc_ref, dt_ref):
    hb = _rms(x_ref[...], nw_ref[...]).astype(BF16)
    u_ref[...] = jnp.dot(hb, wu_ref[...], preferred_element_type=F32)
    z_ref[...] = jnp.dot(hb, wz_ref[...], preferred_element_type=F32)
    xbc_ref[...] = jnp.dot(hb, wx_ref[...], preferred_element_type=F32)
    dt_ref[...] = jnp.dot(hb, wd_ref[...], preferred_element_type=F32)


def _in_proj(x, nw, wu, wz, wx, wd, tm):
    t = x.shape[0]
    row = lambda i: (i, 0)
    fix = lambda i: (0, 0)
    widths = (S5_WIDTH, M2_WIDTH, M2_CONV_DIM, LANES)
    return pl.pallas_call(
        _in_proj_kernel,
        grid=(t // tm,),
        in_specs=[pl.BlockSpec((tm, D_MODEL), row), pl.BlockSpec((1, D_MODEL), fix)]
        + [pl.BlockSpec((D_MODEL, w), fix) for w in widths],
        out_specs=[pl.BlockSpec((tm, w), row) for w in widths],
        out_shape=[jax.ShapeDtypeStruct((t, w), F32) for w in widths],
        compiler_params=_cparams("parallel"),
        name="in_proj",
    )(x, nw, wu, wz, wx, wd)


def _s5_prep_kernel(lrc_ref, lic_ref, lrr_ref, lir_ref, ls_ref, btr_ref, bti_ref, ctr_ref, cti_ref,
                    kt_ref, wout_ref, wsr_ref, wsi_ref, al_ref, *, lc):
    for j in range(S5_GBLK):
        _s5_prep_group(j, lc, lrc_ref, lic_ref, lrr_ref, lir_ref, ls_ref, btr_ref, bti_ref, ctr_ref, cti_ref,
                       kt_ref, wout_ref, wsr_ref, wsi_ref, al_ref)


def _s5_prep_group(j, lc, lrc_ref, lic_ref, lrr_ref, lir_ref, ls_ref, btr_ref, bti_ref, ctr_ref, cti_ref,
                   kt_ref, wout_ref, wsr_ref, wsi_ref, al_ref):
    k = lc * S5_CH
    step = jnp.exp(ls_ref[j])

    def disc(lr, li):
        lg = lr * step
        th = li * step
        mag = jnp.exp(lg)
        ab_re = mag * jnp.cos(th)
        ab_im = mag * jnp.sin(th)
        den = lr * lr + li * li
        f_re = ((ab_re - 1.0) * lr + ab_im * li) / den
        f_im = (ab_im * lr - (ab_re - 1.0) * li) / den
        return lg, th, f_re, f_im

    lg, th, f_re, f_im = disc(lrc_ref[j], lic_ref[j])
    tau = lax.shift_right_logical(lax.broadcasted_iota(jnp.int32, (S5_STATE, k), 1), 4).astype(F32)
    c_re = ctr_ref[j]
    c_im = cti_ref[j]

    def powers(t, lg_, th_):
        mag = jnp.exp(lg_ * t)
        return mag * jnp.cos(th_ * t), mag * jnp.sin(th_ * t)

    p_re, p_im = powers(tau, lg, th)
    fa_re = f_re * p_re - f_im * p_im
    fa_im = f_re * p_im + f_im * p_re
    fca_re = fa_re * c_re - fa_im * c_im
    fca_im = fa_re * c_im + fa_im * c_re
    kt_ref[j] = (jnp.dot(btr_ref[j], fca_re, precision=HIGHEST, preferred_element_type=F32)
                 - jnp.dot(bti_ref[j], fca_im, precision=HIGHEST, preferred_element_type=F32))
    q_re, q_im = powers(tau + 1.0, lg, th)
    wout_ref[j, 0:S5_STATE, :] = q_re * c_re - q_im * c_im
    wout_ref[j, S5_STATE:2 * S5_STATE, :] = -(q_re * c_im + q_im * c_re)

    lgr, thr, fr_re, fr_im = disc(lrr_ref[j], lir_ref[j])
    kk = lax.broadcasted_iota(jnp.int32, (lc, S5_STATE), 0).astype(F32)
    r_re, r_im = powers(kk, lgr, thr)
    g_re = fr_re * r_re - fr_im * r_im
    g_im = fr_re * r_im + fr_im * r_re
    bt_re = btr_ref[j]
    bt_im = bti_ref[j]
    for s in range(lc):
        kpow = lc - 1 - s
        w_re = g_re[kpow:kpow + 1, :]
        w_im = g_im[kpow:kpow + 1, :]
        wsr_ref[j, s * S5_CH:(s + 1) * S5_CH, :] = bt_re * w_re - bt_im * w_im
        wsi_ref[j, s * S5_CH:(s + 1) * S5_CH, :] = bt_re * w_im + bt_im * w_re
    a_re, a_im = powers(float(lc), lgr, thr)
    al_ref[j, 0:1, :] = a_re
    al_ref[j, 1:2, :] = a_im


def _s5_prep(lam_re, lam_im, log_step, b_re, b_im, c_re, c_im, lc):
    g, p, k = S5_GROUPS, S5_STATE, lc * S5_CH
    spec3 = lambda a, b: pl.BlockSpec((S5_GBLK, a, b), lambda i: (i, 0, 0))
    ins = [lam_re.reshape(g, p, 1), lam_im.reshape(g, p, 1), lam_re.reshape(g, 1, p), lam_im.reshape(g, 1, p),
           log_step.reshape(g, 1, 1),
           jnp.swapaxes(b_re, 1, 2), jnp.swapaxes(b_im, 1, 2),
           jnp.tile(jnp.swapaxes(c_re, 1, 2), (1, 1, lc)), jnp.tile(jnp.swapaxes(c_im, 1, 2), (1, 1, lc))]
    kt, wout, wsr, wsi, al = pl.pallas_call(
        functools.partial(_s5_prep_kernel, lc=lc),
        grid=(g // S5_GBLK,),
        in_specs=[spec3(p, 1), spec3(p, 1), spec3(1, p), spec3(1, p), spec3(1, 1),
                  spec3(S5_CH, p), spec3(S5_CH, p), spec3(p, k), spec3(p, k)],
        out_specs=[spec3(S5_CH, k), spec3(2 * p, k), spec3(k, p), spec3(k, p), spec3(2, p)],
        out_shape=[jax.ShapeDtypeStruct((g, S5_CH, k), F32), jax.ShapeDtypeStruct((g, 2 * p, k), F32),
                   jax.ShapeDtypeStruct((g, k, p), F32), jax.ShapeDtypeStruct((g, k, p), F32),
                   jax.ShapeDtypeStruct((g, 2, p), F32)],
        compiler_params=_cparams("parallel"),
        name="s5_prep",
    )(*ins)
    lag = jnp.arange(lc)[None, :] - jnp.arange(lc)[:, None]
    kt4 = kt.reshape(g, S5_CH, lc, S5_CH)
    toep = kt4[:, :, jnp.maximum(lag, 0), :]
    toep = jnp.where((lag >= 0)[None, None, :, :, None], toep, 0.0)
    toep = toep.transpose(0, 2, 1, 3, 4).reshape(g, k, k)
    wst = jnp.concatenate([wsr, wsi, wsi, wsr], axis=-1)
    a_re, a_im = al[:, 0:1, :], al[:, 1:2, :]
    acoef = jnp.concatenate([jnp.concatenate([a_re, a_re], -1), jnp.concatenate([-a_im, a_im], -1),
                             jnp.concatenate([a_im, -a_im], -1)], axis=1)
    return toep.astype(BF16), wst.astype(BF16), wout.astype(BF16), acoef


S5_GBLK = LANES // S5_CH
S5_ROW_TILE = 256


def _s5_kernel(u_ref, h0_ref, h0s_ref, wst_ref, toep_ref, wout_ref, a_ref, y_ref, hf_ref,
               s_scr, hin_scr, h_scr, hs_scr, *, lc, rows, chunks):
    i = pl.program_id(1)

    @pl.when(i == 0)
    def _():
        h_scr[...] = h0_ref[...]
        hs_scr[...] = h0s_ref[...]

    x = u_ref[0]
    ubs = []
    for j in range(S5_GBLK):
        ug = jnp.concatenate([x[:, s * LANES + j * S5_CH:s * LANES + (j + 1) * S5_CH] for s in range(lc)], axis=1)
        ubs.append(ug.astype(BF16))
        s_scr[j] = jnp.dot(ubs[j], wst_ref[j], preferred_element_type=F32)

    def step(r0, carry):
        new = []
        for j in range(S5_GBLK):
            h, hs = carry[2 * j], carry[2 * j + 1]
            hin_scr[j, pl.ds(r0, rows), :] = h
            s = s_scr[j, pl.ds(r0, rows), :]
            a1, a2, a2s = a_ref[j, 0:1, :], a_ref[j, 1:2, :], a_ref[j, 2:3, :]
            new.append(a1 * h + a2 * hs + s[:, :LANES])
            new.append(a1 * hs + a2s * h + s[:, LANES:])
        return tuple(new)

    carry = tuple(ref[j] for j in range(S5_GBLK) for ref in (h_scr, hs_scr))
    if chunks == 1:
        carry = step(0, carry)
    else:
        carry = lax.fori_loop(0, chunks, lambda n, c: step(pl.multiple_of(n * rows, rows), c), carry)
    for j in range(S5_GBLK):
        h_scr[j] = carry[2 * j]
        hs_scr[j] = carry[2 * j + 1]
    hf_ref[...] = h_scr[...]

    ys = [jnp.dot(ubs[j], toep_ref[j], preferred_element_type=F32)
          + jnp.dot(hin_scr[j].astype(BF16), wout_ref[j], preferred_element_type=F32) for j in range(S5_GBLK)]
    for t in range(lc):
        y_ref[0, :, t * LANES:(t + 1) * LANES] = jnp.concatenate(
            [ys[j][:, t * S5_CH:(t + 1) * S5_CH] for j in range(S5_GBLK)], axis=1)


def _s5(ub, h0, ops, lc, rows, chunks):
    toep, wst, wout, acoef = ops
    nb, nr, width = ub.shape
    k = lc * S5_CH
    rt = rows * chunks
    blk = lambda a, b: pl.BlockSpec((S5_GBLK, a, b), lambda gb, i: (gb, 0, 0))
    return pl.pallas_call(
        functools.partial(_s5_kernel, lc=lc, rows=rows, chunks=chunks),
        grid=(nb, nr // rt),
        in_specs=[pl.BlockSpec((1, rt, width), lambda gb, i: (gb, i, 0)),
                  blk(rows, LANES), blk(rows, LANES), blk(k, 2 * LANES), blk(k, k), blk(LANES, k), blk(3, LANES)],
        out_specs=[pl.BlockSpec((1, rt, width), lambda gb, i: (gb, i, 0)), blk(rows, LANES)],
        out_shape=[jax.ShapeDtypeStruct((nb, nr, width), F32), jax.ShapeDtypeStruct((S5_GROUPS, rows, LANES), F32)],
        scratch_shapes=[pltpu.VMEM((S5_GBLK, rt, 2 * LANES), F32), pltpu.VMEM((S5_GBLK, rt, LANES), F32),
                        pltpu.VMEM((S5_GBLK, rows, LANES), F32), pltpu.VMEM((S5_GBLK, rows, LANES), F32)],
        compiler_params=_cparams("parallel", "arbitrary"),
        name="s5_scan",
    )(ub, h0, jnp.roll(h0, S5_STATE, axis=-1), wst, toep, wout, acoef)


def _ssd_kernel(xbc_ref, z_ref, dt_ref, conv0_ref, h0_ref, cw_ref, cb_ref, dtb_ref, alog_ref, dexp_ref, nw_ref,
                y_ref, convn_ref, hn_ref, buf_scr, dt_scr, h_scr, *, q_in, n_chunks):
    q = SSD_CHUNK
    pad = q - q_in
    first = CONV_HALO + pad
    c = pl.program_id(1)

    @pl.when(c == 0)
    def _():
        buf_scr[0:first, :] = jnp.zeros((first, M2_CONV_DIM), F32)
        buf_scr[first - (M2_CONV - 1):first, :] = conv0_ref[0]
        h_scr[...] = h0_ref[0]

    buf_scr[first:CONV_HALO + q, :] = xbc_ref[...]
    conv = cb_ref[...]
    for kk in range(M2_CONV):
        lo = CONV_HALO - (M2_CONV - 1) + kk
        conv = conv + cw_ref[kk:kk + 1, :] * buf_scr[lo:lo + q, :]
    convn_ref[0] = buf_scr[CONV_HALO + q - (M2_CONV - 1):CONV_HALO + q, :]
    if n_chunks > 1:
        buf_scr[0:CONV_HALO, :] = buf_scr[q:q + CONV_HALO, :]
    act = conv * _sigmoid(conv)
    xs = act[:, :M2_WIDTH]

    dtv = dt_ref[...] + dtb_ref[...]
    dt_real = jnp.maximum(dtv, 0.0) + jnp.log1p(jnp.exp(-jnp.abs(dtv)))
    if pad:
        dt_scr[0:pad, :] = jnp.zeros((pad, LANES), F32)
        dt_scr[pad:q, :] = dt_real
        dt = dt_scr[...]
    else:
        dt = dt_real
    a_neg = -jnp.exp(alog_ref[...])
    ri = lax.broadcasted_iota(jnp.int32, (q, q), 0)
    ci = lax.broadcasted_iota(jnp.int32, (q, q), 1)
    tril = ri >= ci
    acs = jnp.dot(tril.astype(F32), dt * a_neg, precision=HIGHEST, preferred_element_type=F32)
    acs_t = acs.T
    dt_t = dt.T
    acs_last = acs[q - 1:q, :]
    eacs = jnp.exp(acs)
    wdec = jnp.exp(acs_last - acs) * dt
    dec_last = jnp.exp(acs_last)
    low = lax.broadcasted_iota(jnp.int32, (q, LANES), 1) < M2_HEAD_DIM
    low1 = low[0:1, :]

    def pair_cols(m, h0):
        return jnp.where(low[0:m.shape[0], :], m[:, h0:h0 + 1], m[:, h0 + 1:h0 + 2])

    y_parts = []
    hpg = M2_HEADS // M2_GROUPS
    gw = hpg * M2_HEAD_DIM
    for g in range(M2_GROUPS):
        bg = act[:, M2_WIDTH + g * M2_D_STATE:M2_WIDTH + (g + 1) * M2_D_STATE]
        cg = act[:, M2_WIDTH + (M2_GROUPS + g) * M2_D_STATE:M2_WIDTH + (M2_GROUPS + g + 1) * M2_D_STATE]
        cgb = cg.astype(BF16)
        cb = lax.dot_general(cgb, bg.astype(BF16), (((1,), (1,)), ((), ())), preferred_element_type=F32)
        hg = h_scr[:, g * gw:(g + 1) * gw]
        yoff = jnp.dot(cgb, hg.astype(BF16), preferred_element_type=F32)
        xw_parts = []
        dec_parts = []
        for pp in range(hpg // 2):
            h0 = g * hpg + 2 * pp
            lanes = slice(h0 * M2_HEAD_DIM, (h0 + 2) * M2_HEAD_DIM)
            w_pair = []
            for h in (h0, h0 + 1):
                seg = acs[:, h:h + 1] - acs_t[h:h + 1, :]
                dec = jnp.exp(jnp.where(tril, seg, -jnp.inf))
                w_pair.append((cb * dec * dt_t[h:h + 1, :]).astype(BF16))
            xp = xs[:, lanes]
            xbd = jnp.concatenate([jnp.where(low, xp, 0.0), jnp.where(low, 0.0, xp)], axis=0).astype(BF16)
            yd = jnp.dot(jnp.concatenate(w_pair, axis=1), xbd, preferred_element_type=F32)
            yo = yoff[:, 2 * pp * M2_HEAD_DIM:(2 * pp + 2) * M2_HEAD_DIM]
            y_parts.append(yd + yo * pair_cols(eacs, h0) + dexp_ref[:, lanes] * xp)
            xw_parts.append(xp * pair_cols(wdec, h0))
            dec_parts.append(jnp.where(low1, dec_last[:, h0:h0 + 1], dec_last[:, h0 + 1:h0 + 2]))
        xw = jnp.concatenate(xw_parts, axis=1).astype(BF16)
        st = jnp.dot(bg.T.astype(BF16), xw, preferred_element_type=F32)
        h_scr[:, g * gw:(g + 1) * gw] = hg * jnp.concatenate(dec_parts, axis=1) + st

    @pl.when(c == n_chunks - 1)
    def _():
        hn_ref[0] = h_scr[...]

    y = jnp.concatenate(y_parts, axis=1)[pad:, :]
    zz = z_ref[...]
    y = y * (zz * _sigmoid(zz))
    outs = []
    for g in range(M2_GROUPS):
        yg = y[:, g * gw:(g + 1) * gw]
        outs.append(yg * lax.rsqrt(jnp.mean(yg * yg, axis=-1, keepdims=True) + EPS))
    y_ref[...] = jnp.concatenate(outs, axis=1) * nw_ref[...]


def _ssd(xbc, z, dt, conv0, h0, prm, n_seq, n_chunks, q_in, shared_init):
    assert q_in == SSD_CHUNK or n_chunks == 1
    assert q_in >= CONV_HALO and q_in % CONV_HALO == 0
    t = xbc.shape[0]
    row = lambda b, c: (b * n_chunks + c, 0)
    fix = lambda b, c: (0, 0)
    init = (lambda b, c: (0, 0, 0)) if shared_init else (lambda b, c: (b, 0, 0))
    per_seq = lambda b, c: (b, 0, 0)
    return pl.pallas_call(
        functools.partial(_ssd_kernel, q_in=q_in, n_chunks=n_chunks),
        grid=(n_seq, n_chunks),
        in_specs=[pl.BlockSpec((q_in, M2_CONV_DIM), row), pl.BlockSpec((q_in, M2_WIDTH), row),
                  pl.BlockSpec((q_in, LANES), row),
                  pl.BlockSpec((1, M2_CONV - 1, M2_CONV_DIM), init),
                  pl.BlockSpec((1, M2_D_STATE, M2_WIDTH), init),
                  pl.BlockSpec((M2_CONV, M2_CONV_DIM), fix), pl.BlockSpec((1, M2_CONV_DIM), fix),
                  pl.BlockSpec((1, LANES), fix), pl.BlockSpec((1, LANES), fix),
                  pl.BlockSpec((1, M2_WIDTH), fix), pl.BlockSpec((1, M2_WIDTH), fix)],
        out_specs=[pl.BlockSpec((q_in, M2_WIDTH), row),
                   pl.BlockSpec((1, M2_CONV - 1, M2_CONV_DIM), per_seq),
                   pl.BlockSpec((1, M2_D_STATE, M2_WIDTH), per_seq)],
        out_shape=[jax.ShapeDtypeStruct((t, M2_WIDTH), F32),
                   jax.ShapeDtypeStruct((n_seq, M2_CONV - 1, M2_CONV_DIM), F32),
                   jax.ShapeDtypeStruct((n_seq, M2_D_STATE, M2_WIDTH), F32)],
        scratch_shapes=[pltpu.VMEM((CONV_HALO + SSD_CHUNK, M2_CONV_DIM), F32),
                        pltpu.VMEM((SSD_CHUNK, LANES), F32),
                        pltpu.VMEM((M2_D_STATE, M2_WIDTH), F32)],
        compiler_params=_cparams("parallel", "arbitrary"),
        name="ssd",
    )(xbc, z, dt, conv0, h0, *prm)


def _ssd_packed_kernel(xbc_ref, hist_ref, z_ref, dt_ref, h0_ref, cw_ref, cb_ref, dtb_ref, alog_ref, dexp_ref, nw_ref,
                       y_ref, hn_ref, buf_scr, *, seq_len):
    q = SSD_CHUNK
    n = M2_D_STATE
    ns = q // seq_len
    shift = seq_len.bit_length() - 1
    hpg = M2_HEADS // M2_GROUPS
    gw = hpg * M2_HEAD_DIM
    cdim = gw + 2 * n
    x = xbc_ref[0]
    buf_scr[0:CONV_HALO, :] = jnp.zeros((CONV_HALO, cdim), F32)
    buf_scr[CONV_HALO:CONV_HALO + q, :] = x
    pos = lax.broadcasted_iota(jnp.int32, (q, cdim), 0) & (seq_len - 1)
    conv = cb_ref[0] + cw_ref[0, M2_CONV - 1:M2_CONV, :] * x
    for k in range(1, M2_CONV):
        prev = jnp.where(pos >= k, buf_scr[CONV_HALO - k:CONV_HALO - k + q, :], hist_ref[k - 1, 0])
        conv = conv + cw_ref[0, M2_CONV - 1 - k:M2_CONV - k, :] * prev
    act = conv * _sigmoid(conv)
    xs = act[:, :gw]
    bg = act[:, gw:gw + n]
    cg = act[:, gw + n:]

    dtv = dt_ref[0] + dtb_ref[0]
    dt = jnp.maximum(dtv, 0.0) + jnp.log1p(jnp.exp(-jnp.abs(dtv)))
    a_neg = -jnp.exp(alog_ref[0])
    ri = lax.broadcasted_iota(jnp.int32, (q, q), 0)
    ci = lax.broadcasted_iota(jnp.int32, (q, q), 1)
    same = lax.shift_right_logical(ri, shift) == lax.shift_right_logical(ci, shift)
    causal = (ri >= ci) & same
    dta = dt * a_neg
    acs = jnp.dot(causal.astype(F32), dta, precision=HIGHEST, preferred_element_type=F32)
    tot = jnp.dot(same.astype(F32), dta, precision=HIGHEST, preferred_element_type=F32)
    acs_t = acs.T
    dt_t = dt.T
    eacs = jnp.exp(acs)
    wdec = jnp.exp(tot - acs) * dt
    dec_tot = jnp.exp(tot)
    low = lax.broadcasted_iota(jnp.int32, (q, LANES), 1) < M2_HEAD_DIM

    def pair_cols(m, h0):
        return jnp.where(low, m[:, h0:h0 + 1], m[:, h0 + 1:h0 + 2])

    cgb = cg.astype(BF16)
    cb = lax.dot_general(cgb, bg.astype(BF16), (((1,), (1,)), ((), ())), preferred_element_type=F32)
    row_seq = lax.shift_right_logical(lax.broadcasted_iota(jnp.int32, (q, n), 0), shift)
    c_blocks = jnp.concatenate([jnp.where(row_seq == s, cg, 0.0) for s in range(ns)], axis=1).astype(BF16)
    h_all = h0_ref[...].reshape(ns * n, gw)
    yoff = jnp.dot(c_blocks, h_all.astype(BF16), preferred_element_type=F32)
    y_parts, xw_parts, dec_parts = [], [], []
    for pp in range(hpg // 2):
        h0 = 2 * pp
        lanes = slice(h0 * M2_HEAD_DIM, (h0 + 2) * M2_HEAD_DIM)
        w_pair = []
        for h in (h0, h0 + 1):
            seg = acs[:, h:h + 1] - acs_t[h:h + 1, :]
            dec = jnp.exp(jnp.where(causal, seg, -jnp.inf))
            w_pair.append((cb * dec * dt_t[h:h + 1, :]).astype(BF16))
        xp = xs[:, lanes]
        xbd = jnp.concatenate([jnp.where(low, xp, 0.0), jnp.where(low, 0.0, xp)], axis=0).astype(BF16)
        yd = jnp.dot(jnp.concatenate(w_pair, axis=1), xbd, preferred_element_type=F32)
        y_parts.append(yd + yoff[:, lanes] * pair_cols(eacs, h0) + dexp_ref[0, :, lanes] * xp)
        xw_parts.append(xp * pair_cols(wdec, h0))
        dec_parts.append(pair_cols(dec_tot, h0))
    xw = jnp.concatenate(xw_parts, axis=1).astype(BF16)
    col_seq = lax.shift_right_logical(lax.broadcasted_iota(jnp.int32, (n, q), 1), shift)
    bgt = bg.T
    b_blocks = jnp.concatenate([jnp.where(col_seq == s, bgt, 0.0) for s in range(ns)], axis=0).astype(BF16)
    st = jnp.dot(b_blocks, xw, preferred_element_type=F32)
    dec_rows = jnp.concatenate(dec_parts, axis=1)
    for s in range(ns):
        hn_ref[s] = h0_ref[s] * dec_rows[s * seq_len:s * seq_len + 1, :] + st[s * n:(s + 1) * n, :]

    zz = z_ref[...]
    y = jnp.concatenate(y_parts, axis=1) * (zz * _sigmoid(zz))
    y_ref[...] = y * lax.rsqrt(jnp.mean(y * y, axis=-1, keepdims=True) + EPS) * nw_ref[0]


def _by_group(a):
    gw = M2_WIDTH // M2_GROUPS
    n = M2_D_STATE
    return jnp.stack([jnp.concatenate([a[..., g * gw:(g + 1) * gw],
                                       a[..., M2_WIDTH + g * n:M2_WIDTH + (g + 1) * n],
                                       a[..., M2_WIDTH + (M2_GROUPS + g) * n:M2_WIDTH + (M2_GROUPS + g + 1) * n]],
                                      axis=-1) for g in range(M2_GROUPS)])


def _ssd_packed(xbc, z, dt, conv0, h0, prm, n_seq, seq_len):
    cw, cb, dtb, alog, dexp, nw = prm
    t = xbc.shape[0]
    hpg = M2_HEADS // M2_GROUPS
    gw = hpg * M2_HEAD_DIM
    cdim = gw + 2 * M2_D_STATE
    heads = lambda a: jnp.stack([jnp.pad(a[..., g * hpg:(g + 1) * hpg], [(0, 0)] * (a.ndim - 1) + [(0, LANES - hpg)])
                                 for g in range(M2_GROUPS)])
    halves = lambda a: jnp.stack([a[..., g * gw:(g + 1) * gw] for g in range(M2_GROUPS)])
    hist = jnp.stack([_by_group(jnp.pad(conv0[:, M2_CONV - 1 - k:, :], ((0, 0), (0, seq_len - k), (0, 0)))
                                .reshape(t, M2_CONV_DIM)) for k in range(1, M2_CONV)])
    per_g = lambda r, c: pl.BlockSpec((1, r, c), lambda i, g: (g, 0, 0))
    rows_g = lambda c: pl.BlockSpec((1, SSD_CHUNK, c), lambda i, g: (g, i, 0))
    ns = SSD_CHUNK // seq_len
    state = pl.BlockSpec((ns, M2_D_STATE, gw), lambda i, g: (i, 0, g))
    y, hn = pl.pallas_call(
        functools.partial(_ssd_packed_kernel, seq_len=seq_len),
        grid=(t // SSD_CHUNK, M2_GROUPS),
        in_specs=[rows_g(cdim),
                  pl.BlockSpec((M2_CONV - 1, 1, SSD_CHUNK, cdim), lambda i, g: (0, g, i, 0)),
                  pl.BlockSpec((SSD_CHUNK, gw), lambda i, g: (i, g)),
                  rows_g(LANES), state,
                  per_g(M2_CONV, cdim), per_g(1, cdim), per_g(1, LANES), per_g(1, LANES), per_g(1, gw), per_g(1, gw)],
        out_specs=[pl.BlockSpec((SSD_CHUNK, gw), lambda i, g: (i, g)), state],
        out_shape=[jax.ShapeDtypeStruct((t, M2_WIDTH), F32), jax.ShapeDtypeStruct((n_seq, M2_D_STATE, M2_WIDTH), F32)],
        scratch_shapes=[pltpu.VMEM((CONV_HALO + SSD_CHUNK, cdim), F32)],
        compiler_params=_cparams("parallel", "parallel"),
        name="ssd_packed",
    )(_by_group(xbc), hist, z, heads(dt), h0, _by_group(cw), _by_group(cb), heads(dtb), heads(alog),
      halves(dexp), halves(nw))
    conv_new = xbc.reshape(n_seq, seq_len, M2_CONV_DIM)[:, seq_len - (M2_CONV - 1):, :]
    return y, conv_new, hn


def _post_kernel(y5_ref, u_ref, ym_ref, x_ref, d_ref, wglu_ref, s5nw_ref, wo1_ref, wo2_ref, fnw_ref, wqt_ref,
                 sk_ref, x1_ref, xt_ref, st_ref):
    g = _gelu_tanh(y5_ref[...] + d_ref[...] * u_ref[...])
    o = g * _sigmoid(jnp.dot(g.astype(BF16), wglu_ref[...], preferred_element_type=F32))
    y5n = _rms(o, s5nw_ref[...])
    x1 = (x_ref[...] + jnp.dot(y5n.astype(BF16), wo1_ref[...], preferred_element_type=F32)
          + jnp.dot(ym_ref[...].astype(BF16), wo2_ref[...], preferred_element_type=F32))
    x1_ref[...] = x1
    hn_t = _rms(x1, fnw_ref[...]).T.astype(BF16)
    xt_ref[...] = hn_t
    q_t = jnp.dot(wqt_ref[...], hn_t, preferred_element_type=F32)
    for k in range(2 * PEER_HEADS):
        qk = q_t[k * PEER_HALF:(k + 1) * PEER_HALF, :].astype(BF16)
        st_ref[k] = jnp.dot(sk_ref[k], qk, preferred_element_type=F32) * LOG2E


def _post(y5, u, ym, x, prm, tm):
    t = x.shape[0]
    row = lambda i: (i, 0)
    fix = lambda i: (0, 0)
    nk = 2 * PEER_HEADS
    return pl.pallas_call(
        _post_kernel,
        grid=(t // tm,),
        in_specs=[pl.BlockSpec((tm, D_MODEL), row)] * 4
        + [pl.BlockSpec((1, D_MODEL), fix), pl.BlockSpec((S5_WIDTH, S5_WIDTH), fix), pl.BlockSpec((1, D_MODEL), fix),
           pl.BlockSpec((S5_WIDTH, D_MODEL), fix), pl.BlockSpec((M2_WIDTH, D_MODEL), fix),
           pl.BlockSpec((1, D_MODEL), fix), pl.BlockSpec((nk * PEER_HALF, D_MODEL), fix),
           pl.BlockSpec((nk, PEER_N_KEYS, PEER_HALF), lambda i: (0, 0, 0))],
        out_specs=[pl.BlockSpec((tm, D_MODEL), row), pl.BlockSpec((D_MODEL, tm), lambda i: (0, i)),
                   pl.BlockSpec((nk, PEER_N_KEYS, tm), lambda i: (0, 0, i))],
        out_shape=[jax.ShapeDtypeStruct((t, D_MODEL), F32), jax.ShapeDtypeStruct((D_MODEL, t), BF16),
                   jax.ShapeDtypeStruct((nk, PEER_N_KEYS, t), F32)],
        compiler_params=_cparams("parallel"),
        name="post",
    )(y5, u, ym, x, *prm)


def _sorting_network(n):
    pairs = []
    p = 1
    while p < n:
        k = p
        while k >= 1:
            for j in range(k % p, n - k, 2 * k):
                for i in range(min(k, n - j - k)):
                    if (i + j) // (2 * p) == (i + j + k) // (2 * p):
                        pairs.append((i + j, i + j + k))
            k //= 2
        p *= 2
    return pairs


SUBLANES = 8
_NET16 = _sorting_network(PEER_N_KEYS // SUBLANES)
LOG2E = math.log2(math.e)


def _top16(tiles):
    n = len(tiles)
    tiles = list(tiles)

    def exchange(i, j):
        tiles[i], tiles[j] = jnp.maximum(tiles[i], tiles[j]), jnp.minimum(tiles[i], tiles[j])

    for i, j in _NET16:
        exchange(i, j)
    shift = SUBLANES // 2
    while shift >= 1:
        other = [pltpu.roll(t, shift, axis=0) for t in tiles]
        tiles = [jnp.maximum(tiles[i], other[n - 1 - i]) for i in range(n)]
        dist = n // 2
        while dist >= 1:
            for i in range(n):
                if i & dist == 0:
                    exchange(i, i + dist)
            dist //= 2
        shift //= 2
    return tiles


def _top_values(s):
    n = PEER_N_KEYS // SUBLANES
    tiles = _top16([s[i * SUBLANES:(i + 1) * SUBLANES, :] for i in range(n)])
    rows = lax.broadcasted_iota(jnp.int32, (PEER_TOPK, s.shape[1]), 0)
    out = jnp.zeros((PEER_TOPK, s.shape[1]), F32)
    for r in range(PEER_TOPK):
        out = jnp.where(rows == r, jnp.concatenate([tiles[r], tiles[r]], axis=0), out)
    return out


def _pair_candidates(v1, v2):
    r8 = lax.broadcasted_iota(jnp.int32, (SUBLANES, v1.shape[1]), 0)
    r16 = lax.broadcasted_iota(jnp.int32, (PEER_TOPK, v1.shape[1]), 0)
    neg = -jnp.inf
    lo2 = v2[0:SUBLANES, :]
    return jnp.concatenate([
        v1[0:1, :] + v2,
        v1[1:2, :] + lo2,
        jnp.where(r16 >= 2, v1 + v2[0:1, :], neg),
        jnp.where(r8 >= 2, v1[0:SUBLANES, :] + v2[1:2, :], neg),
        jnp.where((r8 >= 2) & (r8 <= 4), v1[2:3, :] + lo2, neg),
        jnp.where((r8 >= 2) & (r8 <= 3), v1[3:4, :] + lo2, neg),
        jnp.where(r8 == 2, v1[4:5, :] + lo2, neg),
    ], axis=0)


def _route_kernel(st_ref, s1n_ref, s2n_ref, thr_ref):
    def head(h, carry):
        s1 = st_ref[2 * h]
        s2 = st_ref[2 * h + 1]
        v1 = _top_values(s1)
        v2 = _top_values(s2)
        cand = _pair_candidates(v1, v2)
        n_cand = cand.shape[0] // SUBLANES
        pad = [jnp.full((SUBLANES, cand.shape[1]), -jnp.inf, F32)] * (PEER_N_KEYS // SUBLANES - n_cand)
        ranked = _top16([cand[i * SUBLANES:(i + 1) * SUBLANES, :] for i in range(n_cand)] + pad)
        theta = ranked[PEER_TOPK - 1][0:1, :]
        sel = cand >= theta
        m = v1[0:1, :] + v2[0:1, :]
        zsum = jnp.sum(jnp.where(sel, jnp.exp2(cand - m), 0.0), axis=0, keepdims=True)
        off = m + jnp.log(zsum) * LOG2E + 1.0
        s1n_ref[h] = s1 - off
        s2n_ref[h] = s2
        candn = _pair_candidates(v1 - off, v2)
        thr_ref[h] = jnp.min(jnp.where(sel, candn, jnp.inf), axis=0, keepdims=True)
        return carry

    lax.fori_loop(0, PEER_HEADS, head, 0)


def _route(st, tl):
    nk, keys, t = st.shape
    spec = lambda n: pl.BlockSpec((n, keys, tl), lambda i: (0, 0, i))
    return pl.pallas_call(
        _route_kernel,
        grid=(t // tl,),
        in_specs=[spec(nk)],
        out_specs=[spec(PEER_HEADS), spec(PEER_HEADS), pl.BlockSpec((PEER_HEADS, 1, tl), lambda i: (0, 0, i))],
        out_shape=[jax.ShapeDtypeStruct((PEER_HEADS, keys, t), F32), jax.ShapeDtypeStruct((PEER_HEADS, keys, t), F32),
                   jax.ShapeDtypeStruct((PEER_HEADS, 1, t), F32)],
        compiler_params=_cparams("parallel"),
        name="route",
    )(st)


PEER_E1_BLK = 16
MXU_DEPTH = 256
PEER_LOOKAHEAD = 2


def _peer_kernel(u_ref, vt_ref, xt_ref, s2n_ref, s1n_ref, thr_ref, x1_ref, fnw_ref, y_ref, acc_scr, *, tm):
    k = pl.program_id(1)
    c0 = math.sqrt(2.0 / math.pi)
    c1 = c0 * 0.044715
    per_dot = MXU_DEPTH // PEER_N_KEYS
    n_dots = PEER_E1_BLK // per_dot

    def scores(q):
        return jnp.dot(u_ref[q * MXU_DEPTH:(q + 1) * MXU_DEPTH, :], xt_ref[...], preferred_element_type=F32)

    def gated(q, a):
        wg_rows = []
        for i in range(per_dot):
            r = q * per_dot + i
            rows = slice(i * PEER_N_KEYS, (i + 1) * PEER_N_KEYS)
            wg_cols = []
            for c in range(tm // LANES):
                cols = slice(c * LANES, (c + 1) * LANES)
                w = None
                for h in range(PEER_HEADS):
                    arg = s2n_ref[h, :, cols] + s1n_ref[h, r:r + 1, cols]
                    term = jnp.where(arg >= thr_ref[h, :, cols], jnp.exp2(arg), 0.0)
                    w = term if w is None else w + term
                x = a[rows, cols]
                g = x + x * jnp.tanh(x * (c0 + c1 * (x * x)))
                wg_cols.append((w * g).astype(BF16))
            wg_rows.append(jnp.concatenate(wg_cols, axis=1))
        return jnp.concatenate(wg_rows, axis=0)

    def mixed(q, wg):
        return jnp.dot(vt_ref[0, :, q * MXU_DEPTH:(q + 1) * MXU_DEPTH], wg, preferred_element_type=F32)

    total = None
    ahead = [scores(q) for q in range(min(PEER_LOOKAHEAD, n_dots))]
    wg_prev = None
    for q in range(n_dots):
        a_cur = ahead.pop(0)
        if q + PEER_LOOKAHEAD < n_dots:
            ahead.append(scores(q + PEER_LOOKAHEAD))
        wg = gated(q, a_cur)
        if wg_prev is not None:
            d = mixed(q - 1, wg_prev)
            total = d if total is None else total + d
        wg_prev = wg
    total = total + mixed(n_dots - 1, wg_prev)

    @pl.when(k == 0)
    def _():
        acc_scr[...] = total

    @pl.when(k > 0)
    def _():
        acc_scr[...] += total

    @pl.when(k == pl.num_programs(1) - 1)
    def _():
        y_ref[...] = _rms(x1_ref[...] + acc_scr[...].T, fnw_ref[...])


def _peer(u_bf, vt_bf, xt, route, x1, fnw, tm):
    s1n, s2n, thr = route
    t = x1.shape[0]
    eb = PEER_E1_BLK * PEER_N_KEYS
    per_key = pl.BlockSpec((PEER_HEADS, PEER_N_KEYS, tm), lambda j, k: (0, 0, j))
    per_blk = pl.BlockSpec((PEER_HEADS, PEER_E1_BLK, tm), lambda j, k: (0, k, j))
    return pl.pallas_call(
        functools.partial(_peer_kernel, tm=tm),
        grid=(t // tm, PEER_N_KEYS // PEER_E1_BLK),
        in_specs=[pl.BlockSpec((eb, D_MODEL), lambda j, k: (k, 0)),
                  pl.BlockSpec((1, D_MODEL, eb), lambda j, k: (k, 0, 0)),
                  pl.BlockSpec((D_MODEL, tm), lambda j, k: (0, j)),
                  per_key, per_blk, pl.BlockSpec((PEER_HEADS, 1, tm), lambda j, k: (0, 0, j)),
                  pl.BlockSpec((tm, D_MODEL), lambda j, k: (j, 0)),
                  pl.BlockSpec((1, D_MODEL), lambda j, k: (0, 0))],
        out_specs=pl.BlockSpec((tm, D_MODEL), lambda j, k: (j, 0)),
        out_shape=jax.ShapeDtypeStruct((t, D_MODEL), F32),
        scratch_shapes=[pltpu.VMEM((D_MODEL, tm), F32)],
        compiler_params=_cparams("parallel", "arbitrary"),
        name="peer",
    )(u_bf, vt_bf, xt, s2n, s1n, thr, x1, fnw)


def _tokens_tail(x, post_prm, peer_prm, y5, u, ym):
    x1, xt, st = _post(y5, u, ym, x, post_prm, tm=256)
    u_bf, vt_bf, fnw = peer_prm
    return _peer(u_bf, vt_bf, xt, _route(st, tl=LANES), x1, fnw, tm=512)


def kernel(x_prompt, x_sample, state_s5_re, state_s5_im, state_ssm, state_conv, meta_tokens, norm_mix_w, w_in,
           s5_lambda_re, s5_lambda_im, s5_log_step, s5_b_re, s5_b_im, s5_c_re, s5_c_im, s5_d, s5_w_glu, s5_norm_w,
           m2_conv_w, m2_conv_b, m2_dt_bias, m2_a_log, m2_d, m2_norm_w, w_out, norm_ffn_w, peer_w_q, peer_sub_keys,
           peer_u, peer_v, final_norm_w):
    bp, sp, _ = x_prompt.shape
    bs, ss, _ = x_sample.shape
    g, p = S5_GROUPS, S5_STATE

    w = w_in[0]
    o1, o2, o3 = S5_WIDTH, S5_WIDTH + M2_WIDTH, S5_WIDTH + M2_WIDTH + M2_CONV_DIM
    wu, wz, wx = w[:, :o1].astype(BF16), w[:, o1:o2].astype(BF16), w[:, o2:o3].astype(BF16)
    wd = jnp.pad(w[:, o3:], ((0, 0), (0, LANES - M2_HEADS))).astype(BF16)
    nmw = norm_mix_w[0][None, :]
    pad_h = lambda v: jnp.pad(v, (0, LANES - M2_HEADS))[None, :]
    ssd_prm = (m2_conv_w[0], m2_conv_b[0][None, :], pad_h(m2_dt_bias[0]), pad_h(m2_a_log[0]),
               jnp.repeat(m2_d[0], M2_HEAD_DIM)[None, :], m2_norm_w[0][None, :])
    post_prm = (s5_d[0][None, :], s5_w_glu[0].astype(BF16), s5_norm_w[0][None, :],
                w_out[0][:S5_WIDTH].astype(BF16), w_out[0][S5_WIDTH:].astype(BF16), norm_ffn_w[0][None, :],
                peer_w_q[0].T.astype(BF16),
                peer_sub_keys[0].reshape(2 * PEER_HEADS, PEER_N_KEYS, PEER_HALF).astype(BF16))
    eb = PEER_E1_BLK * PEER_N_KEYS
    vt_blocks = peer_v[0].astype(BF16).reshape(PEER_EXPERTS // eb, eb, D_MODEL).swapaxes(1, 2)
    peer_prm = (peer_u[0].astype(BF16), vt_blocks, final_norm_w[None, :])
    s5_args = (s5_lambda_re[0], s5_lambda_im[0], s5_log_step[0], s5_b_re[0], s5_b_im[0], s5_c_re[0], s5_c_im[0])
    ops16 = _s5_prep(*s5_args, lc=N_META)
    ops8 = _s5_prep(*s5_args, lc=ss)

    lc = N_META
    um, _, xbcm, dtm = _in_proj(meta_tokens, nmw, wu, wz, wx, wd, tm=N_META)
    nb = S5_WIDTH // LANES
    urm = jnp.broadcast_to(um.reshape(lc, nb, LANES).transpose(1, 0, 2).reshape(nb, 1, lc * LANES),
                           (nb, SUBLANES, lc * LANES))
    _, h5m = _s5(urm, jnp.zeros((g, SUBLANES, 2 * p), F32), ops16, lc=lc, rows=SUBLANES, chunks=1)
    _, convm, hm = _ssd(xbcm, jnp.zeros((N_META, M2_WIDTH), F32), dtm,
                        jnp.zeros((1, M2_CONV - 1, M2_CONV_DIM), F32), jnp.zeros((1, M2_D_STATE, M2_WIDTH), F32),
                        ssd_prm, n_seq=1, n_chunks=1, q_in=N_META, shared_init=True)

    xp = x_prompt.reshape(bp * sp, D_MODEL)
    nc = sp // lc
    up, zp, xbcp, dtp = _in_proj(xp, nmw, wu, wz, wx, wd, tm=512)
    urp = up.reshape(bp, nc, lc, nb, LANES).transpose(3, 1, 0, 2, 4).reshape(nb, nc * bp, lc * LANES)
    y5p, h5p = _s5(urp, h5m, ops16, lc=lc, rows=bp, chunks=S5_ROW_TILE // bp)
    y5p = y5p.reshape(nb, nc, bp, lc, LANES).transpose(2, 1, 3, 0, 4).reshape(bp * sp, S5_WIDTH)
    ymp, convp, hp = _ssd(xbcp, zp, dtp, convm, hm, ssd_prm, n_seq=bp, n_chunks=sp // SSD_CHUNK, q_in=SSD_CHUNK,
                          shared_init=True)
    y_prompt = _tokens_tail(xp, post_prm, peer_prm, y5p, up, ymp).reshape(bp, sp, D_MODEL)

    xs = x_sample.reshape(bs * ss, D_MODEL)
    us, zs, xbcs, dts = _in_proj(xs, nmw, wu, wz, wx, wd, tm=512)
    urs = us.reshape(bs, ss, nb, LANES).transpose(2, 0, 1, 3).reshape(nb, bs, ss * LANES)
    h5s0 = jnp.concatenate([state_s5_re[0], state_s5_im[0]], axis=-1).transpose(1, 0, 2)
    y5s, h5s = _s5(urs, h5s0, ops8, lc=ss, rows=bs, chunks=1)
    y5s = y5s.reshape(nb, bs, ss, LANES).transpose(1, 2, 0, 3).reshape(bs * ss, S5_WIDTH)
    hs0 = state_ssm[0].reshape(bs, M2_WIDTH, M2_D_STATE).transpose(0, 2, 1)
    yms, convs, hs = _ssd_packed(xbcs, zs, dts, state_conv[0], hs0, ssd_prm, n_seq=bs, seq_len=ss)
    y_sample = _tokens_tail(xs, post_prm, peer_prm, y5s, us, yms).reshape(bs, ss, D_MODEL)

    def s5_state(hf):
        hf = hf.transpose(1, 0, 2)
        return hf[None, :, :, :p], hf[None, :, :, p:]

    def ssm_state(ht):
        return ht.transpose(0, 2, 1).reshape(1, ht.shape[0], M2_HEADS, M2_HEAD_DIM, M2_D_STATE)

    p5r, p5i = s5_state(h5p)
    s5r, s5i = s5_state(h5s)
    return (y_prompt, y_sample, p5r, p5i, ssm_state(hp), convp[None], s5r, s5i, ssm_state(hs), convs[None])
```

```python
import functools
import math

import jax
import jax.numpy as jnp
from jax import lax
from jax.experimental import pallas as pl
from jax.experimental.pallas import tpu as pltpu

F32 = jnp.float32
BF16 = jnp.bfloat16
HIGHEST = lax.Precision.HIGHEST

D_MODEL = 1024
N_META = 16
S5_WIDTH = 1024
S5_CH = 16
S5_GROUPS = S5_WIDTH // S5_CH
S5_STATE = 64
M2_WIDTH = 1024
M2_HEAD_DIM = 64
M2_HEADS = M2_WIDTH // M2_HEAD_DIM
M2_GROUPS = 2
M2_D_STATE = 128
M2_CONV = 4
M2_CONV_DIM = M2_WIDTH + 2 * M2_GROUPS * M2_D_STATE
PEER_HEADS = 8
PEER_N_KEYS = 128
PEER_EXPERTS = PEER_N_KEYS * PEER_N_KEYS
PEER_HALF = 128
PEER_TOPK = 16
EPS = 1e-6

LANES = 128
SSD_CHUNK = 128
CONV_HALO = 8
VMEM_LIMIT = 56 * 1024 * 1024


def _cparams(*sem):
    return pltpu.CompilerParams(dimension_semantics=sem, vmem_limit_bytes=VMEM_LIMIT)


def _sigmoid(x):
    return 1.0 / (1.0 + jnp.exp(-x))


def _gelu_tanh(x):
    c = math.sqrt(2.0 / math.pi)
    return 0.5 * x * (1.0 + jnp.tanh(c * (x + 0.044715 * (x * x * x))))


def _rms(x, w):
    return x * lax.rsqrt(jnp.mean(x * x, axis=-1, keepdims=True) + EPS) * w


def _bdot(a, b):
    return jnp.dot(a.astype(BF16), b.astype(BF16), preferred_element_type=F32)


def _in_proj_kernel(x_ref, nw_ref, wu_ref, wz_ref, wx_ref, wd_ref, u_ref, z_ref, xbc_ref, dt_ref):
    hb = _rms(x_ref[...], nw_ref[...]).astype(BF16)
    u_ref[...] = jnp.dot(hb, wu_ref[...], preferred_element_type=F32)
    z_ref[...] = jnp.dot(hb, wz_ref[...], preferred_element_type=F32)
    xbc_ref[...] = jnp.dot(hb, wx_ref[...], preferred_element_type=F32)
    dt_ref[...] = jnp.dot(hb, wd_ref[...], preferred_element_type=F32)


def _in_proj(x, nw, wu, wz, wx, wd, tm):
    t = x.shape[0]
    row = lambda i: (i, 0)
    fix = lambda i: (0, 0)
    widths = (S5_WIDTH, M2_WIDTH, M2_CONV_DIM, LANES)
    return pl.pallas_call(
        _in_proj_kernel,
        grid=(t // tm,),
        in_specs=[pl.BlockSpec((tm, D_MODEL), row), pl.BlockSpec((1, D_MODEL), fix)]
        + [pl.BlockSpec((D_MODEL, w), fix) for w in widths],
        out_specs=[pl.BlockSpec((tm, w), row) for w in widths],
        out_shape=[jax.ShapeDtypeStruct((t, w), F32) for w in widths],
        compiler_params=_cparams("parallel"),
        name="in_proj",
    )(x, nw, wu, wz, wx, wd)


def _s5_prep_kernel(lrc_ref, lic_ref, lrr_ref, lir_ref, ls_ref, btr_ref, bti_ref, ctr_ref, cti_ref,
                    kt_ref, wout_ref, wsr_ref, wsi_ref, al_ref, *, lc):
    for j in range(S5_GBLK):
        _s5_prep_group(j, lc, lrc_ref, lic_ref, lrr_ref, lir_ref, ls_ref, btr_ref, bti_ref, ctr_ref, cti_ref,
                       kt_ref, wout_ref, wsr_ref, wsi_ref, al_ref)


def _s5_prep_group(j, lc, lrc_ref, lic_ref, lrr_ref, lir_ref, ls_ref, btr_ref, bti_ref, ctr_ref, cti_ref,
                   kt_ref, wout_ref, wsr_ref, wsi_ref, al_ref):
    k = lc * S5_CH
    step = jnp.exp(ls_ref[j])

    def disc(lr, li):
        lg = lr * step
        th = li * step
        mag = jnp.exp(lg)
        ab_re = mag * jnp.cos(th)
        ab_im = mag * jnp.sin(th)
        den = lr * lr + li * li
        f_re = ((ab_re - 1.0) * lr + ab_im * li) / den
        f_im = (ab_im * lr - (ab_re - 1.0) * li) / den
        return lg, th, f_re, f_im

    lg, th, f_re, f_im = disc(lrc_ref[j], lic_ref[j])
    tau = lax.shift_right_logical(lax.broadcasted_iota(jnp.int32, (S5_STATE, k), 1), 4).astype(F32)
    c_re = ctr_ref[j]
    c_im = cti_ref[j]

    def powers(t, lg_, th_):
        mag = jnp.exp(lg_ * t)
        return mag * jnp.cos(th_ * t), mag * jnp.sin(th_ * t)

    p_re, p_im = powers(tau, lg, th)
    fa_re = f_re * p_re - f_im * p_im
    fa_im = f_re * p_im + f_im * p_re
    fca_re = fa_re * c_re - fa_im * c_im
    fca_im = fa_re * c_im + fa_im * c_re
    kt_ref[j] = (jnp.dot(btr_ref[j], fca_re, precision=HIGHEST, preferred_element_type=F32)
                 - jnp.dot(bti_ref[j], fca_im, precision=HIGHEST, preferred_element_type=F32))
    q_re, q_im = powers(tau + 1.0, lg, th)
    wout_ref[j, 0:S5_STATE, :] = q_re * c_re - q_im * c_im
    wout_ref[j, S5_STATE:2 * S5_STATE, :] = -(q_re * c_im + q_im * c_re)

    lgr, thr, fr_re, fr_im = disc(lrr_ref[j], lir_ref[j])
    kk = lax.broadcasted_iota(jnp.int32, (lc, S5_STATE), 0).astype(F32)
    r_re, r_im = powers(kk, lgr, thr)
    g_re = fr_re * r_re - fr_im * r_im
    g_im = fr_re * r_im + fr_im * r_re
    bt_re = btr_ref[j]
    bt_im = bti_ref[j]
    for s in range(lc):
        kpow = lc - 1 - s
        w_re = g_re[kpow:kpow + 1, :]
        w_im = g_im[kpow:kpow + 1, :]
        wsr_ref[j, s * S5_CH:(s + 1) * S5_CH, :] = bt_re * w_re - bt_im * w_im
        wsi_ref[j, s * S5_CH:(s + 1) * S5_CH, :] = bt_re * w_im + bt_im * w_re
    a_re, a_im = powers(float(lc), lgr, thr)
    al_ref[j, 0:1, :] = a_re
    al_ref[j, 1:2, :] = a_im


def _s5_prep(lam_re, lam_im, log_step, b_re, b_im, c_re, c_im, lc):
    g, p, k = S5_GROUPS, S5_STATE, lc * S5_CH
    spec3 = lambda a, b: pl.BlockSpec((S5_GBLK, a, b), lambda i: (i, 0, 0))
    ins = [lam_re.reshape(g, p, 1), lam_im.reshape(g, p, 1), lam_re.reshape(g, 1, p), lam_im.reshape(g, 1, p),
           log_step.reshape(g, 1, 1),
           jnp.swapaxes(b_re, 1, 2), jnp.swapaxes(b_im, 1, 2),
           jnp.tile(jnp.swapaxes(c_re, 1, 2), (1, 1, lc)), jnp.tile(jnp.swapaxes(c_im, 1, 2), (1, 1, lc))]
    kt, wout, wsr, wsi, al = pl.pallas_call(
        functools.partial(_s5_prep_kernel, lc=lc),
        grid=(g // S5_GBLK,),
        in_specs=[spec3(p, 1), spec3(p, 1), spec3(1, p), spec3(1, p), spec3(1, 1),
                  spec3(S5_CH, p), spec3(S5_CH, p), spec3(p, k), spec3(p, k)],
        out_specs=[spec3(S5_CH, k), spec3(2 * p, k), spec3(k, p), spec3(k, p), spec3(2, p)],
        out_shape=[jax.ShapeDtypeStruct((g, S5_CH, k), F32), jax.ShapeDtypeStruct((g, 2 * p, k), F32),
                   jax.ShapeDtypeStruct((g, k, p), F32), jax.ShapeDtypeStruct((g, k, p), F32),
                   jax.ShapeDtypeStruct((g, 2, p), F32)],
        compiler_params=_cparams("parallel"),
        name="s5_prep",
    )(*ins)
    lag = jnp.arange(lc)[None, :] - jnp.arange(lc)[:, None]
    kt4 = kt.reshape(g, S5_CH, lc, S5_CH)
    toep = kt4[:, :, jnp.maximum(lag, 0), :]
    toep = jnp.where((lag >= 0)[None, None, :, :, None], toep, 0.0)
    toep = toep.transpose(0, 2, 1, 3, 4).reshape(g, k, k)
    wst = jnp.concatenate([wsr, wsi, wsi, wsr], axis=-1)
    a_re, a_im = al[:, 0:1, :], al[:, 1:2, :]
    acoef = jnp.concatenate([jnp.concatenate([a_re, a_re], -1), jnp.concatenate([-a_im, a_im], -1),
                             jnp.concatenate([a_im, -a_im], -1)], axis=1)
    return toep.astype(BF16), wst.astype(BF16), wout.astype(BF16), acoef


S5_GBLK = LANES // S5_CH
S5_ROW_TILE = 256


def _block_transpose(cols):
    n = len(cols)
    blk = lax.shift_right_logical(lax.broadcasted_iota(jnp.int32, (1, LANES), 1), 4)
    rolled = []
    for d in range(n):
        m = cols[d]
        for b in range(1, n):
            m = jnp.where(blk == b, cols[(b + d) % n], m)
        rolled.append(m if d == 0 else pltpu.roll(m, d * S5_CH, axis=1))
    outs = []
    for j in range(n):
        o = rolled[(-j) % n]
        for s in range(1, n):
            o = jnp.where(blk == s, rolled[(s - j) % n], o)
        outs.append(o)
    return outs


def _s5_kernel(u_ref, h0_ref, h0s_ref, wst_ref, toep_ref, wout_ref, a_ref, y_ref, hf_ref,
               s_scr, hin_scr, h_scr, hs_scr, *, lc, rows, chunks):
    i = pl.program_id(1)

    @pl.when(i == 0)
    def _():
        h_scr[...] = h0_ref[...]
        hs_scr[...] = h0s_ref[...]

    x = u_ref[0]
    halves = [_block_transpose([x[:, (S5_GBLK * c + s) * LANES:(S5_GBLK * c + s + 1) * LANES] for s in range(S5_GBLK)])
              for c in range(lc // S5_GBLK)]
    ubs = []
    for j in range(S5_GBLK):
        ubs.append(jnp.concatenate([hv[j] for hv in halves], axis=1).astype(BF16))
        s_scr[j] = jnp.dot(ubs[j], wst_ref[j], preferred_element_type=F32)

    def step(r0, carry):
        new = []
        for j in range(S5_GBLK):
            h, hs = carry[2 * j], carry[2 * j + 1]
            hin_scr[j, pl.ds(r0, rows), :] = h
            s = s_scr[j, pl.ds(r0, rows), :]
            a1, a2, a2s = a_ref[j, 0:1, :], a_ref[j, 1:2, :], a_ref[j, 2:3, :]
            new.append(a1 * h + a2 * hs + s[:, :LANES])
            new.append(a1 * hs + a2s * h + s[:, LANES:])
        return tuple(new)

    carry = tuple(ref[j] for j in range(S5_GBLK) for ref in (h_scr, hs_scr))
    if chunks == 1:
        carry = step(0, carry)
    else:
        carry = lax.fori_loop(0, chunks, lambda n, c: step(pl.multiple_of(n * rows, rows), c), carry)
    for j in range(S5_GBLK):
        h_scr[j] = carry[2 * j]
        hs_scr[j] = carry[2 * j + 1]
    hf_ref[...] = h_scr[...]

    ys = [jnp.dot(ubs[j], toep_ref[j], preferred_element_type=F32)
          + jnp.dot(hin_scr[j].astype(BF16), wout_ref[j], preferred_element_type=F32) for j in range(S5_GBLK)]
    for c in range(lc // S5_GBLK):
        token_major = _block_transpose([ys[j][:, c * LANES:(c + 1) * LANES] for j in range(S5_GBLK)])
        for t in range(S5_GBLK):
            y_ref[0, :, (S5_GBLK * c + t) * LANES:(S5_GBLK * c + t + 1) * LANES] = token_major[t]


def _s5(ub, h0, ops, lc, rows, chunks):
    toep, wst, wout, acoef = ops
    nb, nr, width = ub.shape
    k = lc * S5_CH
    rt = rows * chunks
    blk = lambda a, b: pl.BlockSpec((S5_GBLK, a, b), lambda gb, i: (gb, 0, 0))
    return pl.pallas_call(
        functools.partial(_s5_kernel, lc=lc, rows=rows, chunks=chunks),
        grid=(nb, nr // rt),
        in_specs=[pl.BlockSpec((1, rt, width), lambda gb, i: (gb, i, 0)),
                  blk(rows, LANES), blk(rows, LANES), blk(k, 2 * LANES), blk(k, k), blk(LANES, k), blk(3, LANES)],
        out_specs=[pl.BlockSpec((1, rt, width), lambda gb, i: (gb, i, 0)), blk(rows, LANES)],
        out_shape=[jax.ShapeDtypeStruct((nb, nr, width), F32), jax.ShapeDtypeStruct((S5_GROUPS, rows, LANES), F32)],
        scratch_shapes=[pltpu.VMEM((S5_GBLK, rt, 2 * LANES), F32), pltpu.VMEM((S5_GBLK, rt, LANES), F32),
                        pltpu.VMEM((S5_GBLK, rows, LANES), F32), pltpu.VMEM((S5_GBLK, rows, LANES), F32)],
        compiler_params=_cparams("parallel", "arbitrary"),
        name="s5_scan",
    )(ub, h0, jnp.roll(h0, S5_STATE, axis=-1), wst, toep, wout, acoef)


def _ssd_kernel(xbc_ref, z_ref, dt_ref, conv0_ref, h0_ref, cw_ref, cb_ref, dtb_ref, alog_ref, dexp_ref, nw_ref,
                y_ref, convn_ref, hn_ref, buf_scr, dt_scr, h_scr, *, q_in, n_chunks):
    q = SSD_CHUNK
    pad = q - q_in
    first = CONV_HALO + pad
    c = pl.program_id(1)

    @pl.when(c == 0)
    def _():
        buf_scr[0:first, :] = jnp.zeros((first, M2_CONV_DIM), F32)
        buf_scr[first - (M2_CONV - 1):first, :] = conv0_ref[0]
        h_scr[...] = h0_ref[0]

    buf_scr[first:CONV_HALO + q, :] = xbc_ref[...]
    conv = cb_ref[...]
    for kk in range(M2_CONV):
        lo = CONV_HALO - (M2_CONV - 1) + kk
        conv = conv + cw_ref[kk:kk + 1, :] * buf_scr[lo:lo + q, :]
    convn_ref[0] = buf_scr[CONV_HALO + q - (M2_CONV - 1):CONV_HALO + q, :]
    if n_chunks > 1:
        buf_scr[0:CONV_HALO, :] = buf_scr[q:q + CONV_HALO, :]
    act = conv * _sigmoid(conv)
    xs = act[:, :M2_WIDTH]

    dtv = dt_ref[...] + dtb_ref[...]
    dt_real = jnp.maximum(dtv, 0.0) + jnp.log1p(jnp.exp(-jnp.abs(dtv)))
    if pad:
        dt_scr[0:pad, :] = jnp.zeros((pad, LANES), F32)
        dt_scr[pad:q, :] = dt_real
        dt = dt_scr[...]
    else:
        dt = dt_real
    a_neg = -jnp.exp(alog_ref[...])
    ri = lax.broadcasted_iota(jnp.int32, (q, q), 0)
    ci = lax.broadcasted_iota(jnp.int32, (q, q), 1)
    tril = ri >= ci
    acs = jnp.dot(tril.astype(F32), dt * a_neg, precision=HIGHEST, preferred_element_type=F32)
    acs_t = acs.T
    dt_t = dt.T
    acs_last = acs[q - 1:q, :]
    eacs = jnp.exp(acs)
    wdec = jnp.exp(acs_last - acs) * dt
    dec_last = jnp.exp(acs_last)
    low = lax.broadcasted_iota(jnp.int32, (q, LANES), 1) < M2_HEAD_DIM
    low1 = low[0:1, :]

    def pair_cols(m, h0):
        return jnp.where(low[0:m.shape[0], :], m[:, h0:h0 + 1], m[:, h0 + 1:h0 + 2])

    y_parts = []
    hpg = M2_HEADS // M2_GROUPS
    gw = hpg * M2_HEAD_DIM
    for g in range(M2_GROUPS):
        bg = act[:, M2_WIDTH + g * M2_D_STATE:M2_WIDTH + (g + 1) * M2_D_STATE]
        cg = act[:, M2_WIDTH + (M2_GROUPS + g) * M2_D_STATE:M2_WIDTH + (M2_GROUPS + g + 1) * M2_D_STATE]
        cgb = cg.astype(BF16)
        cb = lax.dot_general(cgb, bg.astype(BF16), (((1,), (1,)), ((), ())), preferred_element_type=F32)
        hg = h_scr[:, g * gw:(g + 1) * gw]
        yoff = jnp.dot(cgb, hg.astype(BF16), preferred_element_type=F32)
        xw_parts = []
        dec_parts = []
        for pp in range(hpg // 2):
            h0 = g * hpg + 2 * pp
            lanes = slice(h0 * M2_HEAD_DIM, (h0 + 2) * M2_HEAD_DIM)
            w_pair = []
            for h in (h0, h0 + 1):
                seg = acs[:, h:h + 1] - acs_t[h:h + 1, :]
                dec = jnp.exp(jnp.where(tril, seg, -jnp.inf))
                w_pair.append((cb * dec * dt_t[h:h + 1, :]).astype(BF16))
            xp = xs[:, lanes]
            xbd = jnp.concatenate([jnp.where(low, xp, 0.0), jnp.where(low, 0.0, xp)], axis=0).astype(BF16)
            yd = jnp.dot(jnp.concatenate(w_pair, axis=1), xbd, preferred_element_type=F32)
            yo = yoff[:, 2 * pp * M2_HEAD_DIM:(2 * pp + 2) * M2_HEAD_DIM]
            y_parts.append(yd + yo * pair_cols(eacs, h0) + dexp_ref[:, lanes] * xp)
            xw_parts.append(xp * pair_cols(wdec, h0))
            dec_parts.append(jnp.where(low1, dec_last[:, h0:h0 + 1], dec_last[:, h0 + 1:h0 + 2]))
        xw = jnp.concatenate(xw_parts, axis=1).astype(BF16)
        st = jnp.dot(bg.T.astype(BF16), xw, preferred_element_type=F32)
        h_scr[:, g * gw:(g + 1) * gw] = hg * jnp.concatenate(dec_parts, axis=1) + st

    @pl.when(c == n_chunks - 1)
    def _():
        hn_ref[0] = h_scr[...]

    y = jnp.concatenate(y_parts, axis=1)[pad:, :]
    zz = z_ref[...]
    y = y * (zz * _sigmoid(zz))
    outs = []
    for g in range(M2_GROUPS):
        yg = y[:, g * gw:(g + 1) * gw]
        outs.append(yg * lax.rsqrt(jnp.mean(yg * yg, axis=-1, keepdims=True) + EPS))
    y_ref[...] = jnp.concatenate(outs, axis=1) * nw_ref[...]


def _ssd(xbc, z, dt, conv0, h0, prm, n_seq, n_chunks, q_in, shared_init):
    assert q_in == SSD_CHUNK or n_chunks == 1
    assert q_in >= CONV_HALO and q_in % CONV_HALO == 0
    t = xbc.shape[0]
    row = lambda b, c: (b * n_chunks + c, 0)
    fix = lambda b, c: (0, 0)
    init = (lambda b, c: (0, 0, 0)) if shared_init else (lambda b, c: (b, 0, 0))
    per_seq = lambda b, c: (b, 0, 0)
    return pl.pallas_call(
        functools.partial(_ssd_kernel, q_in=q_in, n_chunks=n_chunks),
        grid=(n_seq, n_chunks),
        in_specs=[pl.BlockSpec((q_in, M2_CONV_DIM), row), pl.BlockSpec((q_in, M2_WIDTH), row),
                  pl.BlockSpec((q_in, LANES), row),
                  pl.BlockSpec((1, M2_CONV - 1, M2_CONV_DIM), init),
                  pl.BlockSpec((1, M2_D_STATE, M2_WIDTH), init),
                  pl.BlockSpec((M2_CONV, M2_CONV_DIM), fix), pl.BlockSpec((1, M2_CONV_DIM), fix),
                  pl.BlockSpec((1, LANES), fix), pl.BlockSpec((1, LANES), fix),
                  pl.BlockSpec((1, M2_WIDTH), fix), pl.BlockSpec((1, M2_WIDTH), fix)],
        out_specs=[pl.BlockSpec((q_in, M2_WIDTH), row),
                   pl.BlockSpec((1, M2_CONV - 1, M2_CONV_DIM), per_seq),
                   pl.BlockSpec((1, M2_D_STATE, M2_WIDTH), per_seq)],
        out_shape=[jax.ShapeDtypeStruct((t, M2_WIDTH), F32),
                   jax.ShapeDtypeStruct((n_seq, M2_CONV - 1, M2_CONV_DIM), F32),
                   jax.ShapeDtypeStruct((n_seq, M2_D_STATE, M2_WIDTH), F32)],
        scratch_shapes=[pltpu.VMEM((CONV_HALO + SSD_CHUNK, M2_CONV_DIM), F32),
                        pltpu.VMEM((SSD_CHUNK, LANES), F32),
                        pltpu.VMEM((M2_D_STATE, M2_WIDTH), F32)],
        compiler_params=_cparams("parallel", "arbitrary"),
        name="ssd",
    )(xbc, z, dt, conv0, h0, *prm)


def _ssd_packed_kernel(xbc_ref, hist_ref, z_ref, dt_ref, h0_ref, cw_ref, cb_ref, dtb_ref, alog_ref, dexp_ref, nw_ref,
                       y_ref, hn_ref, buf_scr, *, seq_len):
    q = SSD_CHUNK
    n = M2_D_STATE
    ns = q // seq_len
    shift = seq_len.bit_length() - 1
    hpg = M2_HEADS // M2_GROUPS
    gw = hpg * M2_HEAD_DIM
    cdim = gw + 2 * n
    x = xbc_ref[0]
    buf_scr[0:CONV_HALO, :] = jnp.zeros((CONV_HALO, cdim), F32)
    buf_scr[CONV_HALO:CONV_HALO + q, :] = x
    pos = lax.broadcasted_iota(jnp.int32, (q, cdim), 0) & (seq_len - 1)
    conv = cb_ref[0] + cw_ref[0, M2_CONV - 1:M2_CONV, :] * x
    for k in range(1, M2_CONV):
        prev = jnp.where(pos >= k, buf_scr[CONV_HALO - k:CONV_HALO - k + q, :], hist_ref[k - 1, 0])
        conv = conv + cw_ref[0, M2_CONV - 1 - k:M2_CONV - k, :] * prev
    act = conv * _sigmoid(conv)
    xs = act[:, :gw]
    bg = act[:, gw:gw + n]
    cg = act[:, gw + n:]

    dtv = dt_ref[0] + dtb_ref[0]
    dt = jnp.maximum(dtv, 0.0) + jnp.log1p(jnp.exp(-jnp.abs(dtv)))
    a_neg = -jnp.exp(alog_ref[0])
    ri = lax.broadcasted_iota(jnp.int32, (q, q), 0)
    ci = lax.broadcasted_iota(jnp.int32, (q, q), 1)
    same = lax.shift_right_logical(ri, shift) == lax.shift_right_logical(ci, shift)
    causal = (ri >= ci) & same
    dta = dt * a_neg
    acs = jnp.dot(causal.astype(F32), dta, precision=HIGHEST, preferred_element_type=F32)
    tot = jnp.dot(same.astype(F32), dta, precision=HIGHEST, preferred_element_type=F32)
    acs_t = acs.T
    dt_t = dt.T
    eacs = jnp.exp(acs)
    wdec = jnp.exp(tot - acs) * dt
    dec_tot = jnp.exp(tot)
    low = lax.broadcasted_iota(jnp.int32, (q, LANES), 1) < M2_HEAD_DIM

    def pair_cols(m, h0):
        return jnp.where(low, m[:, h0:h0 + 1], m[:, h0 + 1:h0 + 2])

    cgb = cg.astype(BF16)
    cb = lax.dot_general(cgb, bg.astype(BF16), (((1,), (1,)), ((), ())), preferred_element_type=F32)
    row_seq = lax.shift_right_logical(lax.broadcasted_iota(jnp.int32, (q, n), 0), shift)
    c_blocks = jnp.concatenate([jnp.where(row_seq == s, cg, 0.0) for s in range(ns)], axis=1).astype(BF16)
    h_all = h0_ref[...].reshape(ns * n, gw)
    yoff = jnp.dot(c_blocks, h_all.astype(BF16), preferred_element_type=F32)
    y_parts, xw_parts, dec_parts = [], [], []
    for pp in range(hpg // 2):
        h0 = 2 * pp
        lanes = slice(h0 * M2_HEAD_DIM, (h0 + 2) * M2_HEAD_DIM)
        w_pair = []
        for h in (h0, h0 + 1):
            seg = acs[:, h:h + 1] - acs_t[h:h + 1, :]
            dec = jnp.exp(jnp.where(causal, seg, -jnp.inf))
            w_pair.append((cb * dec * dt_t[h:h + 1, :]).astype(BF16))
        xp = xs[:, lanes]
        xbd = jnp.concatenate([jnp.where(low, xp, 0.0), jnp.where(low, 0.0, xp)], axis=0).astype(BF16)
        yd = jnp.dot(jnp.concatenate(w_pair, axis=1), xbd, preferred_element_type=F32)
        y_parts.append(yd + yoff[:, lanes] * pair_cols(eacs, h0) + dexp_ref[0, :, lanes] * xp)
        xw_parts.append(xp * pair_cols(wdec, h0))
        dec_parts.append(pair_cols(dec_tot, h0))
    xw = jnp.concatenate(xw_parts, axis=1).astype(BF16)
    col_seq = lax.shift_right_logical(lax.broadcasted_iota(jnp.int32, (n, q), 1), shift)
    bgt = bg.T
    b_blocks = jnp.concatenate([jnp.where(col_seq == s, bgt, 0.0) for s in range(ns)], axis=0).astype(BF16)
    st = jnp.dot(b_blocks, xw, preferred_element_type=F32)
    dec_rows = jnp.concatenate(dec_parts, axis=1)
    for s in range(ns):
        hn_ref[s] = h0_ref[s] * dec_rows[s * seq_len:s * seq_len + 1, :] + st[s * n:(s + 1) * n, :]

    zz = z_ref[...]
    y = jnp.concatenate(y_parts, axis=1) * (zz * _sigmoid(zz))
    y_ref[...] = y * lax.rsqrt(jnp.mean(y * y, axis=-1, keepdims=True) + EPS) * nw_ref[0]


def _by_group(a):
    gw = M2_WIDTH // M2_GROUPS
    n = M2_D_STATE
    return jnp.stack([jnp.concatenate([a[..., g * gw:(g + 1) * gw],
                                       a[..., M2_WIDTH + g * n:M2_WIDTH + (g + 1) * n],
                                       a[..., M2_WIDTH + (M2_GROUPS + g) * n:M2_WIDTH + (M2_GROUPS + g + 1) * n]],
                                      axis=-1) for g in range(M2_GROUPS)])


def _ssd_packed(xbc, z, dt, conv0, h0, prm, n_seq, seq_len):
    cw, cb, dtb, alog, dexp, nw = prm
    t = xbc.shape[0]
    hpg = M2_HEADS // M2_GROUPS
    gw = hpg * M2_HEAD_DIM
    cdim = gw + 2 * M2_D_STATE
    heads = lambda a: jnp.stack([jnp.pad(a[..., g * hpg:(g + 1) * hpg], [(0, 0)] * (a.ndim - 1) + [(0, LANES - hpg)])
                                 for g in range(M2_GROUPS)])
    halves = lambda a: jnp.stack([a[..., g * gw:(g + 1) * gw] for g in range(M2_GROUPS)])
    hist = jnp.stack([_by_group(jnp.pad(conv0[:, M2_CONV - 1 - k:, :], ((0, 0), (0, seq_len - k), (0, 0)))
                                .reshape(t, M2_CONV_DIM)) for k in range(1, M2_CONV)])
    per_g = lambda r, c: pl.BlockSpec((1, r, c), lambda i, g: (g, 0, 0))
    rows_g = lambda c: pl.BlockSpec((1, SSD_CHUNK, c), lambda i, g: (g, i, 0))
    ns = SSD_CHUNK // seq_len
    state = pl.BlockSpec((ns, M2_D_STATE, gw), lambda i, g: (i, 0, g))
    y, hn = pl.pallas_call(
        functools.partial(_ssd_packed_kernel, seq_len=seq_len),
        grid=(t // SSD_CHUNK, M2_GROUPS),
        in_specs=[rows_g(cdim),
                  pl.BlockSpec((M2_CONV - 1, 1, SSD_CHUNK, cdim), lambda i, g: (0, g, i, 0)),
                  pl.BlockSpec((SSD_CHUNK, gw), lambda i, g: (i, g)),
                  rows_g(LANES), state,
                  per_g(M2_CONV, cdim), per_g(1, cdim), per_g(1, LANES), per_g(1, LANES), per_g(1, gw), per_g(1, gw)],
        out_specs=[pl.BlockSpec((SSD_CHUNK, gw), lambda i, g: (i, g)), state],
        out_shape=[jax.ShapeDtypeStruct((t, M2_WIDTH), F32), jax.ShapeDtypeStruct((n_seq, M2_D_STATE, M2_WIDTH), F32)],
        scratch_shapes=[pltpu.VMEM((CONV_HALO + SSD_CHUNK, cdim), F32)],
        compiler_params=_cparams("parallel", "parallel"),
        name="ssd_packed",
    )(_by_group(xbc), hist, z, heads(dt), h0, _by_group(cw), _by_group(cb), heads(dtb), heads(alog),
      halves(dexp), halves(nw))
    conv_new = xbc.reshape(n_seq, seq_len, M2_CONV_DIM)[:, seq_len - (M2_CONV - 1):, :]
    return y, conv_new, hn


def _post_kernel(y5_ref, u_ref, ym_ref, x_ref, d_ref, wglu_ref, s5nw_ref, wo1_ref, wo2_ref, fnw_ref, wqt_ref,
                 sk_ref, x1_ref, xt_ref, st_ref):
    g = _gelu_tanh(y5_ref[...] + d_ref[...] * u_ref[...])
    o = g * _sigmoid(jnp.dot(g.astype(BF16), wglu_ref[...], preferred_element_type=F32))
    y5n = _rms(o, s5nw_ref[...])
    x1 = (x_ref[...] + jnp.dot(y5n.astype(BF16), wo1_ref[...], preferred_element_type=F32)
          + jnp.dot(ym_ref[...].astype(BF16), wo2_ref[...], preferred_element_type=F32))
    x1_ref[...] = x1
    hn_t = _rms(x1, fnw_ref[...]).T.astype(BF16)
    xt_ref[...] = hn_t
    q_t = jnp.dot(wqt_ref[...], hn_t, preferred_element_type=F32)
    for k in range(2 * PEER_HEADS):
        qk = q_t[k * PEER_HALF:(k + 1) * PEER_HALF, :].astype(BF16)
        st_ref[k] = jnp.dot(sk_ref[k], qk, preferred_element_type=F32) * LOG2E


def _post(y5, u, ym, x, prm, tm):
    t = x.shape[0]
    row = lambda i: (i, 0)
    fix = lambda i: (0, 0)
    nk = 2 * PEER_HEADS
    once = functools.partial(pl.BlockSpec, pipeline_mode=pl.Buffered(1))
    return pl.pallas_call(
        _post_kernel,
        grid=(t // tm,),
        in_specs=[pl.BlockSpec((tm, D_MODEL), row)] * 4
        + [pl.BlockSpec((1, D_MODEL), fix), once((S5_WIDTH, S5_WIDTH), fix), pl.BlockSpec((1, D_MODEL), fix),
           once((S5_WIDTH, D_MODEL), fix), once((M2_WIDTH, D_MODEL), fix),
           pl.BlockSpec((1, D_MODEL), fix), once((nk * PEER_HALF, D_MODEL), fix),
           once((nk, PEER_N_KEYS, PEER_HALF), lambda i: (0, 0, 0))],
        out_specs=[pl.BlockSpec((tm, D_MODEL), row), pl.BlockSpec((D_MODEL, tm), lambda i: (0, i)),
                   pl.BlockSpec((nk, PEER_N_KEYS, tm), lambda i: (0, 0, i))],
        out_shape=[jax.ShapeDtypeStruct((t, D_MODEL), F32), jax.ShapeDtypeStruct((D_MODEL, t), BF16),
                   jax.ShapeDtypeStruct((nk, PEER_N_KEYS, t), F32)],
        compiler_params=_cparams("parallel"),
        name="post",
    )(y5, u, ym, x, *prm)


def _sorting_network(n):
    pairs = []
    p = 1
    while p < n:
        k = p
        while k >= 1:
            for j in range(k % p, n - k, 2 * k):
                for i in range(min(k, n - j - k)):
                    if (i + j) // (2 * p) == (i + j + k) // (2 * p):
                        pairs.append((i + j, i + j + k))
            k //= 2
        p *= 2
    return pairs


SUBLANES = 8
_NET16 = _sorting_network(PEER_N_KEYS // SUBLANES)
LOG2E = math.log2(math.e)


def _top16(tiles):
    n = len(tiles)
    tiles = list(tiles)

    def exchange(i, j):
        tiles[i], tiles[j] = jnp.maximum(tiles[i], tiles[j]), jnp.minimum(tiles[i], tiles[j])

    for i, j in _NET16:
        exchange(i, j)
    shift = SUBLANES // 2
    while shift >= 1:
        other = [pltpu.roll(t, shift, axis=0) for t in tiles]
        tiles = [jnp.maximum(tiles[i], other[n - 1 - i]) for i in range(n)]
        dist = n // 2
        while dist >= 1:
            for i in range(n):
                if i & dist == 0:
                    exchange(i, i + dist)
            dist //= 2
        shift //= 2
    return tiles


def _top_values(s):
    n = PEER_N_KEYS // SUBLANES
    tiles = _top16([s[i * SUBLANES:(i + 1) * SUBLANES, :] for i in range(n)])
    rows = lax.broadcasted_iota(jnp.int32, (PEER_TOPK, s.shape[1]), 0)
    out = jnp.zeros((PEER_TOPK, s.shape[1]), F32)
    for r in range(PEER_TOPK):
        out = jnp.where(rows == r, jnp.concatenate([tiles[r], tiles[r]], axis=0), out)
    return out


def _pair_candidates(v1, v2):
    r8 = lax.broadcasted_iota(jnp.int32, (SUBLANES, v1.shape[1]), 0)
    r16 = lax.broadcasted_iota(jnp.int32, (PEER_TOPK, v1.shape[1]), 0)
    neg = -jnp.inf
    lo2 = v2[0:SUBLANES, :]
    return jnp.concatenate([
        v1[0:1, :] + v2,
        v1[1:2, :] + lo2,
        jnp.where(r16 >= 2, v1 + v2[0:1, :], neg),
        jnp.where(r8 >= 2, v1[0:SUBLANES, :] + v2[1:2, :], neg),
        jnp.where((r8 >= 2) & (r8 <= 4), v1[2:3, :] + lo2, neg),
        jnp.where((r8 >= 2) & (r8 <= 3), v1[3:4, :] + lo2, neg),
        jnp.where(r8 == 2, v1[4:5, :] + lo2, neg),
    ], axis=0)


def _route_kernel(st_ref, s1n_ref, s2n_ref, thr_ref):
    def head(h, carry):
        s1 = st_ref[2 * h]
        s2 = st_ref[2 * h + 1]
        v1 = _top_values(s1)
        v2 = _top_values(s2)
        cand = _pair_candidates(v1, v2)
        n_cand = cand.shape[0] // SUBLANES
        pad = [jnp.full((SUBLANES, cand.shape[1]), -jnp.inf, F32)] * (PEER_N_KEYS // SUBLANES - n_cand)
        ranked = _top16([cand[i * SUBLANES:(i + 1) * SUBLANES, :] for i in range(n_cand)] + pad)
        theta = ranked[PEER_TOPK - 1][0:1, :]
        sel = cand >= theta
        m = v1[0:1, :] + v2[0:1, :]
        zsum = jnp.sum(jnp.where(sel, jnp.exp2(cand - m), 0.0), axis=0, keepdims=True)
        off = m + jnp.log(zsum) * LOG2E + 1.0
        s1n_ref[h] = s1 - off
        s2n_ref[h] = s2
        candn = _pair_candidates(v1 - off, v2)
        thr_ref[h] = jnp.min(jnp.where(sel, candn, jnp.inf), axis=0, keepdims=True)
        return carry

    lax.fori_loop(0, PEER_HEADS, head, 0)


def _route(st, tl):
    nk, keys, t = st.shape
    spec = lambda n: pl.BlockSpec((n, keys, tl), lambda i: (0, 0, i))
    return pl.pallas_call(
        _route_kernel,
        grid=(t // tl,),
        in_specs=[spec(nk)],
        out_specs=[spec(PEER_HEADS), spec(PEER_HEADS), pl.BlockSpec((PEER_HEADS, 1, tl), lambda i: (0, 0, i))],
        out_shape=[jax.ShapeDtypeStruct((PEER_HEADS, keys, t), F32), jax.ShapeDtypeStruct((PEER_HEADS, keys, t), F32),
                   jax.ShapeDtypeStruct((PEER_HEADS, 1, t), F32)],
        compiler_params=_cparams("parallel"),
        name="route",
    )(st)


PEER_E1_BLK = 16
PEER_CHUNK_KEYS = (4, 4, 4, 4)
assert sum(PEER_CHUNK_KEYS) == PEER_E1_BLK


def _peer_kernel(u_ref, vt_ref, xt_ref, s2n_ref, s1n_ref, thr_ref, x1_ref, fnw_ref, y_ref, acc_scr, *, tm):
    k = pl.program_id(1)
    c0 = math.sqrt(2.0 / math.pi)
    c1 = c0 * 0.044715
    first_key = [sum(PEER_CHUNK_KEYS[:q]) for q in range(len(PEER_CHUNK_KEYS))]

    def experts(q):
        return slice(first_key[q] * PEER_N_KEYS, (first_key[q] + PEER_CHUNK_KEYS[q]) * PEER_N_KEYS)

    def scores(q):
        return jnp.dot(u_ref[experts(q), :], xt_ref[...], preferred_element_type=F32)

    def gated(q, a):
        wg_rows = []
        for i in range(PEER_CHUNK_KEYS[q]):
            r = first_key[q] + i
            rows = slice(i * PEER_N_KEYS, (i + 1) * PEER_N_KEYS)
            wg_cols = []
            for c in range(tm // LANES):
                cols = slice(c * LANES, (c + 1) * LANES)
                w = None
                for h in range(PEER_HEADS):
                    arg = s2n_ref[h, :, cols] + s1n_ref[h, r:r + 1, cols]
                    term = jnp.where(arg >= thr_ref[h, :, cols], jnp.exp2(arg), 0.0)
                    w = term if w is None else w + term
                x = a[rows, cols]
                g = x + x * jnp.tanh(x * (c0 + c1 * (x * x)))
                wg_cols.append((w * g).astype(BF16))
            wg_rows.append(jnp.concatenate(wg_cols, axis=1))
        return jnp.concatenate(wg_rows, axis=0)

    def mixed(q, wg):
        return jnp.dot(vt_ref[0, :, experts(q)], wg, preferred_element_type=F32)

    n_chunks = len(PEER_CHUNK_KEYS)
    total = None
    a_next = scores(0)
    wg_prev = None
    for q in range(n_chunks):
        a_cur = a_next
        if q + 1 < n_chunks:
            a_next = scores(q + 1)
        wg = gated(q, a_cur)
        if wg_prev is not None:
            d = mixed(q - 1, wg_prev)
            total = d if total is None else total + d
        wg_prev = wg
    total = total + mixed(n_chunks - 1, wg_prev)

    @pl.when(k == 0)
    def _():
        acc_scr[...] = total

    @pl.when(k > 0)
    def _():
        acc_scr[...] += total

    @pl.when(k == pl.num_programs(1) - 1)
    def _():
        y_ref[...] = _rms(x1_ref[...] + acc_scr[...].T, fnw_ref[...])


def _peer(u_bf, vt_bf, xt, route, x1, fnw, tm):
    s1n, s2n, thr = route
    t = x1.shape[0]
    eb = PEER_E1_BLK * PEER_N_KEYS
    per_key = pl.BlockSpec((PEER_HEADS, PEER_N_KEYS, tm), lambda j, k: (0, 0, j))
    per_blk = pl.BlockSpec((PEER_HEADS, PEER_E1_BLK, tm), lambda j, k: (0, k, j))
    return pl.pallas_call(
        functools.partial(_peer_kernel, tm=tm),
        grid=(t // tm, PEER_N_KEYS // PEER_E1_BLK),
        in_specs=[pl.BlockSpec((eb, D_MODEL), lambda j, k: (k, 0)),
                  pl.BlockSpec((1, D_MODEL, eb), lambda j, k: (k, 0, 0)),
                  pl.BlockSpec((D_MODEL, tm), lambda j, k: (0, j)),
                  per_key, per_blk, pl.BlockSpec((PEER_HEADS, 1, tm), lambda j, k: (0, 0, j)),
                  pl.BlockSpec((tm, D_MODEL), lambda j, k: (j, 0)),
                  pl.BlockSpec((1, D_MODEL), lambda j, k: (0, 0))],
        out_specs=pl.BlockSpec((tm, D_MODEL), lambda j, k: (j, 0)),
        out_shape=jax.ShapeDtypeStruct((t, D_MODEL), F32),
        scratch_shapes=[pltpu.VMEM((D_MODEL, tm), F32)],
        compiler_params=_cparams("parallel", "arbitrary"),
        name="peer",
    )(u_bf, vt_bf, xt, s2n, s1n, thr, x1, fnw)


def _tokens_tail(x, post_prm, peer_prm, y5, u, ym):
    x1, xt, st = _post(y5, u, ym, x, post_prm, tm=512)
    u_bf, vt_bf, fnw = peer_prm
    return _peer(u_bf, vt_bf, xt, _route(st, tl=LANES), x1, fnw, tm=512)


def kernel(x_prompt, x_sample, state_s5_re, state_s5_im, state_ssm, state_conv, meta_tokens, norm_mix_w, w_in,
           s5_lambda_re, s5_lambda_im, s5_log_step, s5_b_re, s5_b_im, s5_c_re, s5_c_im, s5_d, s5_w_glu, s5_norm_w,
           m2_conv_w, m2_conv_b, m2_dt_bias, m2_a_log, m2_d, m2_norm_w, w_out, norm_ffn_w, peer_w_q, peer_sub_keys,
           peer_u, peer_v, final_norm_w):
    bp, sp, _ = x_prompt.shape
    bs, ss, _ = x_sample.shape
    g, p = S5_GROUPS, S5_STATE

    w = w_in[0]
    o1, o2, o3 = S5_WIDTH, S5_WIDTH + M2_WIDTH, S5_WIDTH + M2_WIDTH + M2_CONV_DIM
    wu, wz, wx = w[:, :o1].astype(BF16), w[:, o1:o2].astype(BF16), w[:, o2:o3].astype(BF16)
    wd = jnp.pad(w[:, o3:], ((0, 0), (0, LANES - M2_HEADS))).astype(BF16)
    nmw = norm_mix_w[0][None, :]
    pad_h = lambda v: jnp.pad(v, (0, LANES - M2_HEADS))[None, :]
    ssd_prm = (m2_conv_w[0], m2_conv_b[0][None, :], pad_h(m2_dt_bias[0]), pad_h(m2_a_log[0]),
               jnp.repeat(m2_d[0], M2_HEAD_DIM)[None, :], m2_norm_w[0][None, :])
    post_prm = (s5_d[0][None, :], s5_w_glu[0].astype(BF16), s5_norm_w[0][None, :],
                w_out[0][:S5_WIDTH].astype(BF16), w_out[0][S5_WIDTH:].astype(BF16), norm_ffn_w[0][None, :],
                peer_w_q[0].T.astype(BF16),
                peer_sub_keys[0].reshape(2 * PEER_HEADS, PEER_N_KEYS, PEER_HALF).astype(BF16))
    eb = PEER_E1_BLK * PEER_N_KEYS
    vt_blocks = peer_v[0].astype(BF16).reshape(PEER_EXPERTS // eb, eb, D_MODEL).swapaxes(1, 2)
    peer_prm = (peer_u[0].astype(BF16), vt_blocks, final_norm_w[None, :])
    s5_args = (s5_lambda_re[0], s5_lambda_im[0], s5_log_step[0], s5_b_re[0], s5_b_im[0], s5_c_re[0], s5_c_im[0])
    ops16 = _s5_prep(*s5_args, lc=N_META)
    ops8 = _s5_prep(*s5_args, lc=ss)

    lc = N_META
    um, _, xbcm, dtm = _in_proj(meta_tokens, nmw, wu, wz, wx, wd, tm=N_META)
    nb = S5_WIDTH // LANES
    urm = jnp.broadcast_to(um.reshape(lc, nb, LANES).transpose(1, 0, 2).reshape(nb, 1, lc * LANES),
                           (nb, SUBLANES, lc * LANES))
    _, h5m = _s5(urm, jnp.zeros((g, SUBLANES, 2 * p), F32), ops16, lc=lc, rows=SUBLANES, chunks=1)
    _, convm, hm = _ssd(xbcm, jnp.zeros((N_META, M2_WIDTH), F32), dtm,
                        jnp.zeros((1, M2_CONV - 1, M2_CONV_DIM), F32), jnp.zeros((1, M2_D_STATE, M2_WIDTH), F32),
                        ssd_prm, n_seq=1, n_chunks=1, q_in=N_META, shared_init=True)

    xp = x_prompt.reshape(bp * sp, D_MODEL)
    nc = sp // lc
    up, zp, xbcp, dtp = _in_proj(xp, nmw, wu, wz, wx, wd, tm=512)
    urp = up.reshape(bp, nc, lc, nb, LANES).transpose(3, 1, 0, 2, 4).reshape(nb, nc * bp, lc * LANES)
    y5p, h5p = _s5(urp, h5m, ops16, lc=lc, rows=bp, chunks=S5_ROW_TILE // bp)
    y5p = y5p.reshape(nb, nc, bp, lc, LANES).transpose(2, 1, 3, 0, 4).reshape(bp * sp, S5_WIDTH)
    ymp, convp, hp = _ssd(xbcp, zp, dtp, convm, hm, ssd_prm, n_seq=bp, n_chunks=sp // SSD_CHUNK, q_in=SSD_CHUNK,
                          shared_init=True)
    y_prompt = _tokens_tail(xp, post_prm, peer_prm, y5p, up, ymp).reshape(bp, sp, D_MODEL)

    xs = x_sample.reshape(bs * ss, D_MODEL)
    us, zs, xbcs, dts = _in_proj(xs, nmw, wu, wz, wx, wd, tm=512)
    urs = us.reshape(bs, ss, nb, LANES).transpose(2, 0, 1, 3).reshape(nb, bs, ss * LANES)
    h5s0 = jnp.concatenate([state_s5_re[0], state_s5_im[0]], axis=-1).transpose(1, 0, 2)
    y5s, h5s = _s5(urs, h5s0, ops8, lc=ss, rows=bs, chunks=1)
    y5s = y5s.reshape(nb, bs, ss, LANES).transpose(1, 2, 0, 3).reshape(bs * ss, S5_WIDTH)
    hs0 = state_ssm[0].reshape(bs, M2_WIDTH, M2_D_STATE).transpose(0, 2, 1)
    yms, convs, hs = _ssd_packed(xbcs, zs, dts, state_conv[0], hs0, ssd_prm, n_seq=bs, seq_len=ss)
    y_sample = _tokens_tail(xs, post_prm, peer_prm, y5s, us, yms).reshape(bs, ss, D_MODEL)

    def s5_state(hf):
        hf = hf.transpose(1, 0, 2)
        return hf[None, :, :, :p], hf[None, :, :, p:]

    def ssm_state(ht):
        return ht.transpose(0, 2, 1).reshape(1, ht.shape[0], M2_HEADS, M2_HEAD_DIM, M2_D_STATE)

    p5r, p5i = s5_state(h5p)
    s5r, s5i = s5_state(h5s)
    return (y_prompt, y_sample, p5r, p5i, ssm_state(hp), convp[None], s5r, s5i, ssm_state(hs), convs[None])
```

```python
import functools
import math

import jax
import jax.numpy as jnp
from jax import lax
from jax.experimental import pallas as pl
from jax.experimental.pallas import tpu as pltpu

F32 = jnp.float32
BF16 = jnp.bfloat16
HIGHEST = lax.Precision.HIGHEST

D_MODEL = 1024
N_META = 16
S5_WIDTH = 1024
S5_CH = 16
S5_GROUPS = S5_WIDTH // S5_CH
S5_STATE = 64
M2_WIDTH = 1024
M2_HEAD_DIM = 64
M2_HEADS = M2_WIDTH // M2_HEAD_DIM
M2_GROUPS = 2
M2_D_STATE = 128
M2_CONV = 4
M2_CONV_DIM = M2_WIDTH + 2 * M2_GROUPS * M2_D_STATE
PEER_HEADS = 8
PEER_N_KEYS = 128
PEER_EXPERTS = PEER_N_KEYS * PEER_N_KEYS
PEER_HALF = 128
PEER_TOPK = 16
EPS = 1e-6

LANES = 128
SSD_CHUNK = 128
CONV_HALO = 8
VMEM_LIMIT = 56 * 1024 * 1024


def _cparams(*sem):
    return pltpu.CompilerParams(dimension_semantics=sem, vmem_limit_bytes=VMEM_LIMIT)


def _sigmoid(x):
    return 1.0 / (1.0 + jnp.exp(-x))


def _gelu_tanh(x):
    c = math.sqrt(2.0 / math.pi)
    return 0.5 * x * (1.0 + jnp.tanh(c * (x + 0.044715 * (x * x * x))))


def _rms(x, w):
    return x * lax.rsqrt(jnp.mean(x * x, axis=-1, keepdims=True) + EPS) * w


def _bdot(a, b):
    return jnp.dot(a.astype(BF16), b.astype(BF16), preferred_element_type=F32)


def _store_blocked(ref, val):
    if len(ref.shape) == 2:
        ref[...] = val
        return
    for gb in range(ref.shape[0]):
        piece = val[:, gb * LANES:(gb + 1) * LANES]
        if len(ref.shape) == 5:
            ref[gb, :, 0] = piece.reshape(ref.shape[1], ref.shape[3], LANES)
        else:
            ref[gb] = piece.reshape(ref.shape[1:])


def _load_blocked(ref):
    if len(ref.shape) == 2:
        return ref[...]
    parts = []
    for gb in range(ref.shape[0]):
        piece = ref[gb, :, 0] if len(ref.shape) == 5 else ref[gb]
        parts.append(piece.reshape(piece.shape[0] * piece.shape[1], LANES))
    return jnp.concatenate(parts, axis=1)


def _in_proj_kernel(x_ref, nw_ref, wu_ref, wz_ref, wx_ref, wd_ref, u_ref, z_ref, xbc_ref, dt_ref):
    hb = _rms(x_ref[...], nw_ref[...]).astype(BF16)
    _store_blocked(u_ref, jnp.dot(hb, wu_ref[...], preferred_element_type=F32))
    z_ref[...] = jnp.dot(hb, wz_ref[...], preferred_element_type=F32)
    xbc_ref[...] = jnp.dot(hb, wx_ref[...], preferred_element_type=F32)
    dt_ref[...] = jnp.dot(hb, wd_ref[...], preferred_element_type=F32)


def _in_proj(x, nw, wu, wz, wx, wd, tm, u_layout=None):
    t = x.shape[0]
    row = lambda i: (i, 0)
    fix = lambda i: (0, 0)
    widths = (S5_WIDTH, M2_WIDTH, M2_CONV_DIM, LANES)
    out_specs = [pl.BlockSpec((tm, w), row) for w in widths]
    out_shape = [jax.ShapeDtypeStruct((t, w), F32) for w in widths]
    if u_layout is not None:
        out_specs[0] = pl.BlockSpec(u_layout[1], u_layout[2])
        out_shape[0] = jax.ShapeDtypeStruct(u_layout[0], F32)
    return pl.pallas_call(
        _in_proj_kernel,
        grid=(t // tm,),
        in_specs=[pl.BlockSpec((tm, D_MODEL), row), pl.BlockSpec((1, D_MODEL), fix)]
        + [pl.BlockSpec((D_MODEL, w), fix) for w in widths],
        out_specs=out_specs,
        out_shape=out_shape,
        compiler_params=_cparams("parallel"),
        name="in_proj",
    )(x, nw, wu, wz, wx, wd)


def _s5_prep_kernel(lrc_ref, lic_ref, lrr_ref, lir_ref, ls_ref, btr_ref, bti_ref, ctr_ref, cti_ref,
                    kt_ref, wout_ref, wsr_ref, wsi_ref, al_ref, *, lc):
    for j in range(S5_GBLK):
        _s5_prep_group(j, lc, lrc_ref, lic_ref, lrr_ref, lir_ref, ls_ref, btr_ref, bti_ref, ctr_ref, cti_ref,
                       kt_ref, wout_ref, wsr_ref, wsi_ref, al_ref)


def _s5_prep_group(j, lc, lrc_ref, lic_ref, lrr_ref, lir_ref, ls_ref, btr_ref, bti_ref, ctr_ref, cti_ref,
                   kt_ref, wout_ref, wsr_ref, wsi_ref, al_ref):
    k = lc * S5_CH
    step = jnp.exp(ls_ref[j])

    def disc(lr, li):
        lg = lr * step
        th = li * step
        mag = jnp.exp(lg)
        ab_re = mag * jnp.cos(th)
        ab_im = mag * jnp.sin(th)
        den = lr * lr + li * li
        f_re = ((ab_re - 1.0) * lr + ab_im * li) / den
        f_im = (ab_im * lr - (ab_re - 1.0) * li) / den
        return lg, th, f_re, f_im

    lg, th, f_re, f_im = disc(lrc_ref[j], lic_ref[j])
    tau = lax.shift_right_logical(lax.broadcasted_iota(jnp.int32, (S5_STATE, k), 1), 4).astype(F32)
    c_re = ctr_ref[j]
    c_im = cti_ref[j]

    def powers(t, lg_, th_):
        mag = jnp.exp(lg_ * t)
        return mag * jnp.cos(th_ * t), mag * jnp.sin(th_ * t)

    p_re, p_im = powers(tau, lg, th)
    fa_re = f_re * p_re - f_im * p_im
    fa_im = f_re * p_im + f_im * p_re
    fca_re = fa_re * c_re - fa_im * c_im
    fca_im = fa_re * c_im + fa_im * c_re
    kt_ref[j] = (jnp.dot(btr_ref[j], fca_re, precision=HIGHEST, preferred_element_type=F32)
                 - jnp.dot(bti_ref[j], fca_im, precision=HIGHEST, preferred_element_type=F32))
    q_re, q_im = powers(tau + 1.0, lg, th)
    wout_ref[j, 0:S5_STATE, :] = q_re * c_re - q_im * c_im
    wout_ref[j, S5_STATE:2 * S5_STATE, :] = -(q_re * c_im + q_im * c_re)

    lgr, thr, fr_re, fr_im = disc(lrr_ref[j], lir_ref[j])
    kk = lax.broadcasted_iota(jnp.int32, (lc, S5_STATE), 0).astype(F32)
    r_re, r_im = powers(kk, lgr, thr)
    g_re = fr_re * r_re - fr_im * r_im
    g_im = fr_re * r_im + fr_im * r_re
    bt_re = btr_ref[j]
    bt_im = bti_ref[j]
    for s in range(lc):
        kpow = lc - 1 - s
        w_re = g_re[kpow:kpow + 1, :]
        w_im = g_im[kpow:kpow + 1, :]
        wsr_ref[j, s * S5_CH:(s + 1) * S5_CH, :] = bt_re * w_re - bt_im * w_im
        wsi_ref[j, s * S5_CH:(s + 1) * S5_CH, :] = bt_re * w_im + bt_im * w_re
    a_re, a_im = powers(float(lc), lgr, thr)
    al_ref[j, 0:1, :] = a_re
    al_ref[j, 1:2, :] = a_im


def _s5_prep(lam_re, lam_im, log_step, b_re, b_im, c_re, c_im, lc):
    g, p, k = S5_GROUPS, S5_STATE, lc * S5_CH
    spec3 = lambda a, b: pl.BlockSpec((S5_GBLK, a, b), lambda i: (i, 0, 0))
    ins = [lam_re.reshape(g, p, 1), lam_im.reshape(g, p, 1), lam_re.reshape(g, 1, p), lam_im.reshape(g, 1, p),
           log_step.reshape(g, 1, 1),
           jnp.swapaxes(b_re, 1, 2), jnp.swapaxes(b_im, 1, 2),
           jnp.tile(jnp.swapaxes(c_re, 1, 2), (1, 1, lc)), jnp.tile(jnp.swapaxes(c_im, 1, 2), (1, 1, lc))]
    kt, wout, wsr, wsi, al = pl.pallas_call(
        functools.partial(_s5_prep_kernel, lc=lc),
        grid=(g // S5_GBLK,),
        in_specs=[spec3(p, 1), spec3(p, 1), spec3(1, p), spec3(1, p), spec3(1, 1),
                  spec3(S5_CH, p), spec3(S5_CH, p), spec3(p, k), spec3(p, k)],
        out_specs=[spec3(S5_CH, k), spec3(2 * p, k), spec3(k, p), spec3(k, p), spec3(2, p)],
        out_shape=[jax.ShapeDtypeStruct((g, S5_CH, k), F32), jax.ShapeDtypeStruct((g, 2 * p, k), F32),
                   jax.ShapeDtypeStruct((g, k, p), F32), jax.ShapeDtypeStruct((g, k, p), F32),
                   jax.ShapeDtypeStruct((g, 2, p), F32)],
        compiler_params=_cparams("parallel"),
        name="s5_prep",
    )(*ins)
    lag = jnp.arange(lc)[None, :] - jnp.arange(lc)[:, None]
    kt4 = kt.reshape(g, S5_CH, lc, S5_CH)
    toep = kt4[:, :, jnp.maximum(lag, 0), :]
    toep = jnp.where((lag >= 0)[None, None, :, :, None], toep, 0.0)
    toep = toep.transpose(0, 2, 1, 3, 4).reshape(g, k, k)
    wst = jnp.concatenate([wsr, wsi, wsi, wsr], axis=-1)
    a_re, a_im = al[:, 0:1, :], al[:, 1:2, :]
    acoef = jnp.concatenate([jnp.concatenate([a_re, a_re], -1), jnp.concatenate([-a_im, a_im], -1),
                             jnp.concatenate([a_im, -a_im], -1)], axis=1)
    return toep.astype(BF16), wst.astype(BF16), wout.astype(BF16), acoef


S5_GBLK = LANES // S5_CH
S5_ROW_TILE = 256


def _block_transpose(cols):
    n = len(cols)
    blk = lax.shift_right_logical(lax.broadcasted_iota(jnp.int32, (1, LANES), 1), 4)
    rolled = []
    for d in range(n):
        m = cols[d]
        for b in range(1, n):
            m = jnp.where(blk == b, cols[(b + d) % n], m)
        rolled.append(m if d == 0 else pltpu.roll(m, d * S5_CH, axis=1))
    outs = []
    for j in range(n):
        o = rolled[(-j) % n]
        for s in range(1, n):
            o = jnp.where(blk == s, rolled[(s - j) % n], o)
        outs.append(o)
    return outs


def _s5_kernel(u_ref, h0_ref, h0s_ref, wst_ref, toep_ref, wout_ref, a_ref, y_ref, hf_ref,
               s_scr, hin_scr, h_scr, hs_scr, *, lc, rows, chunks):
    i = pl.program_id(1)

    @pl.when(i == 0)
    def _():
        h_scr[...] = h0_ref[...]
        hs_scr[...] = h0s_ref[...]

    x = u_ref[0]
    halves = [_block_transpose([x[:, (S5_GBLK * c + s) * LANES:(S5_GBLK * c + s + 1) * LANES] for s in range(S5_GBLK)])
              for c in range(lc // S5_GBLK)]
    ubs = []
    for j in range(S5_GBLK):
        ubs.append(jnp.concatenate([hv[j] for hv in halves], axis=1).astype(BF16))
        s_scr[j] = jnp.dot(ubs[j], wst_ref[j], preferred_element_type=F32)

    def step(r0, carry):
        new = []
        for j in range(S5_GBLK):
            h, hs = carry[2 * j], carry[2 * j + 1]
            hin_scr[j, pl.ds(r0, rows), :] = h
            s = s_scr[j, pl.ds(r0, rows), :]
            a1, a2, a2s = a_ref[j, 0:1, :], a_ref[j, 1:2, :], a_ref[j, 2:3, :]
            new.append(a1 * h + a2 * hs + s[:, :LANES])
            new.append(a1 * hs + a2s * h + s[:, LANES:])
        return tuple(new)

    carry = tuple(ref[j] for j in range(S5_GBLK) for ref in (h_scr, hs_scr))
    if chunks == 1:
        carry = step(0, carry)
    else:
        carry = lax.fori_loop(0, chunks, lambda n, c: step(pl.multiple_of(n * rows, rows), c), carry)
    for j in range(S5_GBLK):
        h_scr[j] = carry[2 * j]
        hs_scr[j] = carry[2 * j + 1]
    hf_ref[...] = h_scr[...]

    ys = [jnp.dot(ubs[j], toep_ref[j], preferred_element_type=F32)
          + jnp.dot(hin_scr[j].astype(BF16), wout_ref[j], preferred_element_type=F32) for j in range(S5_GBLK)]
    for c in range(lc // S5_GBLK):
        token_major = _block_transpose([ys[j][:, c * LANES:(c + 1) * LANES] for j in range(S5_GBLK)])
        for t in range(S5_GBLK):
            y_ref[0, :, (S5_GBLK * c + t) * LANES:(S5_GBLK * c + t + 1) * LANES] = token_major[t]


def _s5(ub, h0, ops, lc, rows, chunks):
    toep, wst, wout, acoef = ops
    nb, nr, width = ub.shape
    k = lc * S5_CH
    rt = rows * chunks
    blk = lambda a, b: pl.BlockSpec((S5_GBLK, a, b), lambda gb, i: (gb, 0, 0))
    return pl.pallas_call(
        functools.partial(_s5_kernel, lc=lc, rows=rows, chunks=chunks),
        grid=(nb, nr // rt),
        in_specs=[pl.BlockSpec((1, rt, width), lambda gb, i: (gb, i, 0)),
                  blk(rows, LANES), blk(rows, LANES), blk(k, 2 * LANES), blk(k, k), blk(LANES, k), blk(3, LANES)],
        out_specs=[pl.BlockSpec((1, rt, width), lambda gb, i: (gb, i, 0)), blk(rows, LANES)],
        out_shape=[jax.ShapeDtypeStruct((nb, nr, width), F32), jax.ShapeDtypeStruct((S5_GROUPS, rows, LANES), F32)],
        scratch_shapes=[pltpu.VMEM((S5_GBLK, rt, 2 * LANES), F32), pltpu.VMEM((S5_GBLK, rt, LANES), F32),
                        pltpu.VMEM((S5_GBLK, rows, LANES), F32), pltpu.VMEM((S5_GBLK, rows, LANES), F32)],
        compiler_params=_cparams("parallel", "arbitrary"),
        name="s5_scan",
    )(ub, h0, jnp.roll(h0, S5_STATE, axis=-1), wst, toep, wout, acoef)


def _ssd_kernel(xbc_ref, z_ref, dt_ref, conv0_ref, h0_ref, cw_ref, cb_ref, dtb_ref, alog_ref, dexp_ref, nw_ref,
                y_ref, convn_ref, hn_ref, buf_scr, dt_scr, h_scr, *, q_in, n_chunks):
    q = SSD_CHUNK
    pad = q - q_in
    first = CONV_HALO + pad
    c = pl.program_id(1)

    @pl.when(c == 0)
    def _():
        buf_scr[0:first, :] = jnp.zeros((first, M2_CONV_DIM), F32)
        buf_scr[first - (M2_CONV - 1):first, :] = conv0_ref[0]
        h_scr[...] = h0_ref[0]

    buf_scr[first:CONV_HALO + q, :] = xbc_ref[...]
    conv = cb_ref[...]
    for kk in range(M2_CONV):
        lo = CONV_HALO - (M2_CONV - 1) + kk
        conv = conv + cw_ref[kk:kk + 1, :] * buf_scr[lo:lo + q, :]
    convn_ref[0] = buf_scr[CONV_HALO + q - (M2_CONV - 1):CONV_HALO + q, :]
    if n_chunks > 1:
        buf_scr[0:CONV_HALO, :] = buf_scr[q:q + CONV_HALO, :]
    act = conv * _sigmoid(conv)
    xs = act[:, :M2_WIDTH]

    dtv = dt_ref[...] + dtb_ref[...]
    dt_real = jnp.maximum(dtv, 0.0) + jnp.log1p(jnp.exp(-jnp.abs(dtv)))
    if pad:
        dt_scr[0:pad, :] = jnp.zeros((pad, LANES), F32)
        dt_scr[pad:q, :] = dt_real
        dt = dt_scr[...]
    else:
        dt = dt_real
    a_neg = -jnp.exp(alog_ref[...])
    ri = lax.broadcasted_iota(jnp.int32, (q, q), 0)
    ci = lax.broadcasted_iota(jnp.int32, (q, q), 1)
    tril = ri >= ci
    acs = jnp.dot(tril.astype(F32), dt * a_neg, precision=HIGHEST, preferred_element_type=F32)
    acs_t = acs.T
    dt_t = dt.T
    acs_last = acs[q - 1:q, :]
    eacs = jnp.exp(acs)
    wdec = jnp.exp(acs_last - acs) * dt
    dec_last = jnp.exp(acs_last)
    low = lax.broadcasted_iota(jnp.int32, (q, LANES), 1) < M2_HEAD_DIM
    low1 = low[0:1, :]

    def pair_cols(m, h0):
        return jnp.where(low[0:m.shape[0], :], m[:, h0:h0 + 1], m[:, h0 + 1:h0 + 2])

    y_parts = []
    hpg = M2_HEADS // M2_GROUPS
    gw = hpg * M2_HEAD_DIM
    for g in range(M2_GROUPS):
        bg = act[:, M2_WIDTH + g * M2_D_STATE:M2_WIDTH + (g + 1) * M2_D_STATE]
        cg = act[:, M2_WIDTH + (M2_GROUPS + g) * M2_D_STATE:M2_WIDTH + (M2_GROUPS + g + 1) * M2_D_STATE]
        cgb = cg.astype(BF16)
        cb = lax.dot_general(cgb, bg.astype(BF16), (((1,), (1,)), ((), ())), preferred_element_type=F32)
        hg = h_scr[:, g * gw:(g + 1) * gw]
        yoff = jnp.dot(cgb, hg.astype(BF16), preferred_element_type=F32)
        xw_parts = []
        dec_parts = []
        for pp in range(hpg // 2):
            h0 = g * hpg + 2 * pp
            lanes = slice(h0 * M2_HEAD_DIM, (h0 + 2) * M2_HEAD_DIM)
            w_pair = []
            for h in (h0, h0 + 1):
                seg = acs[:, h:h + 1] - acs_t[h:h + 1, :]
                dec = jnp.exp(jnp.where(tril, seg, -jnp.inf))
                w_pair.append((cb * dec * dt_t[h:h + 1, :]).astype(BF16))
            xp = xs[:, lanes]
            xbd = jnp.concatenate([jnp.where(low, xp, 0.0), jnp.where(low, 0.0, xp)], axis=0).astype(BF16)
            yd = jnp.dot(jnp.concatenate(w_pair, axis=1), xbd, preferred_element_type=F32)
            yo = yoff[:, 2 * pp * M2_HEAD_DIM:(2 * pp + 2) * M2_HEAD_DIM]
            y_parts.append(yd + yo * pair_cols(eacs, h0) + dexp_ref[:, lanes] * xp)
            xw_parts.append(xp * pair_cols(wdec, h0))
            dec_parts.append(jnp.where(low1, dec_last[:, h0:h0 + 1], dec_last[:, h0 + 1:h0 + 2]))
        xw = jnp.concatenate(xw_parts, axis=1).astype(BF16)
        st = jnp.dot(bg.T.astype(BF16), xw, preferred_element_type=F32)
        h_scr[:, g * gw:(g + 1) * gw] = hg * jnp.concatenate(dec_parts, axis=1) + st

    @pl.when(c == n_chunks - 1)
    def _():
        hn_ref[0] = h_scr[...]

    y = jnp.concatenate(y_parts, axis=1)[pad:, :]
    zz = z_ref[...]
    y = y * (zz * _sigmoid(zz))
    outs = []
    for g in range(M2_GROUPS):
        yg = y[:, g * gw:(g + 1) * gw]
        outs.append(yg * lax.rsqrt(jnp.mean(yg * yg, axis=-1, keepdims=True) + EPS))
    y_ref[...] = jnp.concatenate(outs, axis=1) * nw_ref[...]


def _ssd(xbc, z, dt, conv0, h0, prm, n_seq, n_chunks, q_in, shared_init):
    assert q_in == SSD_CHUNK or n_chunks == 1
    assert q_in >= CONV_HALO and q_in % CONV_HALO == 0
    t = xbc.shape[0]
    row = lambda b, c: (b * n_chunks + c, 0)
    fix = lambda b, c: (0, 0)
    init = (lambda b, c: (0, 0, 0)) if shared_init else (lambda b, c: (b, 0, 0))
    per_seq = lambda b, c: (b, 0, 0)
    return pl.pallas_call(
        functools.partial(_ssd_kernel, q_in=q_in, n_chunks=n_chunks),
        grid=(n_seq, n_chunks),
        in_specs=[pl.BlockSpec((q_in, M2_CONV_DIM), row), pl.BlockSpec((q_in, M2_WIDTH), row),
                  pl.BlockSpec((q_in, LANES), row),
                  pl.BlockSpec((1, M2_CONV - 1, M2_CONV_DIM), init),
                  pl.BlockSpec((1, M2_D_STATE, M2_WIDTH), init),
                  pl.BlockSpec((M2_CONV, M2_CONV_DIM), fix), pl.BlockSpec((1, M2_CONV_DIM), fix),
                  pl.BlockSpec((1, LANES), fix), pl.BlockSpec((1, LANES), fix),
                  pl.BlockSpec((1, M2_WIDTH), fix), pl.BlockSpec((1, M2_WIDTH), fix)],
        out_specs=[pl.BlockSpec((q_in, M2_WIDTH), row),
                   pl.BlockSpec((1, M2_CONV - 1, M2_CONV_DIM), per_seq),
                   pl.BlockSpec((1, M2_D_STATE, M2_WIDTH), per_seq)],
        out_shape=[jax.ShapeDtypeStruct((t, M2_WIDTH), F32),
                   jax.ShapeDtypeStruct((n_seq, M2_CONV - 1, M2_CONV_DIM), F32),
                   jax.ShapeDtypeStruct((n_seq, M2_D_STATE, M2_WIDTH), F32)],
        scratch_shapes=[pltpu.VMEM((CONV_HALO + SSD_CHUNK, M2_CONV_DIM), F32),
                        pltpu.VMEM((SSD_CHUNK, LANES), F32),
                        pltpu.VMEM((M2_D_STATE, M2_WIDTH), F32)],
        compiler_params=_cparams("parallel", "arbitrary"),
        name="ssd",
    )(xbc, z, dt, conv0, h0, *prm)


def _ssd_packed_kernel(xbc_ref, hist_ref, z_ref, dt_ref, h0_ref, cw_ref, cb_ref, dtb_ref, alog_ref, dexp_ref, nw_ref,
                       y_ref, hn_ref, buf_scr, *, seq_len):
    q = SSD_CHUNK
    n = M2_D_STATE
    ns = q // seq_len
    shift = seq_len.bit_length() - 1
    hpg = M2_HEADS // M2_GROUPS
    gw = hpg * M2_HEAD_DIM
    cdim = gw + 2 * n
    x = xbc_ref[0]
    buf_scr[0:CONV_HALO, :] = jnp.zeros((CONV_HALO, cdim), F32)
    buf_scr[CONV_HALO:CONV_HALO + q, :] = x
    pos = lax.broadcasted_iota(jnp.int32, (q, cdim), 0) & (seq_len - 1)
    conv = cb_ref[0] + cw_ref[0, M2_CONV - 1:M2_CONV, :] * x
    for k in range(1, M2_CONV):
        prev = jnp.where(pos >= k, buf_scr[CONV_HALO - k:CONV_HALO - k + q, :], hist_ref[k - 1, 0])
        conv = conv + cw_ref[0, M2_CONV - 1 - k:M2_CONV - k, :] * prev
    act = conv * _sigmoid(conv)
    xs = act[:, :gw]
    bg = act[:, gw:gw + n]
    cg = act[:, gw + n:]

    dtv = dt_ref[0] + dtb_ref[0]
    dt = jnp.maximum(dtv, 0.0) + jnp.log1p(jnp.exp(-jnp.abs(dtv)))
    a_neg = -jnp.exp(alog_ref[0])
    ri = lax.broadcasted_iota(jnp.int32, (q, q), 0)
    ci = lax.broadcasted_iota(jnp.int32, (q, q), 1)
    same = lax.shift_right_logical(ri, shift) == lax.shift_right_logical(ci, shift)
    causal = (ri >= ci) & same
    dta = dt * a_neg
    acs = jnp.dot(causal.astype(F32), dta, precision=HIGHEST, preferred_element_type=F32)
    tot = jnp.dot(same.astype(F32), dta, precision=HIGHEST, preferred_element_type=F32)
    acs_t = acs.T
    dt_t = dt.T
    eacs = jnp.exp(acs)
    wdec = jnp.exp(tot - acs) * dt
    dec_tot = jnp.exp(tot)
    low = lax.broadcasted_iota(jnp.int32, (q, LANES), 1) < M2_HEAD_DIM

    def pair_cols(m, h0):
        return jnp.where(low, m[:, h0:h0 + 1], m[:, h0 + 1:h0 + 2])

    cgb = cg.astype(BF16)
    cb = lax.dot_general(cgb, bg.astype(BF16), (((1,), (1,)), ((), ())), preferred_element_type=F32)
    row_seq = lax.shift_right_logical(lax.broadcasted_iota(jnp.int32, (q, n), 0), shift)
    c_blocks = jnp.concatenate([jnp.where(row_seq == s, cg, 0.0) for s in range(ns)], axis=1).astype(BF16)
    h_all = h0_ref[...].reshape(ns * n, gw)
    yoff = jnp.dot(c_blocks, h_all.astype(BF16), preferred_element_type=F32)
    y_parts, xw_parts, dec_parts = [], [], []
    for pp in range(hpg // 2):
        h0 = 2 * pp
        lanes = slice(h0 * M2_HEAD_DIM, (h0 + 2) * M2_HEAD_DIM)
        w_pair = []
        for h in (h0, h0 + 1):
            seg = acs[:, h:h + 1] - acs_t[h:h + 1, :]
            dec = jnp.exp(jnp.where(causal, seg, -jnp.inf))
            w_pair.append((cb * dec * dt_t[h:h + 1, :]).astype(BF16))
        xp = xs[:, lanes]
        xbd = jnp.concatenate([jnp.where(low, xp, 0.0), jnp.where(low, 0.0, xp)], axis=0).astype(BF16)
        yd = jnp.dot(jnp.concatenate(w_pair, axis=1), xbd, preferred_element_type=F32)
        y_parts.append(yd + yoff[:, lanes] * pair_cols(eacs, h0) + dexp_ref[0, :, lanes] * xp)
        xw_parts.append(xp * pair_cols(wdec, h0))
        dec_parts.append(pair_cols(dec_tot, h0))
    xw = jnp.concatenate(xw_parts, axis=1).astype(BF16)
    col_seq = lax.shift_right_logical(lax.broadcasted_iota(jnp.int32, (n, q), 1), shift)
    bgt = bg.T
    b_blocks = jnp.concatenate([jnp.where(col_seq == s, bgt, 0.0) for s in range(ns)], axis=0).astype(BF16)
    st = jnp.dot(b_blocks, xw, preferred_element_type=F32)
    dec_rows = jnp.concatenate(dec_parts, axis=1)
    for s in range(ns):
        hn_ref[s] = h0_ref[s] * dec_rows[s * seq_len:s * seq_len + 1, :] + st[s * n:(s + 1) * n, :]

    zz = z_ref[...]
    y = jnp.concatenate(y_parts, axis=1) * (zz * _sigmoid(zz))
    y_ref[...] = y * lax.rsqrt(jnp.mean(y * y, axis=-1, keepdims=True) + EPS) * nw_ref[0]


def _by_group(a):
    gw = M2_WIDTH // M2_GROUPS
    n = M2_D_STATE
    return jnp.stack([jnp.concatenate([a[..., g * gw:(g + 1) * gw],
                                       a[..., M2_WIDTH + g * n:M2_WIDTH + (g + 1) * n],
                                       a[..., M2_WIDTH + (M2_GROUPS + g) * n:M2_WIDTH + (M2_GROUPS + g + 1) * n]],
                                      axis=-1) for g in range(M2_GROUPS)])


def _ssd_packed(xbc, z, dt, conv0, h0, prm, n_seq, seq_len):
    cw, cb, dtb, alog, dexp, nw = prm
    t = xbc.shape[0]
    hpg = M2_HEADS // M2_GROUPS
    gw = hpg * M2_HEAD_DIM
    cdim = gw + 2 * M2_D_STATE
    heads = lambda a: jnp.stack([jnp.pad(a[..., g * hpg:(g + 1) * hpg], [(0, 0)] * (a.ndim - 1) + [(0, LANES - hpg)])
                                 for g in range(M2_GROUPS)])
    halves = lambda a: jnp.stack([a[..., g * gw:(g + 1) * gw] for g in range(M2_GROUPS)])
    hist = jnp.stack([_by_group(jnp.pad(conv0[:, M2_CONV - 1 - k:, :], ((0, 0), (0, seq_len - k), (0, 0)))
                                .reshape(t, M2_CONV_DIM)) for k in range(1, M2_CONV)])
    per_g = lambda r, c: pl.BlockSpec((1, r, c), lambda i, g: (g, 0, 0))
    rows_g = lambda c: pl.BlockSpec((1, SSD_CHUNK, c), lambda i, g: (g, i, 0))
    ns = SSD_CHUNK // seq_len
    state = pl.BlockSpec((ns, M2_D_STATE, gw), lambda i, g: (i, 0, g))
    y, hn = pl.pallas_call(
        functools.partial(_ssd_packed_kernel, seq_len=seq_len),
        grid=(t // SSD_CHUNK, M2_GROUPS),
        in_specs=[rows_g(cdim),
                  pl.BlockSpec((M2_CONV - 1, 1, SSD_CHUNK, cdim), lambda i, g: (0, g, i, 0)),
                  pl.BlockSpec((SSD_CHUNK, gw), lambda i, g: (i, g)),
                  rows_g(LANES), state,
                  per_g(M2_CONV, cdim), per_g(1, cdim), per_g(1, LANES), per_g(1, LANES), per_g(1, gw), per_g(1, gw)],
        out_specs=[pl.BlockSpec((SSD_CHUNK, gw), lambda i, g: (i, g)), state],
        out_shape=[jax.ShapeDtypeStruct((t, M2_WIDTH), F32), jax.ShapeDtypeStruct((n_seq, M2_D_STATE, M2_WIDTH), F32)],
        scratch_shapes=[pltpu.VMEM((CONV_HALO + SSD_CHUNK, cdim), F32)],
        compiler_params=_cparams("parallel", "parallel"),
        name="ssd_packed",
    )(_by_group(xbc), hist, z, heads(dt), h0, _by_group(cw), _by_group(cb), heads(dtb), heads(alog),
      halves(dexp), halves(nw))
    conv_new = xbc.reshape(n_seq, seq_len, M2_CONV_DIM)[:, seq_len - (M2_CONV - 1):, :]
    return y, conv_new, hn


def _post_kernel(y5_ref, u_ref, ym_ref, x_ref, d_ref, wglu_ref, s5nw_ref, wo1_ref, wo2_ref, fnw_ref, wqt_ref,
                 sk_ref, x1_ref, xt_ref, st_ref):
    g = _gelu_tanh(_load_blocked(y5_ref) + d_ref[...] * _load_blocked(u_ref))
    o = g * _sigmoid(jnp.dot(g.astype(BF16), wglu_ref[...], preferred_element_type=F32))
    y5n = _rms(o, s5nw_ref[...])
    x1 = (x_ref[...] + jnp.dot(y5n.astype(BF16), wo1_ref[...], preferred_element_type=F32)
          + jnp.dot(ym_ref[...].astype(BF16), wo2_ref[...], preferred_element_type=F32))
    x1_ref[...] = x1
    hn_t = _rms(x1, fnw_ref[...]).T.astype(BF16)
    xt_ref[...] = hn_t
    q_t = jnp.dot(wqt_ref[...], hn_t, preferred_element_type=F32)
    for k in range(2 * PEER_HEADS):
        qk = q_t[k * PEER_HALF:(k + 1) * PEER_HALF, :].astype(BF16)
        st_ref[k] = jnp.dot(sk_ref[k], qk, preferred_element_type=F32) * LOG2E


def _post(y5, u, ym, x, prm, tm, s5_block=None):
    t = x.shape[0]
    row = lambda i: (i, 0)
    s5_spec = pl.BlockSpec((tm, D_MODEL), row) if s5_block is None else pl.BlockSpec(*s5_block)
    fix = lambda i: (0, 0)
    nk = 2 * PEER_HEADS
    once = functools.partial(pl.BlockSpec, pipeline_mode=pl.Buffered(1))
    return pl.pallas_call(
        _post_kernel,
        grid=(t // tm,),
        in_specs=[s5_spec, s5_spec, pl.BlockSpec((tm, D_MODEL), row), pl.BlockSpec((tm, D_MODEL), row)]
        + [pl.BlockSpec((1, D_MODEL), fix), once((S5_WIDTH, S5_WIDTH), fix), pl.BlockSpec((1, D_MODEL), fix),
           once((S5_WIDTH, D_MODEL), fix), once((M2_WIDTH, D_MODEL), fix),
           pl.BlockSpec((1, D_MODEL), fix), once((nk * PEER_HALF, D_MODEL), fix),
           once((nk, PEER_N_KEYS, PEER_HALF), lambda i: (0, 0, 0))],
        out_specs=[pl.BlockSpec((tm, D_MODEL), row), pl.BlockSpec((D_MODEL, tm), lambda i: (0, i)),
                   pl.BlockSpec((nk, PEER_N_KEYS, tm), lambda i: (0, 0, i))],
        out_shape=[jax.ShapeDtypeStruct((t, D_MODEL), F32), jax.ShapeDtypeStruct((D_MODEL, t), BF16),
                   jax.ShapeDtypeStruct((nk, PEER_N_KEYS, t), F32)],
        compiler_params=_cparams("parallel"),
        name="post",
    )(y5, u, ym, x, *prm)


def _sorting_network(n):
    pairs = []
    p = 1
    while p < n:
        k = p
        while k >= 1:
            for j in range(k % p, n - k, 2 * k):
                for i in range(min(k, n - j - k)):
                    if (i + j) // (2 * p) == (i + j + k) // (2 * p):
                        pairs.append((i + j, i + j + k))
            k //= 2
        p *= 2
    return pairs


SUBLANES = 8
_NET16 = _sorting_network(PEER_N_KEYS // SUBLANES)
LOG2E = math.log2(math.e)


def _top16(tiles):
    n = len(tiles)
    tiles = list(tiles)

    def exchange(i, j):
        hi, lo = tiles[i], tiles[j]
        if lo is None:
            return
        if hi is None:
            tiles[i], tiles[j] = lo, None
        else:
            tiles[i], tiles[j] = jnp.maximum(hi, lo), jnp.minimum(hi, lo)

    def larger(a, b):
        return b if a is None else a if b is None else jnp.maximum(a, b)

    for i, j in _NET16:
        exchange(i, j)
    shift = SUBLANES // 2
    while shift >= 1:
        other = [None if t is None else pltpu.roll(t, shift, axis=0) for t in tiles]
        tiles = [larger(tiles[i], other[n - 1 - i]) for i in range(n)]
        dist = n // 2
        while dist >= 1:
            for i in range(n):
                if i & dist == 0:
                    exchange(i, i + dist)
            dist //= 2
        shift //= 2
    return tiles


def _top_values(s):
    n = PEER_N_KEYS // SUBLANES
    tiles = _top16([s[i * SUBLANES:(i + 1) * SUBLANES, :] for i in range(n)])
    rows = lax.broadcasted_iota(jnp.int32, (PEER_TOPK, s.shape[1]), 0)
    out = jnp.zeros((PEER_TOPK, s.shape[1]), F32)
    for r in range(PEER_TOPK):
        out = jnp.where(rows == r, jnp.concatenate([tiles[r], tiles[r]], axis=0), out)
    return out


def _pair_candidates(v1, v2):
    r8 = lax.broadcasted_iota(jnp.int32, (SUBLANES, v1.shape[1]), 0)
    r16 = lax.broadcasted_iota(jnp.int32, (PEER_TOPK, v1.shape[1]), 0)
    neg = -jnp.inf
    lo2 = v2[0:SUBLANES, :]
    return jnp.concatenate([
        v1[0:1, :] + v2,
        v1[1:2, :] + lo2,
        jnp.where(r16 >= 2, v1 + v2[0:1, :], neg),
        jnp.where(r8 >= 2, v1[0:SUBLANES, :] + v2[1:2, :], neg),
        jnp.where((r8 >= 2) & (r8 <= 4), v1[2:3, :] + lo2, neg),
        jnp.where((r8 >= 2) & (r8 <= 3), v1[3:4, :] + lo2, neg),
        jnp.where(r8 == 2, v1[4:5, :] + lo2, neg),
    ], axis=0)


def _route_kernel(st_ref, s1n_ref, s2n_ref, thr_ref):
    def head(h, carry):
        s1 = st_ref[2 * h]
        s2 = st_ref[2 * h + 1]
        v1 = _top_values(s1)
        v2 = _top_values(s2)
        cand = _pair_candidates(v1, v2)
        n_cand = cand.shape[0] // SUBLANES
        pad = [None] * (PEER_N_KEYS // SUBLANES - n_cand)
        ranked = _top16([cand[i * SUBLANES:(i + 1) * SUBLANES, :] for i in range(n_cand)] + pad)
        theta = ranked[PEER_TOPK - 1][0:1, :]
        sel = cand >= theta
        m = v1[0:1, :] + v2[0:1, :]
        zsum = jnp.sum(jnp.where(sel, jnp.exp2(cand - m), 0.0), axis=0, keepdims=True)
        off = m + jnp.log(zsum) * LOG2E + 1.0
        s1n_ref[h] = s1 - off
        s2n_ref[h] = s2
        candn = _pair_candidates(v1 - off, v2)
        thr_ref[h] = jnp.min(jnp.where(sel, candn, jnp.inf), axis=0, keepdims=True)
        return carry

    lax.fori_loop(0, PEER_HEADS, head, 0)


def _route(st, tl):
    nk, keys, t = st.shape
    spec = lambda n: pl.BlockSpec((n, keys, tl), lambda i: (0, 0, i))
    return pl.pallas_call(
        _route_kernel,
        grid=(t // tl,),
        in_specs=[spec(nk)],
        out_specs=[spec(PEER_HEADS), spec(PEER_HEADS), pl.BlockSpec((PEER_HEADS, 1, tl), lambda i: (0, 0, i))],
        out_shape=[jax.ShapeDtypeStruct((PEER_HEADS, keys, t), F32), jax.ShapeDtypeStruct((PEER_HEADS, keys, t), F32),
                   jax.ShapeDtypeStruct((PEER_HEADS, 1, t), F32)],
        compiler_params=_cparams("parallel"),
        name="route",
    )(st)


PEER_E1_BLK = 16
PEER_CHUNK_KEYS = (4, 4, 4, 4)
assert sum(PEER_CHUNK_KEYS) == PEER_E1_BLK


def _peer_kernel(u_ref, vt_ref, xt_ref, s2n_ref, s1n_ref, thr_ref, x1_ref, fnw_ref, y_ref, acc_scr, *, tm):
    k = pl.program_id(1)
    c0 = math.sqrt(2.0 / math.pi)
    c1 = c0 * 0.044715
    first_key = [sum(PEER_CHUNK_KEYS[:q]) for q in range(len(PEER_CHUNK_KEYS))]

    def experts(q):
        return slice(first_key[q] * PEER_N_KEYS, (first_key[q] + PEER_CHUNK_KEYS[q]) * PEER_N_KEYS)

    def scores(q):
        return jnp.dot(u_ref[experts(q), :], xt_ref[...], preferred_element_type=F32)

    def gated(q, a):
        wg_rows = []
        for i in range(PEER_CHUNK_KEYS[q]):
            r = first_key[q] + i
            rows = slice(i * PEER_N_KEYS, (i + 1) * PEER_N_KEYS)
            wg_cols = []
            for c in range(tm // LANES):
                cols = slice(c * LANES, (c + 1) * LANES)
                w = None
                for h in range(PEER_HEADS):
                    arg = s2n_ref[h, :, cols] + s1n_ref[h, r:r + 1, cols]
                    term = jnp.where(arg >= thr_ref[h, :, cols], jnp.exp2(arg), 0.0)
                    w = term if w is None else w + term
                x = a[rows, cols]
                g = x + x * jnp.tanh(x * (c0 + c1 * (x * x)))
                wg_cols.append((w * g).astype(BF16))
            wg_rows.append(jnp.concatenate(wg_cols, axis=1))
        return jnp.concatenate(wg_rows, axis=0)

    def mixed(q, wg):
        return jnp.dot(vt_ref[0, :, experts(q)], wg, preferred_element_type=F32)

    n_chunks = len(PEER_CHUNK_KEYS)
    total = None
    a_next = scores(0)
    wg_prev = None
    for q in range(n_chunks):
        a_cur = a_next
        if q + 1 < n_chunks:
            a_next = scores(q + 1)
        wg = gated(q, a_cur)
        if wg_prev is not None:
            d = mixed(q - 1, wg_prev)
            total = d if total is None else total + d
        wg_prev = wg
    total = total + mixed(n_chunks - 1, wg_prev)

    @pl.when(k == 0)
    def _():
        acc_scr[...] = total

    @pl.when(k > 0)
    def _():
        acc_scr[...] += total

    @pl.when(k == pl.num_programs(1) - 1)
    def _():
        y_ref[...] = _rms(x1_ref[...] + acc_scr[...].T, fnw_ref[...])


def _peer(u_bf, vt_bf, xt, route, x1, fnw, tm):
    s1n, s2n, thr = route
    t = x1.shape[0]
    eb = PEER_E1_BLK * PEER_N_KEYS
    per_key = pl.BlockSpec((PEER_HEADS, PEER_N_KEYS, tm), lambda j, k: (0, 0, j))
    per_blk = pl.BlockSpec((PEER_HEADS, PEER_E1_BLK, tm), lambda j, k: (0, k, j))
    return pl.pallas_call(
        functools.partial(_peer_kernel, tm=tm),
        grid=(t // tm, PEER_N_KEYS // PEER_E1_BLK),
        in_specs=[pl.BlockSpec((eb, D_MODEL), lambda j, k: (k, 0)),
                  pl.BlockSpec((1, D_MODEL, eb), lambda j, k: (k, 0, 0)),
                  pl.BlockSpec((D_MODEL, tm), lambda j, k: (0, j)),
                  per_key, per_blk, pl.BlockSpec((PEER_HEADS, 1, tm), lambda j, k: (0, 0, j)),
                  pl.BlockSpec((tm, D_MODEL), lambda j, k: (j, 0)),
                  pl.BlockSpec((1, D_MODEL), lambda j, k: (0, 0))],
        out_specs=pl.BlockSpec((tm, D_MODEL), lambda j, k: (j, 0)),
        out_shape=jax.ShapeDtypeStruct((t, D_MODEL), F32),
        scratch_shapes=[pltpu.VMEM((D_MODEL, tm), F32)],
        compiler_params=_cparams("parallel", "arbitrary"),
        name="peer",
    )(u_bf, vt_bf, xt, s2n, s1n, thr, x1, fnw)


TOKEN_TILE = 512


def _tokens_tail(x, post_prm, peer_prm, y5, u, ym, s5_block):
    x1, xt, st = _post(y5, u, ym, x, post_prm, tm=TOKEN_TILE, s5_block=s5_block)
    u_bf, vt_bf, fnw = peer_prm
    return _peer(u_bf, vt_bf, xt, _route(st, tl=LANES), x1, fnw, tm=512)


def kernel(x_prompt, x_sample, state_s5_re, state_s5_im, state_ssm, state_conv, meta_tokens, norm_mix_w, w_in,
           s5_lambda_re, s5_lambda_im, s5_log_step, s5_b_re, s5_b_im, s5_c_re, s5_c_im, s5_d, s5_w_glu, s5_norm_w,
           m2_conv_w, m2_conv_b, m2_dt_bias, m2_a_log, m2_d, m2_norm_w, w_out, norm_ffn_w, peer_w_q, peer_sub_keys,
           peer_u, peer_v, final_norm_w):
    bp, sp, _ = x_prompt.shape
    bs, ss, _ = x_sample.shape
    g, p = S5_GROUPS, S5_STATE

    w = w_in[0]
    o1, o2, o3 = S5_WIDTH, S5_WIDTH + M2_WIDTH, S5_WIDTH + M2_WIDTH + M2_CONV_DIM
    wu, wz, wx = w[:, :o1].astype(BF16), w[:, o1:o2].astype(BF16), w[:, o2:o3].astype(BF16)
    wd = jnp.pad(w[:, o3:], ((0, 0), (0, LANES - M2_HEADS))).astype(BF16)
    nmw = norm_mix_w[0][None, :]
    pad_h = lambda v: jnp.pad(v, (0, LANES - M2_HEADS))[None, :]
    ssd_prm = (m2_conv_w[0], m2_conv_b[0][None, :], pad_h(m2_dt_bias[0]), pad_h(m2_a_log[0]),
               jnp.repeat(m2_d[0], M2_HEAD_DIM)[None, :], m2_norm_w[0][None, :])
    post_prm = (s5_d[0][None, :], s5_w_glu[0].astype(BF16), s5_norm_w[0][None, :],
                w_out[0][:S5_WIDTH].astype(BF16), w_out[0][S5_WIDTH:].astype(BF16), norm_ffn_w[0][None, :],
                peer_w_q[0].T.astype(BF16),
                peer_sub_keys[0].reshape(2 * PEER_HEADS, PEER_N_KEYS, PEER_HALF).astype(BF16))
    eb = PEER_E1_BLK * PEER_N_KEYS
    vt_blocks = peer_v[0].astype(BF16).reshape(PEER_EXPERTS // eb, eb, D_MODEL).swapaxes(1, 2)
    peer_prm = (peer_u[0].astype(BF16), vt_blocks, final_norm_w[None, :])
    s5_args = (s5_lambda_re[0], s5_lambda_im[0], s5_log_step[0], s5_b_re[0], s5_b_im[0], s5_c_re[0], s5_c_im[0])
    ops16 = _s5_prep(*s5_args, lc=N_META)
    ops8 = _s5_prep(*s5_args, lc=ss)

    lc = N_META
    um, _, xbcm, dtm = _in_proj(meta_tokens, nmw, wu, wz, wx, wd, tm=N_META)
    nb = S5_WIDTH // LANES
    urm = jnp.broadcast_to(um.reshape(lc, nb, LANES).transpose(1, 0, 2).reshape(nb, 1, lc * LANES),
                           (nb, SUBLANES, lc * LANES))
    _, h5m = _s5(urm, jnp.zeros((g, SUBLANES, 2 * p), F32), ops16, lc=lc, rows=SUBLANES, chunks=1)
    _, convm, hm = _ssd(xbcm, jnp.zeros((N_META, M2_WIDTH), F32), dtm,
                        jnp.zeros((1, M2_CONV - 1, M2_CONV_DIM), F32), jnp.zeros((1, M2_D_STATE, M2_WIDTH), F32),
                        ssd_prm, n_seq=1, n_chunks=1, q_in=N_META, shared_init=True)

    xp = x_prompt.reshape(bp * sp, D_MODEL)
    nc = sp // lc
    cpt = TOKEN_TILE // lc
    tps = sp // TOKEN_TILE
    p_shape = (nb, nc, bp, lc, LANES)
    p_block = ((nb, cpt, 1, lc, LANES), lambda i: (0, i % tps, i // tps, 0, 0))
    up, zp, xbcp, dtp = _in_proj(xp, nmw, wu, wz, wx, wd, tm=TOKEN_TILE, u_layout=(p_shape,) + p_block)
    y5p, h5p = _s5(up.reshape(nb, nc * bp, lc * LANES), h5m, ops16, lc=lc, rows=bp, chunks=S5_ROW_TILE // bp)
    ymp, convp, hp = _ssd(xbcp, zp, dtp, convm, hm, ssd_prm, n_seq=bp, n_chunks=sp // SSD_CHUNK, q_in=SSD_CHUNK,
                          shared_init=True)
    y_prompt = _tokens_tail(xp, post_prm, peer_prm, y5p.reshape(p_shape), up, ymp, p_block).reshape(bp, sp, D_MODEL)

    xs = x_sample.reshape(bs * ss, D_MODEL)
    s_shape = (nb, bs, ss, LANES)
    s_block = ((nb, TOKEN_TILE // ss, ss, LANES), lambda i: (0, i, 0, 0))
    us, zs, xbcs, dts = _in_proj(xs, nmw, wu, wz, wx, wd, tm=TOKEN_TILE, u_layout=(s_shape,) + s_block)
    h5s0 = jnp.concatenate([state_s5_re[0], state_s5_im[0]], axis=-1).transpose(1, 0, 2)
    y5s, h5s = _s5(us.reshape(nb, bs, ss * LANES), h5s0, ops8, lc=ss, rows=bs, chunks=1)
    hs0 = state_ssm[0].reshape(bs, M2_WIDTH, M2_D_STATE).transpose(0, 2, 1)
    yms, convs, hs = _ssd_packed(xbcs, zs, dts, state_conv[0], hs0, ssd_prm, n_seq=bs, seq_len=ss)
    y_sample = _tokens_tail(xs, post_prm, peer_prm, y5s.reshape(s_shape), us, yms, s_block).reshape(bs, ss, D_MODEL)

    def s5_state(hf):
        hf = hf.transpose(1, 0, 2)
        return hf[None, :, :, :p], hf[None, :, :, p:]

    def ssm_state(ht):
        return ht.transpose(0, 2, 1).reshape(1, ht.shape[0], M2_HEADS, M2_HEAD_DIM, M2_D_STATE)

    p5r, p5i = s5_state(h5p)
    s5r, s5i = s5_state(h5s)
    return (y_prompt, y_sample, p5r, p5i, ssm_state(hp), convp[None], s5r, s5i, ssm_state(hs), convs[None])
```

```python
import functools
import math

import jax
import jax.numpy as jnp
from jax import lax
from jax.experimental import pallas as pl
from jax.experimental.pallas import tpu as pltpu

F32 = jnp.float32
BF16 = jnp.bfloat16
HIGHEST = lax.Precision.HIGHEST

D_MODEL = 1024
N_META = 16
S5_WIDTH = 1024
S5_CH = 16
S5_GROUPS = S5_WIDTH // S5_CH
S5_STATE = 64
M2_WIDTH = 1024
M2_HEAD_DIM = 64
M2_HEADS = M2_WIDTH // M2_HEAD_DIM
M2_GROUPS = 2
M2_D_STATE = 128
M2_CONV = 4
M2_CONV_DIM = M2_WIDTH + 2 * M2_GROUPS * M2_D_STATE
PEER_HEADS = 8
PEER_N_KEYS = 128
PEER_EXPERTS = PEER_N_KEYS * PEER_N_KEYS
PEER_HALF = 128
PEER_TOPK = 16
EPS = 1e-6

LANES = 128
SSD_CHUNK = 128
CONV_HALO = 8
VMEM_LIMIT = 56 * 1024 * 1024


def _cparams(*sem):
    return pltpu.CompilerParams(dimension_semantics=sem, vmem_limit_bytes=VMEM_LIMIT)


def _sigmoid(x):
    return 1.0 / (1.0 + jnp.exp(-x))


def _gelu_tanh(x):
    c = math.sqrt(2.0 / math.pi)
    return 0.5 * x * (1.0 + jnp.tanh(c * (x + 0.044715 * (x * x * x))))


def _rms(x, w):
    return x * lax.rsqrt(jnp.mean(x * x, axis=-1, keepdims=True) + EPS) * w


def _bdot(a, b):
    return jnp.dot(a.astype(BF16), b.astype(BF16), preferred_element_type=F32)


def _store_blocked(ref, val):
    if len(ref.shape) == 2:
        ref[...] = val
        return
    for gb in range(ref.shape[0]):
        piece = val[:, gb * LANES:(gb + 1) * LANES]
        if len(ref.shape) == 5:
            ref[gb, :, 0] = piece.reshape(ref.shape[1], ref.shape[3], LANES)
        else:
            ref[gb] = piece.reshape(ref.shape[1:])


def _load_blocked(ref):
    if len(ref.shape) == 2:
        return ref[...]
    parts = []
    for gb in range(ref.shape[0]):
        piece = ref[gb, :, 0] if len(ref.shape) == 5 else ref[gb]
        parts.append(piece.reshape(piece.shape[0] * piece.shape[1], LANES))
    return jnp.concatenate(parts, axis=1)


def _in_proj_kernel(x_ref, nw_ref, wu_ref, wz_ref, wx_ref, wd_ref, u_ref, z_ref, xbc_ref, dt_ref):
    hb = _rms(x_ref[...], nw_ref[...]).astype(BF16)
    _store_blocked(u_ref, jnp.dot(hb, wu_ref[...], preferred_element_type=F32))
    z_ref[...] = jnp.dot(hb, wz_ref[...], preferred_element_type=F32)
    xbc_ref[...] = jnp.dot(hb, wx_ref[...], preferred_element_type=F32)
    dt_ref[...] = jnp.dot(hb, wd_ref[...], preferred_element_type=F32)


def _in_proj(x, nw, wu, wz, wx, wd, tm, u_layout=None):
    t = x.shape[0]
    row = lambda i: (i, 0)
    fix = lambda i: (0, 0)
    widths = (S5_WIDTH, M2_WIDTH, M2_CONV_DIM, LANES)
    out_specs = [pl.BlockSpec((tm, w), row) for w in widths]
    out_shape = [jax.ShapeDtypeStruct((t, w), F32) for w in widths]
    if u_layout is not None:
        out_specs[0] = pl.BlockSpec(u_layout[1], u_layout[2])
        out_shape[0] = jax.ShapeDtypeStruct(u_layout[0], F32)
    return pl.pallas_call(
        _in_proj_kernel,
        grid=(t // tm,),
        in_specs=[pl.BlockSpec((tm, D_MODEL), row), pl.BlockSpec((1, D_MODEL), fix)]
        + [pl.BlockSpec((D_MODEL, w), fix) for w in widths],
        out_specs=out_specs,
        out_shape=out_shape,
        compiler_params=_cparams("parallel"),
        name="in_proj",
    )(x, nw, wu, wz, wx, wd)


def _s5_prep_kernel(lrc_ref, lic_ref, lrr_ref, lir_ref, ls_ref, btr_ref, bti_ref, ctr_ref, cti_ref,
                    kt_ref, wout_ref, wsr_ref, wsi_ref, al_ref, *, lc):
    for j in range(S5_GBLK):
        _s5_prep_group(j, lc, lrc_ref, lic_ref, lrr_ref, lir_ref, ls_ref, btr_ref, bti_ref, ctr_ref, cti_ref,
                       kt_ref, wout_ref, wsr_ref, wsi_ref, al_ref)


def _s5_prep_group(j, lc, lrc_ref, lic_ref, lrr_ref, lir_ref, ls_ref, btr_ref, bti_ref, ctr_ref, cti_ref,
                   kt_ref, wout_ref, wsr_ref, wsi_ref, al_ref):
    k = lc * S5_CH
    step = jnp.exp(ls_ref[j])

    def disc(lr, li):
        lg = lr * step
        th = li * step
        mag = jnp.exp(lg)
        ab_re = mag * jnp.cos(th)
        ab_im = mag * jnp.sin(th)
        den = lr * lr + li * li
        f_re = ((ab_re - 1.0) * lr + ab_im * li) / den
        f_im = (ab_im * lr - (ab_re - 1.0) * li) / den
        return ab_re, ab_im, f_re, f_im

    def powers(n, a_re_, a_im_):
        p_re_ = jnp.ones(n.shape, F32)
        p_im_ = jnp.zeros(n.shape, F32)
        b_re_, b_im_ = a_re_, a_im_
        bit = 1
        while bit <= lc:
            use = (n & bit) != 0
            t_re = p_re_ * b_re_ - p_im_ * b_im_
            t_im = p_re_ * b_im_ + p_im_ * b_re_
            p_re_ = jnp.where(use, t_re, p_re_)
            p_im_ = jnp.where(use, t_im, p_im_)
            b_re_, b_im_ = b_re_ * b_re_ - b_im_ * b_im_, 2.0 * (b_re_ * b_im_)
            bit *= 2
        return p_re_, p_im_

    ac_re, ac_im, f_re, f_im = disc(lrc_ref[j], lic_ref[j])
    tau = lax.shift_right_logical(lax.broadcasted_iota(jnp.int32, (S5_STATE, k), 1), 4)
    c_re = ctr_ref[j]
    c_im = cti_ref[j]
    p_re, p_im = powers(tau, ac_re, ac_im)
    fa_re = f_re * p_re - f_im * p_im
    fa_im = f_re * p_im + f_im * p_re
    fca_re = fa_re * c_re - fa_im * c_im
    fca_im = fa_re * c_im + fa_im * c_re
    kt_ref[j] = (jnp.dot(btr_ref[j], fca_re, precision=HIGHEST, preferred_element_type=F32)
                 - jnp.dot(bti_ref[j], fca_im, precision=HIGHEST, preferred_element_type=F32))
    q_re, q_im = powers(tau + 1, ac_re, ac_im)
    wout_ref[j, 0:S5_STATE, :] = q_re * c_re - q_im * c_im
    wout_ref[j, S5_STATE:2 * S5_STATE, :] = -(q_re * c_im + q_im * c_re)

    ar_re, ar_im, fr_re, fr_im = disc(lrr_ref[j], lir_ref[j])
    kk = lax.broadcasted_iota(jnp.int32, (lc, S5_STATE), 0)
    r_re, r_im = powers(kk, ar_re, ar_im)
    g_re = fr_re * r_re - fr_im * r_im
    g_im = fr_re * r_im + fr_im * r_re
    bt_re = btr_ref[j]
    bt_im = bti_ref[j]
    for s in range(lc):
        kpow = lc - 1 - s
        w_re = g_re[kpow:kpow + 1, :]
        w_im = g_im[kpow:kpow + 1, :]
        wsr_ref[j, s * S5_CH:(s + 1) * S5_CH, :] = bt_re * w_re - bt_im * w_im
        wsi_ref[j, s * S5_CH:(s + 1) * S5_CH, :] = bt_re * w_im + bt_im * w_re
    a_re, a_im = powers(jnp.full((1, S5_STATE), lc, jnp.int32), ar_re, ar_im)
    al_ref[j, 0:1, :] = a_re
    al_ref[j, 1:2, :] = a_im


def _s5_prep(lam_re, lam_im, log_step, b_re, b_im, c_re, c_im, lc):
    g, p, k = S5_GROUPS, S5_STATE, lc * S5_CH
    spec3 = lambda a, b: pl.BlockSpec((S5_GBLK, a, b), lambda i: (i, 0, 0))
    ins = [lam_re.reshape(g, p, 1), lam_im.reshape(g, p, 1), lam_re.reshape(g, 1, p), lam_im.reshape(g, 1, p),
           log_step.reshape(g, 1, 1),
           jnp.swapaxes(b_re, 1, 2), jnp.swapaxes(b_im, 1, 2),
           jnp.tile(jnp.swapaxes(c_re, 1, 2), (1, 1, lc)), jnp.tile(jnp.swapaxes(c_im, 1, 2), (1, 1, lc))]
    kt, wout, wsr, wsi, al = pl.pallas_call(
        functools.partial(_s5_prep_kernel, lc=lc),
        grid=(g // S5_GBLK,),
        in_specs=[spec3(p, 1), spec3(p, 1), spec3(1, p), spec3(1, p), spec3(1, 1),
                  spec3(S5_CH, p), spec3(S5_CH, p), spec3(p, k), spec3(p, k)],
        out_specs=[spec3(S5_CH, k), spec3(2 * p, k), spec3(k, p), spec3(k, p), spec3(2, p)],
        out_shape=[jax.ShapeDtypeStruct((g, S5_CH, k), F32), jax.ShapeDtypeStruct((g, 2 * p, k), F32),
                   jax.ShapeDtypeStruct((g, k, p), F32), jax.ShapeDtypeStruct((g, k, p), F32),
                   jax.ShapeDtypeStruct((g, 2, p), F32)],
        compiler_params=_cparams("parallel"),
        name="s5_prep",
    )(*ins)
    lag = jnp.arange(lc)[None, :] - jnp.arange(lc)[:, None]
    kt4 = kt.reshape(g, S5_CH, lc, S5_CH)
    toep = kt4[:, :, jnp.maximum(lag, 0), :]
    toep = jnp.where((lag >= 0)[None, None, :, :, None], toep, 0.0)
    toep = toep.transpose(0, 2, 1, 3, 4).reshape(g, k, k)
    wst = jnp.concatenate([wsr, wsi, wsi, wsr], axis=-1)
    a_re, a_im = al[:, 0:1, :], al[:, 1:2, :]
    acoef = jnp.concatenate([jnp.concatenate([a_re, a_re], -1), jnp.concatenate([-a_im, a_im], -1),
                             jnp.concatenate([a_im, -a_im], -1)], axis=1)
    return toep.astype(BF16), wst.astype(BF16), wout.astype(BF16), acoef


S5_GBLK = LANES // S5_CH
S5_ROW_TILE = 256


def _block_transpose(cols):
    n = len(cols)
    blk = lax.shift_right_logical(lax.broadcasted_iota(jnp.int32, (1, LANES), 1), 4)
    rolled = []
    for d in range(n):
        m = cols[d]
        for b in range(1, n):
            m = jnp.where(blk == b, cols[(b + d) % n], m)
        rolled.append(m if d == 0 else pltpu.roll(m, d * S5_CH, axis=1))
    outs = []
    for j in range(n):
        o = rolled[(-j) % n]
        for s in range(1, n):
            o = jnp.where(blk == s, rolled[(s - j) % n], o)
        outs.append(o)
    return outs


def _s5_kernel(u_ref, h0_ref, h0s_ref, wst_ref, toep_ref, wout_ref, a_ref, y_ref, hf_ref,
               s_scr, hin_scr, h_scr, hs_scr, *, lc, rows, chunks):
    i = pl.program_id(1)

    @pl.when(i == 0)
    def _():
        h_scr[...] = h0_ref[...]
        hs_scr[...] = h0s_ref[...]

    x = u_ref[0]
    halves = [_block_transpose([x[:, (S5_GBLK * c + s) * LANES:(S5_GBLK * c + s + 1) * LANES] for s in range(S5_GBLK)])
              for c in range(lc // S5_GBLK)]
    ubs = []
    for j in range(S5_GBLK):
        ubs.append(jnp.concatenate([hv[j] for hv in halves], axis=1).astype(BF16))
        s_scr[j] = jnp.dot(ubs[j], wst_ref[j], preferred_element_type=F32)

    def step(r0, carry):
        new = []
        for j in range(S5_GBLK):
            h, hs = carry[2 * j], carry[2 * j + 1]
            hin_scr[j, pl.ds(r0, rows), :] = h
            s = s_scr[j, pl.ds(r0, rows), :]
            a1, a2, a2s = a_ref[j, 0:1, :], a_ref[j, 1:2, :], a_ref[j, 2:3, :]
            new.append(a1 * h + a2 * hs + s[:, :LANES])
            new.append(a1 * hs + a2s * h + s[:, LANES:])
        return tuple(new)

    carry = tuple(ref[j] for j in range(S5_GBLK) for ref in (h_scr, hs_scr))
    if chunks == 1:
        carry = step(0, carry)
    else:
        carry = lax.fori_loop(0, chunks, lambda n, c: step(pl.multiple_of(n * rows, rows), c), carry)
    for j in range(S5_GBLK):
        h_scr[j] = carry[2 * j]
        hs_scr[j] = carry[2 * j + 1]
    hf_ref[...] = h_scr[...]

    ys = [jnp.dot(ubs[j], toep_ref[j], preferred_element_type=F32)
          + jnp.dot(hin_scr[j].astype(BF16), wout_ref[j], preferred_element_type=F32) for j in range(S5_GBLK)]
    for c in range(lc // S5_GBLK):
        token_major = _block_transpose([ys[j][:, c * LANES:(c + 1) * LANES] for j in range(S5_GBLK)])
        for t in range(S5_GBLK):
            y_ref[0, :, (S5_GBLK * c + t) * LANES:(S5_GBLK * c + t + 1) * LANES] = token_major[t]


def _s5(ub, h0, ops, lc, rows, chunks):
    toep, wst, wout, acoef = ops
    nb, nr, width = ub.shape
    k = lc * S5_CH
    rt = rows * chunks
    blk = lambda a, b: pl.BlockSpec((S5_GBLK, a, b), lambda gb, i: (gb, 0, 0))
    return pl.pallas_call(
        functools.partial(_s5_kernel, lc=lc, rows=rows, chunks=chunks),
        grid=(nb, nr // rt),
        in_specs=[pl.BlockSpec((1, rt, width), lambda gb, i: (gb, i, 0)),
                  blk(rows, LANES), blk(rows, LANES), blk(k, 2 * LANES), blk(k, k), blk(LANES, k), blk(3, LANES)],
        out_specs=[pl.BlockSpec((1, rt, width), lambda gb, i: (gb, i, 0)), blk(rows, LANES)],
        out_shape=[jax.ShapeDtypeStruct((nb, nr, width), F32), jax.ShapeDtypeStruct((S5_GROUPS, rows, LANES), F32)],
        scratch_shapes=[pltpu.VMEM((S5_GBLK, rt, 2 * LANES), F32), pltpu.VMEM((S5_GBLK, rt, LANES), F32),
                        pltpu.VMEM((S5_GBLK, rows, LANES), F32), pltpu.VMEM((S5_GBLK, rows, LANES), F32)],
        compiler_params=_cparams("parallel", "arbitrary"),
        name="s5_scan",
    )(ub, h0, jnp.roll(h0, S5_STATE, axis=-1), wst, toep, wout, acoef)


def _ssd_kernel(xbc_ref, z_ref, dt_ref, conv0_ref, h0_ref, cw_ref, cb_ref, dtb_ref, alog_ref, dexp_ref, nw_ref,
                y_ref, convn_ref, hn_ref, buf_scr, dt_scr, h_scr, *, q_in, n_chunks):
    q = SSD_CHUNK
    pad = q - q_in
    first = CONV_HALO + pad
    c = pl.program_id(1)

    @pl.when(c == 0)
    def _():
        buf_scr[0:first, :] = jnp.zeros((first, M2_CONV_DIM), F32)
        buf_scr[first - (M2_CONV - 1):first, :] = conv0_ref[0]
        h_scr[...] = h0_ref[0]

    buf_scr[first:CONV_HALO + q, :] = xbc_ref[...]
    conv = cb_ref[...]
    for kk in range(M2_CONV):
        lo = CONV_HALO - (M2_CONV - 1) + kk
        conv = conv + cw_ref[kk:kk + 1, :] * buf_scr[lo:lo + q, :]
    convn_ref[0] = buf_scr[CONV_HALO + q - (M2_CONV - 1):CONV_HALO + q, :]
    if n_chunks > 1:
        buf_scr[0:CONV_HALO, :] = buf_scr[q:q + CONV_HALO, :]
    act = conv * _sigmoid(conv)
    xs = act[:, :M2_WIDTH]

    dtv = dt_ref[...] + dtb_ref[...]
    dt_real = jnp.maximum(dtv, 0.0) + jnp.log1p(jnp.exp(-jnp.abs(dtv)))
    if pad:
        dt_scr[0:pad, :] = jnp.zeros((pad, LANES), F32)
        dt_scr[pad:q, :] = dt_real
        dt = dt_scr[...]
    else:
        dt = dt_real
    a_neg = -jnp.exp(alog_ref[...])
    ri = lax.broadcasted_iota(jnp.int32, (q, q), 0)
    ci = lax.broadcasted_iota(jnp.int32, (q, q), 1)
    tril = ri >= ci
    acs = jnp.dot(tril.astype(F32), dt * a_neg, precision=HIGHEST, preferred_element_type=F32)
    acs_t = acs.T
    dt_t = dt.T
    acs_last = acs[q - 1:q, :]
    eacs = jnp.exp(acs)
    wdec = jnp.exp(acs_last - acs) * dt
    dec_last = jnp.exp(acs_last)
    low = lax.broadcasted_iota(jnp.int32, (q, LANES), 1) < M2_HEAD_DIM
    low1 = low[0:1, :]

    def pair_cols(m, h0):
        return jnp.where(low[0:m.shape[0], :], m[:, h0:h0 + 1], m[:, h0 + 1:h0 + 2])

    y_parts = []
    hpg = M2_HEADS // M2_GROUPS
    gw = hpg * M2_HEAD_DIM
    for g in range(M2_GROUPS):
        bg = act[:, M2_WIDTH + g * M2_D_STATE:M2_WIDTH + (g + 1) * M2_D_STATE]
        cg = act[:, M2_WIDTH + (M2_GROUPS + g) * M2_D_STATE:M2_WIDTH + (M2_GROUPS + g + 1) * M2_D_STATE]
        cgb = cg.astype(BF16)
        cb = lax.dot_general(cgb, bg.astype(BF16), (((1,), (1,)), ((), ())), preferred_element_type=F32)
        hg = h_scr[:, g * gw:(g + 1) * gw]
        yoff = jnp.dot(cgb, hg.astype(BF16), preferred_element_type=F32)
        xw_parts = []
        dec_parts = []
        for pp in range(hpg // 2):
            h0 = g * hpg + 2 * pp
            lanes = slice(h0 * M2_HEAD_DIM, (h0 + 2) * M2_HEAD_DIM)
            w_pair = []
            for h in (h0, h0 + 1):
                seg = acs[:, h:h + 1] - acs_t[h:h + 1, :]
                dec = jnp.exp(jnp.where(tril, seg, -jnp.inf))
                w_pair.append((cb * dec * dt_t[h:h + 1, :]).astype(BF16))
            xp = xs[:, lanes]
            xbd = jnp.concatenate([jnp.where(low, xp, 0.0), jnp.where(low, 0.0, xp)], axis=0).astype(BF16)
            yd = jnp.dot(jnp.concatenate(w_pair, axis=1), xbd, preferred_element_type=F32)
            yo = yoff[:, 2 * pp * M2_HEAD_DIM:(2 * pp + 2) * M2_HEAD_DIM]
            y_parts.append(yd + yo * pair_cols(eacs, h0) + dexp_ref[:, lanes] * xp)
            xw_parts.append(xp * pair_cols(wdec, h0))
            dec_parts.append(jnp.where(low1, dec_last[:, h0:h0 + 1], dec_last[:, h0 + 1:h0 + 2]))
        xw = jnp.concatenate(xw_parts, axis=1).astype(BF16)
        st = jnp.dot(bg.T.astype(BF16), xw, preferred_element_type=F32)
        h_scr[:, g * gw:(g + 1) * gw] = hg * jnp.concatenate(dec_parts, axis=1) + st

    @pl.when(c == n_chunks - 1)
    def _():
        hn_ref[0] = h_scr[...]

    y = jnp.concatenate(y_parts, axis=1)[pad:, :]
    zz = z_ref[...]
    y = y * (zz * _sigmoid(zz))
    outs = []
    for g in range(M2_GROUPS):
        yg = y[:, g * gw:(g + 1) * gw]
        outs.append(yg * lax.rsqrt(jnp.mean(yg * yg, axis=-1, keepdims=True) + EPS))
    y_ref[...] = jnp.concatenate(outs, axis=1) * nw_ref[...]


def _ssd(xbc, z, dt, conv0, h0, prm, n_seq, n_chunks, q_in, shared_init):
    assert q_in == SSD_CHUNK or n_chunks == 1
    assert q_in >= CONV_HALO and q_in % CONV_HALO == 0
    t = xbc.shape[0]
    row = lambda b, c: (b * n_chunks + c, 0)
    fix = lambda b, c: (0, 0)
    init = (lambda b, c: (0, 0, 0)) if shared_init else (lambda b, c: (b, 0, 0))
    per_seq = lambda b, c: (b, 0, 0)
    return pl.pallas_call(
        functools.partial(_ssd_kernel, q_in=q_in, n_chunks=n_chunks),
        grid=(n_seq, n_chunks),
        in_specs=[pl.BlockSpec((q_in, M2_CONV_DIM), row), pl.BlockSpec((q_in, M2_WIDTH), row),
                  pl.BlockSpec((q_in, LANES), row),
                  pl.BlockSpec((1, M2_CONV - 1, M2_CONV_DIM), init),
                  pl.BlockSpec((1, M2_D_STATE, M2_WIDTH), init),
                  pl.BlockSpec((M2_CONV, M2_CONV_DIM), fix), pl.BlockSpec((1, M2_CONV_DIM), fix),
                  pl.BlockSpec((1, LANES), fix), pl.BlockSpec((1, LANES), fix),
                  pl.BlockSpec((1, M2_WIDTH), fix), pl.BlockSpec((1, M2_WIDTH), fix)],
        out_specs=[pl.BlockSpec((q_in, M2_WIDTH), row),
                   pl.BlockSpec((1, M2_CONV - 1, M2_CONV_DIM), per_seq),
                   pl.BlockSpec((1, M2_D_STATE, M2_WIDTH), per_seq)],
        out_shape=[jax.ShapeDtypeStruct((t, M2_WIDTH), F32),
                   jax.ShapeDtypeStruct((n_seq, M2_CONV - 1, M2_CONV_DIM), F32),
                   jax.ShapeDtypeStruct((n_seq, M2_D_STATE, M2_WIDTH), F32)],
        scratch_shapes=[pltpu.VMEM((CONV_HALO + SSD_CHUNK, M2_CONV_DIM), F32),
                        pltpu.VMEM((SSD_CHUNK, LANES), F32),
                        pltpu.VMEM((M2_D_STATE, M2_WIDTH), F32)],
        compiler_params=_cparams("parallel", "arbitrary"),
        name="ssd",
    )(xbc, z, dt, conv0, h0, *prm)


def _ssd_packed_kernel(xbc_ref, hist_ref, z_ref, dt_ref, h0_ref, cw_ref, cb_ref, dtb_ref, alog_ref, dexp_ref, nw_ref,
                       y_ref, hn_ref, buf_scr, *, seq_len):
    q = SSD_CHUNK
    n = M2_D_STATE
    ns = q // seq_len
    shift = seq_len.bit_length() - 1
    hpg = M2_HEADS // M2_GROUPS
    gw = hpg * M2_HEAD_DIM
    cdim = gw + 2 * n
    x = xbc_ref[0]
    buf_scr[0:CONV_HALO, :] = jnp.zeros((CONV_HALO, cdim), F32)
    buf_scr[CONV_HALO:CONV_HALO + q, :] = x
    pos = lax.broadcasted_iota(jnp.int32, (q, cdim), 0) & (seq_len - 1)
    conv = cb_ref[0] + cw_ref[0, M2_CONV - 1:M2_CONV, :] * x
    for k in range(1, M2_CONV):
        prev = jnp.where(pos >= k, buf_scr[CONV_HALO - k:CONV_HALO - k + q, :], hist_ref[k - 1, 0])
        conv = conv + cw_ref[0, M2_CONV - 1 - k:M2_CONV - k, :] * prev
    act = conv * _sigmoid(conv)
    xs = act[:, :gw]
    bg = act[:, gw:gw + n]
    cg = act[:, gw + n:]

    dtv = dt_ref[0] + dtb_ref[0]
    dt = jnp.maximum(dtv, 0.0) + jnp.log1p(jnp.exp(-jnp.abs(dtv)))
    a_neg = -jnp.exp(alog_ref[0])
    ri = lax.broadcasted_iota(jnp.int32, (q, q), 0)
    ci = lax.broadcasted_iota(jnp.int32, (q, q), 1)
    same = lax.shift_right_logical(ri, shift) == lax.shift_right_logical(ci, shift)
    causal = (ri >= ci) & same
    dta = dt * a_neg
    acs = jnp.dot(causal.astype(F32), dta, precision=HIGHEST, preferred_element_type=F32)
    tot = jnp.dot(same.astype(F32), dta, precision=HIGHEST, preferred_element_type=F32)
    acs_t = acs.T
    dt_t = dt.T
    eacs = jnp.exp(acs)
    wdec = jnp.exp(tot - acs) * dt
    dec_tot = jnp.exp(tot)
    low = lax.broadcasted_iota(jnp.int32, (q, LANES), 1) < M2_HEAD_DIM

    def pair_cols(m, h0):
        return jnp.where(low, m[:, h0:h0 + 1], m[:, h0 + 1:h0 + 2])

    cgb = cg.astype(BF16)
    cb = lax.dot_general(cgb, bg.astype(BF16), (((1,), (1,)), ((), ())), preferred_element_type=F32)
    row_seq = lax.shift_right_logical(lax.broadcasted_iota(jnp.int32, (q, n), 0), shift)
    c_blocks = jnp.concatenate([jnp.where(row_seq == s, cg, 0.0) for s in range(ns)], axis=1).astype(BF16)
    h_all = h0_ref[...].reshape(ns * n, gw)
    yoff = jnp.dot(c_blocks, h_all.astype(BF16), preferred_element_type=F32)
    y_parts, xw_parts, dec_parts = [], [], []
    for pp in range(hpg // 2):
        h0 = 2 * pp
        lanes = slice(h0 * M2_HEAD_DIM, (h0 + 2) * M2_HEAD_DIM)
        w_pair = []
        for h in (h0, h0 + 1):
            seg = acs[:, h:h + 1] - acs_t[h:h + 1, :]
            dec = jnp.exp(jnp.where(causal, seg, -jnp.inf))
            w_pair.append((cb * dec * dt_t[h:h + 1, :]).astype(BF16))
        xp = xs[:, lanes]
        xbd = jnp.concatenate([jnp.where(low, xp, 0.0), jnp.where(low, 0.0, xp)], axis=0).astype(BF16)
        yd = jnp.dot(jnp.concatenate(w_pair, axis=1), xbd, preferred_element_type=F32)
        y_parts.append(yd + yoff[:, lanes] * pair_cols(eacs, h0) + dexp_ref[0, :, lanes] * xp)
        xw_parts.append(xp * pair_cols(wdec, h0))
        dec_parts.append(pair_cols(dec_tot, h0))
    xw = jnp.concatenate(xw_parts, axis=1).astype(BF16)
    col_seq = lax.shift_right_logical(lax.broadcasted_iota(jnp.int32, (n, q), 1), shift)
    bgt = bg.T
    b_blocks = jnp.concatenate([jnp.where(col_seq == s, bgt, 0.0) for s in range(ns)], axis=0).astype(BF16)
    st = jnp.dot(b_blocks, xw, preferred_element_type=F32)
    dec_rows = jnp.concatenate(dec_parts, axis=1)
    for s in range(ns):
        hn_ref[s] = h0_ref[s] * dec_rows[s * seq_len:s * seq_len + 1, :] + st[s * n:(s + 1) * n, :]

    zz = z_ref[...]
    y = jnp.concatenate(y_parts, axis=1) * (zz * _sigmoid(zz))
    y_ref[...] = y * lax.rsqrt(jnp.mean(y * y, axis=-1, keepdims=True) + EPS) * nw_ref[0]


def _by_group(a):
    gw = M2_WIDTH // M2_GROUPS
    n = M2_D_STATE
    return jnp.stack([jnp.concatenate([a[..., g * gw:(g + 1) * gw],
                                       a[..., M2_WIDTH + g * n:M2_WIDTH + (g + 1) * n],
                                       a[..., M2_WIDTH + (M2_GROUPS + g) * n:M2_WIDTH + (M2_GROUPS + g + 1) * n]],
                                      axis=-1) for g in range(M2_GROUPS)])


def _ssd_packed(xbc, z, dt, conv0, h0, prm, n_seq, seq_len):
    cw, cb, dtb, alog, dexp, nw = prm
    t = xbc.shape[0]
    hpg = M2_HEADS // M2_GROUPS
    gw = hpg * M2_HEAD_DIM
    cdim = gw + 2 * M2_D_STATE
    heads = lambda a: jnp.stack([jnp.pad(a[..., g * hpg:(g + 1) * hpg], [(0, 0)] * (a.ndim - 1) + [(0, LANES - hpg)])
                                 for g in range(M2_GROUPS)])
    halves = lambda a: jnp.stack([a[..., g * gw:(g + 1) * gw] for g in range(M2_GROUPS)])
    hist = jnp.stack([_by_group(jnp.pad(conv0[:, M2_CONV - 1 - k:, :], ((0, 0), (0, seq_len - k), (0, 0)))
                                .reshape(t, M2_CONV_DIM)) for k in range(1, M2_CONV)])
    per_g = lambda r, c: pl.BlockSpec((1, r, c), lambda i, g: (g, 0, 0))
    rows_g = lambda c: pl.BlockSpec((1, SSD_CHUNK, c), lambda i, g: (g, i, 0))
    ns = SSD_CHUNK // seq_len
    state = pl.BlockSpec((ns, M2_D_STATE, gw), lambda i, g: (i, 0, g))
    y, hn = pl.pallas_call(
        functools.partial(_ssd_packed_kernel, seq_len=seq_len),
        grid=(t // SSD_CHUNK, M2_GROUPS),
        in_specs=[rows_g(cdim),
                  pl.BlockSpec((M2_CONV - 1, 1, SSD_CHUNK, cdim), lambda i, g: (0, g, i, 0)),
                  pl.BlockSpec((SSD_CHUNK, gw), lambda i, g: (i, g)),
                  rows_g(LANES), state,
                  per_g(M2_CONV, cdim), per_g(1, cdim), per_g(1, LANES), per_g(1, LANES), per_g(1, gw), per_g(1, gw)],
        out_specs=[pl.BlockSpec((SSD_CHUNK, gw), lambda i, g: (i, g)), state],
        out_shape=[jax.ShapeDtypeStruct((t, M2_WIDTH), F32), jax.ShapeDtypeStruct((n_seq, M2_D_STATE, M2_WIDTH), F32)],
        scratch_shapes=[pltpu.VMEM((CONV_HALO + SSD_CHUNK, cdim), F32)],
        compiler_params=_cparams("parallel", "parallel"),
        name="ssd_packed",
    )(_by_group(xbc), hist, z, heads(dt), h0, _by_group(cw), _by_group(cb), heads(dtb), heads(alog),
      halves(dexp), halves(nw))
    conv_new = xbc.reshape(n_seq, seq_len, M2_CONV_DIM)[:, seq_len - (M2_CONV - 1):, :]
    return y, conv_new, hn


def _post_kernel(y5_ref, u_ref, ym_ref, x_ref, d_ref, wglu_ref, s5nw_ref, wo1_ref, wo2_ref, fnw_ref, wqt_ref,
                 sk_ref, x1_ref, xt_ref, st_ref):
    g = _gelu_tanh(_load_blocked(y5_ref) + d_ref[...] * _load_blocked(u_ref))
    o = g * _sigmoid(jnp.dot(g.astype(BF16), wglu_ref[...], preferred_element_type=F32))
    y5n = _rms(o, s5nw_ref[...])
    x1 = (x_ref[...] + jnp.dot(y5n.astype(BF16), wo1_ref[...], preferred_element_type=F32)
          + jnp.dot(ym_ref[...].astype(BF16), wo2_ref[...], preferred_element_type=F32))
    x1_ref[...] = x1
    hn_t = _rms(x1, fnw_ref[...]).T.astype(BF16)
    xt_ref[...] = hn_t
    q_t = jnp.dot(wqt_ref[...], hn_t, preferred_element_type=F32)
    for k in range(2 * PEER_HEADS):
        qk = q_t[k * PEER_HALF:(k + 1) * PEER_HALF, :].astype(BF16)
        st_ref[k] = jnp.dot(sk_ref[k], qk, preferred_element_type=F32) * LOG2E


def _post(y5, u, ym, x, prm, tm, s5_block=None):
    t = x.shape[0]
    row = lambda i: (i, 0)
    s5_spec = pl.BlockSpec((tm, D_MODEL), row) if s5_block is None else pl.BlockSpec(*s5_block)
    fix = lambda i: (0, 0)
    nk = 2 * PEER_HEADS
    once = functools.partial(pl.BlockSpec, pipeline_mode=pl.Buffered(1))
    return pl.pallas_call(
        _post_kernel,
        grid=(t // tm,),
        in_specs=[s5_spec, s5_spec, pl.BlockSpec((tm, D_MODEL), row), pl.BlockSpec((tm, D_MODEL), row)]
        + [pl.BlockSpec((1, D_MODEL), fix), once((S5_WIDTH, S5_WIDTH), fix), pl.BlockSpec((1, D_MODEL), fix),
           once((S5_WIDTH, D_MODEL), fix), once((M2_WIDTH, D_MODEL), fix),
           pl.BlockSpec((1, D_MODEL), fix), once((nk * PEER_HALF, D_MODEL), fix),
           once((nk, PEER_N_KEYS, PEER_HALF), lambda i: (0, 0, 0))],
        out_specs=[pl.BlockSpec((tm, D_MODEL), row), pl.BlockSpec((D_MODEL, tm), lambda i: (0, i)),
                   pl.BlockSpec((nk, PEER_N_KEYS, tm), lambda i: (0, 0, i))],
        out_shape=[jax.ShapeDtypeStruct((t, D_MODEL), F32), jax.ShapeDtypeStruct((D_MODEL, t), BF16),
                   jax.ShapeDtypeStruct((nk, PEER_N_KEYS, t), F32)],
        compiler_params=_cparams("parallel"),
        name="post",
    )(y5, u, ym, x, *prm)


def _sorting_network(n):
    pairs = []
    p = 1
    while p < n:
        k = p
        while k >= 1:
            for j in range(k % p, n - k, 2 * k):
                for i in range(min(k, n - j - k)):
                    if (i + j) // (2 * p) == (i + j + k) // (2 * p):
                        pairs.append((i + j, i + j + k))
            k //= 2
        p *= 2
    return pairs


SUBLANES = 8
_NET16 = _sorting_network(PEER_N_KEYS // SUBLANES)
LOG2E = math.log2(math.e)


def _top16(tiles):
    n = len(tiles)
    tiles = list(tiles)

    def exchange(i, j):
        hi, lo = tiles[i], tiles[j]
        if lo is None:
            return
        if hi is None:
            tiles[i], tiles[j] = lo, None
        else:
            tiles[i], tiles[j] = jnp.maximum(hi, lo), jnp.minimum(hi, lo)

    def larger(a, b):
        return b if a is None else a if b is None else jnp.maximum(a, b)

    for i, j in _NET16:
        exchange(i, j)
    shift = SUBLANES // 2
    while shift >= 1:
        other = [None if t is None else pltpu.roll(t, shift, axis=0) for t in tiles]
        tiles = [larger(tiles[i], other[n - 1 - i]) for i in range(n)]
        dist = n // 2
        while dist >= 1:
            for i in range(n):
                if i & dist == 0:
                    exchange(i, i + dist)
            dist //= 2
        shift //= 2
    return tiles


def _top_values(s):
    n = PEER_N_KEYS // SUBLANES
    tiles = _top16([s[i * SUBLANES:(i + 1) * SUBLANES, :] for i in range(n)])
    rows = lax.broadcasted_iota(jnp.int32, (PEER_TOPK, s.shape[1]), 0)
    out = jnp.zeros((PEER_TOPK, s.shape[1]), F32)
    for r in range(PEER_TOPK):
        out = jnp.where(rows == r, jnp.concatenate([tiles[r], tiles[r]], axis=0), out)
    return out


def _pair_candidates(v1, v2):
    r8 = lax.broadcasted_iota(jnp.int32, (SUBLANES, v1.shape[1]), 0)
    r16 = lax.broadcasted_iota(jnp.int32, (PEER_TOPK, v1.shape[1]), 0)
    neg = -jnp.inf
    lo2 = v2[0:SUBLANES, :]
    return jnp.concatenate([
        v1[0:1, :] + v2,
        v1[1:2, :] + lo2,
        jnp.where(r16 >= 2, v1 + v2[0:1, :], neg),
        jnp.where(r8 >= 2, v1[0:SUBLANES, :] + v2[1:2, :], neg),
        jnp.where((r8 >= 2) & (r8 <= 4), v1[2:3, :] + lo2, neg),
        jnp.where((r8 >= 2) & (r8 <= 3), v1[3:4, :] + lo2, neg),
        jnp.where(r8 == 2, v1[4:5, :] + lo2, neg),
    ], axis=0)


def _route_kernel(st_ref, s1n_ref, s2n_ref, thr_ref):
    def head(h, carry):
        s1 = st_ref[2 * h]
        s2 = st_ref[2 * h + 1]
        v1 = _top_values(s1)
        v2 = _top_values(s2)
        cand = _pair_candidates(v1, v2)
        n_cand = cand.shape[0] // SUBLANES
        pad = [None] * (PEER_N_KEYS // SUBLANES - n_cand)
        ranked = _top16([cand[i * SUBLANES:(i + 1) * SUBLANES, :] for i in range(n_cand)] + pad)
        theta = ranked[PEER_TOPK - 1][0:1, :]
        sel = cand >= theta
        m = v1[0:1, :] + v2[0:1, :]
        zsum = jnp.sum(jnp.where(sel, jnp.exp2(cand - m), 0.0), axis=0, keepdims=True)
        off = m + jnp.log(zsum) * LOG2E + 1.0
        s1n_ref[h] = s1 - off
        s2n_ref[h] = s2
        candn = _pair_candidates(v1 - off, v2)
        thr_ref[h] = jnp.min(jnp.where(sel, candn, jnp.inf), axis=0, keepdims=True)
        return carry

    lax.fori_loop(0, PEER_HEADS, head, 0)


def _route(st, tl):
    nk, keys, t = st.shape
    spec = lambda n: pl.BlockSpec((n, keys, tl), lambda i: (0, 0, i))
    return pl.pallas_call(
        _route_kernel,
        grid=(t // tl,),
        in_specs=[spec(nk)],
        out_specs=[spec(PEER_HEADS), spec(PEER_HEADS), pl.BlockSpec((PEER_HEADS, 1, tl), lambda i: (0, 0, i))],
        out_shape=[jax.ShapeDtypeStruct((PEER_HEADS, keys, t), F32), jax.ShapeDtypeStruct((PEER_HEADS, keys, t), F32),
                   jax.ShapeDtypeStruct((PEER_HEADS, 1, t), F32)],
        compiler_params=_cparams("parallel"),
        name="route",
    )(st)


PEER_E1_BLK = 16
PEER_CHUNK_KEYS = (4, 4, 4, 4)
assert sum(PEER_CHUNK_KEYS) == PEER_E1_BLK


def _peer_kernel(u_ref, vt_ref, xt_ref, s2n_ref, s1n_ref, thr_ref, x1_ref, fnw_ref, y_ref, acc_scr, *, tm):
    k = pl.program_id(1)
    c0 = math.sqrt(2.0 / math.pi)
    c1 = c0 * 0.044715
    first_key = [sum(PEER_CHUNK_KEYS[:q]) for q in range(len(PEER_CHUNK_KEYS))]

    def experts(q):
        return slice(first_key[q] * PEER_N_KEYS, (first_key[q] + PEER_CHUNK_KEYS[q]) * PEER_N_KEYS)

    def scores(q):
        return jnp.dot(u_ref[experts(q), :], xt_ref[...], preferred_element_type=F32)

    def gated(q, a):
        wg_rows = []
        for i in range(PEER_CHUNK_KEYS[q]):
            r = first_key[q] + i
            rows = slice(i * PEER_N_KEYS, (i + 1) * PEER_N_KEYS)
            wg_cols = []
            for c in range(tm // LANES):
                cols = slice(c * LANES, (c + 1) * LANES)
                w = None
                for h in range(PEER_HEADS):
                    arg = s2n_ref[h, :, cols] + s1n_ref[h, r:r + 1, cols]
                    term = jnp.where(arg >= thr_ref[h, :, cols], jnp.exp2(arg), 0.0)
                    w = term if w is None else w + term
                x = a[rows, cols]
                g = x + x * jnp.tanh(x * (c0 + c1 * (x * x)))
                wg_cols.append((w * g).astype(BF16))
            wg_rows.append(jnp.concatenate(wg_cols, axis=1))
        return jnp.concatenate(wg_rows, axis=0)

    def mixed(q, wg):
        return jnp.dot(vt_ref[0, :, experts(q)], wg, preferred_element_type=F32)

    n_chunks = len(PEER_CHUNK_KEYS)
    total = None
    a_next = scores(0)
    wg_prev = None
    for q in range(n_chunks):
        a_cur = a_next
        if q + 1 < n_chunks:
            a_next = scores(q + 1)
        wg = gated(q, a_cur)
        if wg_prev is not None:
            d = mixed(q - 1, wg_prev)
            total = d if total is None else total + d
        wg_prev = wg
    total = total + mixed(n_chunks - 1, wg_prev)

    @pl.when(k == 0)
    def _():
        acc_scr[...] = total

    @pl.when(k > 0)
    def _():
        acc_scr[...] += total

    @pl.when(k == pl.num_programs(1) - 1)
    def _():
        y_ref[...] = _rms(x1_ref[...] + acc_scr[...].T, fnw_ref[...])


def _peer(u_bf, vt_bf, xt, route, x1, fnw, tm):
    s1n, s2n, thr = route
    t = x1.shape[0]
    eb = PEER_E1_BLK * PEER_N_KEYS
    per_key = pl.BlockSpec((PEER_HEADS, PEER_N_KEYS, tm), lambda j, k: (0, 0, j))
    per_blk = pl.BlockSpec((PEER_HEADS, PEER_E1_BLK, tm), lambda j, k: (0, k, j))
    return pl.pallas_call(
        functools.partial(_peer_kernel, tm=tm),
        grid=(t // tm, PEER_N_KEYS // PEER_E1_BLK),
        in_specs=[pl.BlockSpec((eb, D_MODEL), lambda j, k: (k, 0)),
                  pl.BlockSpec((1, D_MODEL, eb), lambda j, k: (k, 0, 0)),
                  pl.BlockSpec((D_MODEL, tm), lambda j, k: (0, j)),
                  per_key, per_blk, pl.BlockSpec((PEER_HEADS, 1, tm), lambda j, k: (0, 0, j)),
                  pl.BlockSpec((tm, D_MODEL), lambda j, k: (j, 0)),
                  pl.BlockSpec((1, D_MODEL), lambda j, k: (0, 0))],
        out_specs=pl.BlockSpec((tm, D_MODEL), lambda j, k: (j, 0)),
        out_shape=jax.ShapeDtypeStruct((t, D_MODEL), F32),
        scratch_shapes=[pltpu.VMEM((D_MODEL, tm), F32)],
        compiler_params=_cparams("parallel", "arbitrary"),
        name="peer",
    )(u_bf, vt_bf, xt, s2n, s1n, thr, x1, fnw)


TOKEN_TILE = 512


def _tokens_tail(x, post_prm, peer_prm, y5, u, ym, s5_block):
    x1, xt, st = _post(y5, u, ym, x, post_prm, tm=TOKEN_TILE, s5_block=s5_block)
    u_bf, vt_bf, fnw = peer_prm
    return _peer(u_bf, vt_bf, xt, _route(st, tl=LANES), x1, fnw, tm=512)


def kernel(x_prompt, x_sample, state_s5_re, state_s5_im, state_ssm, state_conv, meta_tokens, norm_mix_w, w_in,
           s5_lambda_re, s5_lambda_im, s5_log_step, s5_b_re, s5_b_im, s5_c_re, s5_c_im, s5_d, s5_w_glu, s5_norm_w,
           m2_conv_w, m2_conv_b, m2_dt_bias, m2_a_log, m2_d, m2_norm_w, w_out, norm_ffn_w, peer_w_q, peer_sub_keys,
           peer_u, peer_v, final_norm_w):
    bp, sp, _ = x_prompt.shape
    bs, ss, _ = x_sample.shape
    g, p = S5_GROUPS, S5_STATE

    w = w_in[0]
    o1, o2, o3 = S5_WIDTH, S5_WIDTH + M2_WIDTH, S5_WIDTH + M2_WIDTH + M2_CONV_DIM
    wu, wz, wx = w[:, :o1].astype(BF16), w[:, o1:o2].astype(BF16), w[:, o2:o3].astype(BF16)
    wd = jnp.pad(w[:, o3:], ((0, 0), (0, LANES - M2_HEADS))).astype(BF16)
    nmw = norm_mix_w[0][None, :]
    pad_h = lambda v: jnp.pad(v, (0, LANES - M2_HEADS))[None, :]
    ssd_prm = (m2_conv_w[0], m2_conv_b[0][None, :], pad_h(m2_dt_bias[0]), pad_h(m2_a_log[0]),
               jnp.repeat(m2_d[0], M2_HEAD_DIM)[None, :], m2_norm_w[0][None, :])
    post_prm = (s5_d[0][None, :], s5_w_glu[0].astype(BF16), s5_norm_w[0][None, :],
                w_out[0][:S5_WIDTH].astype(BF16), w_out[0][S5_WIDTH:].astype(BF16), norm_ffn_w[0][None, :],
                peer_w_q[0].T.astype(BF16),
                peer_sub_keys[0].reshape(2 * PEER_HEADS, PEER_N_KEYS, PEER_HALF).astype(BF16))
    eb = PEER_E1_BLK * PEER_N_KEYS
    vt_blocks = peer_v[0].astype(BF16).reshape(PEER_EXPERTS // eb, eb, D_MODEL).swapaxes(1, 2)
    peer_prm = (peer_u[0].astype(BF16), vt_blocks, final_norm_w[None, :])
    s5_args = (s5_lambda_re[0], s5_lambda_im[0], s5_log_step[0], s5_b_re[0], s5_b_im[0], s5_c_re[0], s5_c_im[0])
    ops16 = _s5_prep(*s5_args, lc=N_META)
    ops8 = _s5_prep(*s5_args, lc=ss)

    lc = N_META
    um, _, xbcm, dtm = _in_proj(meta_tokens, nmw, wu, wz, wx, wd, tm=N_META)
    nb = S5_WIDTH // LANES
    urm = jnp.broadcast_to(um.reshape(lc, nb, LANES).transpose(1, 0, 2).reshape(nb, 1, lc * LANES),
                           (nb, SUBLANES, lc * LANES))
    _, h5m = _s5(urm, jnp.zeros((g, SUBLANES, 2 * p), F32), ops16, lc=lc, rows=SUBLANES, chunks=1)
    _, convm, hm = _ssd(xbcm, jnp.zeros((N_META, M2_WIDTH), F32), dtm,
                        jnp.zeros((1, M2_CONV - 1, M2_CONV_DIM), F32), jnp.zeros((1, M2_D_STATE, M2_WIDTH), F32),
                        ssd_prm, n_seq=1, n_chunks=1, q_in=N_META, shared_init=True)

    xp = x_prompt.reshape(bp * sp, D_MODEL)
    nc = sp // lc
    cpt = TOKEN_TILE // lc
    tps = sp // TOKEN_TILE
    p_shape = (nb, nc, bp, lc, LANES)
    p_block = ((nb, cpt, 1, lc, LANES), lambda i: (0, i % tps, i // tps, 0, 0))
    up, zp, xbcp, dtp = _in_proj(xp, nmw, wu, wz, wx, wd, tm=TOKEN_TILE, u_layout=(p_shape,) + p_block)
    y5p, h5p = _s5(up.reshape(nb, nc * bp, lc * LANES), h5m, ops16, lc=lc, rows=bp, chunks=S5_ROW_TILE // bp)
    ymp, convp, hp = _ssd(xbcp, zp, dtp, convm, hm, ssd_prm, n_seq=bp, n_chunks=sp // SSD_CHUNK, q_in=SSD_CHUNK,
                          shared_init=True)
    y_prompt = _tokens_tail(xp, post_prm, peer_prm, y5p.reshape(p_shape), up, ymp, p_block).reshape(bp, sp, D_MODEL)

    xs = x_sample.reshape(bs * ss, D_MODEL)
    s_shape = (nb, bs, ss, LANES)
    s_block = ((nb, TOKEN_TILE // ss, ss, LANES), lambda i: (0, i, 0, 0))
    us, zs, xbcs, dts = _in_proj(xs, nmw, wu, wz, wx, wd, tm=TOKEN_TILE, u_layout=(s_shape,) + s_block)
    h5s0 = jnp.concatenate([state_s5_re[0], state_s5_im[0]], axis=-1).transpose(1, 0, 2)
    y5s, h5s = _s5(us.reshape(nb, bs, ss * LANES), h5s0, ops8, lc=ss, rows=bs, chunks=1)
    hs0 = state_ssm[0].reshape(bs, M2_WIDTH, M2_D_STATE).transpose(0, 2, 1)
    yms, convs, hs = _ssd_packed(xbcs, zs, dts, state_conv[0], hs0, ssd_prm, n_seq=bs, seq_len=ss)
    y_sample = _tokens_tail(xs, post_prm, peer_prm, y5s.reshape(s_shape), us, yms, s_block).reshape(bs, ss, D_MODEL)

    def s5_state(hf):
        hf = hf.transpose(1, 0, 2)
        return hf[None, :, :, :p], hf[None, :, :, p:]

    def ssm_state(ht):
        return ht.transpose(0, 2, 1).reshape(1, ht.shape[0], M2_HEADS, M2_HEAD_DIM, M2_D_STATE)

    p5r, p5i = s5_state(h5p)
    s5r, s5i = s5_state(h5s)
    return (y_prompt, y_sample, p5r, p5i, ssm_state(hp), convp[None], s5r, s5i, ssm_state(hs), convs[None])
```

```python
import functools
import math

import jax
import jax.numpy as jnp
from jax import lax
from jax.experimental import pallas as pl
from jax.experimental.pallas import tpu as pltpu

F32 = jnp.float32
BF16 = jnp.bfloat16
HIGHEST = lax.Precision.HIGHEST

D_MODEL = 1024
N_META = 16
S5_WIDTH = 1024
S5_CH = 16
S5_GROUPS = S5_WIDTH // S5_CH
S5_STATE = 64
M2_WIDTH = 1024
M2_HEAD_DIM = 64
M2_HEADS = M2_WIDTH // M2_HEAD_DIM
M2_GROUPS = 2
M2_D_STATE = 128
M2_CONV = 4
M2_CONV_DIM = M2_WIDTH + 2 * M2_GROUPS * M2_D_STATE
PEER_HEADS = 8
PEER_N_KEYS = 128
PEER_EXPERTS = PEER_N_KEYS * PEER_N_KEYS
PEER_HALF = 128
PEER_TOPK = 16
EPS = 1e-6

LANES = 128
SSD_CHUNK = 128
CONV_HALO = 8
VMEM_LIMIT = 56 * 1024 * 1024


def _cparams(*sem):
    return pltpu.CompilerParams(dimension_semantics=sem, vmem_limit_bytes=VMEM_LIMIT)


def _sigmoid(x):
    return 1.0 / (1.0 + jnp.exp(-x))


def _gelu_tanh(x):
    c = math.sqrt(2.0 / math.pi)
    return 0.5 * x * (1.0 + jnp.tanh(c * (x + 0.044715 * (x * x * x))))


def _rms(x, w):
    return x * lax.rsqrt(jnp.mean(x * x, axis=-1, keepdims=True) + EPS) * w


def _bdot(a, b):
    return jnp.dot(a.astype(BF16), b.astype(BF16), preferred_element_type=F32)


def _store_blocked(ref, val):
    if len(ref.shape) == 2:
        ref[...] = val
        return
    for gb in range(ref.shape[0]):
        piece = val[:, gb * LANES:(gb + 1) * LANES]
        if len(ref.shape) == 5:
            ref[gb, :, 0] = piece.reshape(ref.shape[1], ref.shape[3], LANES)
        else:
            ref[gb] = piece.reshape(ref.shape[1:])


def _load_blocked(ref):
    if len(ref.shape) == 2:
        return ref[...]
    parts = []
    for gb in range(ref.shape[0]):
        piece = ref[gb, :, 0] if len(ref.shape) == 5 else ref[gb]
        parts.append(piece.reshape(piece.shape[0] * piece.shape[1], LANES))
    return jnp.concatenate(parts, axis=1)


def _in_proj_kernel(x_ref, nw_ref, wu_ref, wz_ref, wx_ref, wd_ref, u_ref, z_ref, xbc_ref, dt_ref):
    hb = _rms(x_ref[...], nw_ref[...]).astype(BF16)
    _store_blocked(u_ref, jnp.dot(hb, wu_ref[...], preferred_element_type=F32))
    z_ref[...] = jnp.dot(hb, wz_ref[...], preferred_element_type=F32)
    xbc_ref[...] = jnp.dot(hb, wx_ref[...], preferred_element_type=F32)
    dt_ref[...] = jnp.dot(hb, wd_ref[...], preferred_element_type=F32)


def _in_proj(x, nw, wu, wz, wx, wd, tm, u_layout=None):
    t = x.shape[0]
    row = lambda i: (i, 0)
    fix = lambda i: (0, 0)
    widths = (S5_WIDTH, M2_WIDTH, M2_CONV_DIM, LANES)
    out_specs = [pl.BlockSpec((tm, w), row) for w in widths]
    out_shape = [jax.ShapeDtypeStruct((t, w), F32) for w in widths]
    if u_layout is not None:
        out_specs[0] = pl.BlockSpec(u_layout[1], u_layout[2])
        out_shape[0] = jax.ShapeDtypeStruct(u_layout[0], F32)
    return pl.pallas_call(
        _in_proj_kernel,
        grid=(t // tm,),
        in_specs=[pl.BlockSpec((tm, D_MODEL), row), pl.BlockSpec((1, D_MODEL), fix)]
        + [pl.BlockSpec((D_MODEL, w), fix) for w in widths],
        out_specs=out_specs,
        out_shape=out_shape,
        compiler_params=_cparams("parallel"),
        name="in_proj",
    )(x, nw, wu, wz, wx, wd)


def _s5_prep_kernel(lrc_ref, lic_ref, lrr_ref, lir_ref, ls_ref, btr_ref, bti_ref, ctr_ref, cti_ref,
                    kt_ref, wout_ref, wsr_ref, wsi_ref, al_ref, *, lc):
    for j in range(S5_GBLK):
        _s5_prep_group(j, lc, lrc_ref, lic_ref, lrr_ref, lir_ref, ls_ref, btr_ref, bti_ref, ctr_ref, cti_ref,
                       kt_ref, wout_ref, wsr_ref, wsi_ref, al_ref)


def _s5_prep_group(j, lc, lrc_ref, lic_ref, lrr_ref, lir_ref, ls_ref, btr_ref, bti_ref, ctr_ref, cti_ref,
                   kt_ref, wout_ref, wsr_ref, wsi_ref, al_ref):
    k = lc * S5_CH
    step = jnp.exp(ls_ref[j])

    def disc(lr, li):
        lg = lr * step
        th = li * step
        mag = jnp.exp(lg)
        ab_re = mag * jnp.cos(th)
        ab_im = mag * jnp.sin(th)
        den = lr * lr + li * li
        f_re = ((ab_re - 1.0) * lr + ab_im * li) / den
        f_im = (ab_im * lr - (ab_re - 1.0) * li) / den
        return ab_re, ab_im, f_re, f_im

    def powers(n, a_re_, a_im_):
        p_re_ = jnp.ones(n.shape, F32)
        p_im_ = jnp.zeros(n.shape, F32)
        b_re_, b_im_ = a_re_, a_im_
        bit = 1
        while bit <= lc:
            use = (n & bit) != 0
            t_re = p_re_ * b_re_ - p_im_ * b_im_
            t_im = p_re_ * b_im_ + p_im_ * b_re_
            p_re_ = jnp.where(use, t_re, p_re_)
            p_im_ = jnp.where(use, t_im, p_im_)
            b_re_, b_im_ = b_re_ * b_re_ - b_im_ * b_im_, 2.0 * (b_re_ * b_im_)
            bit *= 2
        return p_re_, p_im_

    ac_re, ac_im, f_re, f_im = disc(lrc_ref[j], lic_ref[j])
    tau = lax.shift_right_logical(lax.broadcasted_iota(jnp.int32, (S5_STATE, k), 1), 4)
    c_re = ctr_ref[j]
    c_im = cti_ref[j]
    p_re, p_im = powers(tau, ac_re, ac_im)
    fa_re = f_re * p_re - f_im * p_im
    fa_im = f_re * p_im + f_im * p_re
    fca_re = fa_re * c_re - fa_im * c_im
    fca_im = fa_re * c_im + fa_im * c_re
    kt_ref[j] = (jnp.dot(btr_ref[j], fca_re, precision=HIGHEST, preferred_element_type=F32)
                 - jnp.dot(bti_ref[j], fca_im, precision=HIGHEST, preferred_element_type=F32))
    q_re, q_im = powers(tau + 1, ac_re, ac_im)
    wout_ref[j, 0:S5_STATE, :] = q_re * c_re - q_im * c_im
    wout_ref[j, S5_STATE:2 * S5_STATE, :] = -(q_re * c_im + q_im * c_re)

    ar_re, ar_im, fr_re, fr_im = disc(lrr_ref[j], lir_ref[j])
    kk = lax.broadcasted_iota(jnp.int32, (lc, S5_STATE), 0)
    r_re, r_im = powers(kk, ar_re, ar_im)
    g_re = fr_re * r_re - fr_im * r_im
    g_im = fr_re * r_im + fr_im * r_re
    bt_re = btr_ref[j]
    bt_im = bti_ref[j]
    for s in range(lc):
        kpow = lc - 1 - s
        w_re = g_re[kpow:kpow + 1, :]
        w_im = g_im[kpow:kpow + 1, :]
        wsr_ref[j, s * S5_CH:(s + 1) * S5_CH, :] = bt_re * w_re - bt_im * w_im
        wsi_ref[j, s * S5_CH:(s + 1) * S5_CH, :] = bt_re * w_im + bt_im * w_re
    a_re, a_im = powers(jnp.full((1, S5_STATE), lc, jnp.int32), ar_re, ar_im)
    al_ref[j, 0:1, :] = a_re
    al_ref[j, 1:2, :] = a_im


def _s5_prep(lam_re, lam_im, log_step, b_re, b_im, c_re, c_im, lc):
    g, p, k = S5_GROUPS, S5_STATE, lc * S5_CH
    spec3 = lambda a, b: pl.BlockSpec((S5_GBLK, a, b), lambda i: (i, 0, 0))
    ins = [lam_re.reshape(g, p, 1), lam_im.reshape(g, p, 1), lam_re.reshape(g, 1, p), lam_im.reshape(g, 1, p),
           log_step.reshape(g, 1, 1),
           jnp.swapaxes(b_re, 1, 2), jnp.swapaxes(b_im, 1, 2),
           jnp.tile(jnp.swapaxes(c_re, 1, 2), (1, 1, lc)), jnp.tile(jnp.swapaxes(c_im, 1, 2), (1, 1, lc))]
    kt, wout, wsr, wsi, al = pl.pallas_call(
        functools.partial(_s5_prep_kernel, lc=lc),
        grid=(g // S5_GBLK,),
        in_specs=[spec3(p, 1), spec3(p, 1), spec3(1, p), spec3(1, p), spec3(1, 1),
                  spec3(S5_CH, p), spec3(S5_CH, p), spec3(p, k), spec3(p, k)],
        out_specs=[spec3(S5_CH, k), spec3(2 * p, k), spec3(k, p), spec3(k, p), spec3(2, p)],
        out_shape=[jax.ShapeDtypeStruct((g, S5_CH, k), F32), jax.ShapeDtypeStruct((g, 2 * p, k), F32),
                   jax.ShapeDtypeStruct((g, k, p), F32), jax.ShapeDtypeStruct((g, k, p), F32),
                   jax.ShapeDtypeStruct((g, 2, p), F32)],
        compiler_params=_cparams("parallel"),
        name="s5_prep",
    )(*ins)
    lag = jnp.arange(lc)[None, :] - jnp.arange(lc)[:, None]
    kt4 = kt.reshape(g, S5_CH, lc, S5_CH)
    toep = kt4[:, :, jnp.maximum(lag, 0), :]
    toep = jnp.where((lag >= 0)[None, None, :, :, None], toep, 0.0)
    toep = toep.transpose(0, 2, 1, 3, 4).reshape(g, k, k)
    wst = jnp.concatenate([wsr, wsi, wsi, wsr], axis=-1)
    a_re, a_im = al[:, 0:1, :], al[:, 1:2, :]
    acoef = jnp.concatenate([jnp.concatenate([a_re, a_re], -1), jnp.concatenate([-a_im, a_im], -1),
                             jnp.concatenate([a_im, -a_im], -1)], axis=1)
    return toep.astype(BF16), wst.astype(BF16), wout.astype(BF16), acoef


S5_GBLK = LANES // S5_CH
S5_ROW_TILE = 256


def _block_transpose(cols):
    n = len(cols)
    blk = lax.shift_right_logical(lax.broadcasted_iota(jnp.int32, (1, LANES), 1), 4)
    rolled = []
    for d in range(n):
        m = cols[d]
        for b in range(1, n):
            m = jnp.where(blk == b, cols[(b + d) % n], m)
        rolled.append(m if d == 0 else pltpu.roll(m, d * S5_CH, axis=1))
    outs = []
    for j in range(n):
        o = rolled[(-j) % n]
        for s in range(1, n):
            o = jnp.where(blk == s, rolled[(s - j) % n], o)
        outs.append(o)
    return outs


def _s5_kernel(u_ref, h0_ref, h0s_ref, wst_ref, toep_ref, wout_ref, a_ref, y_ref, hf_ref,
               s_scr, hin_scr, h_scr, hs_scr, *, lc, rows, chunks):
    i = pl.program_id(1)

    @pl.when(i == 0)
    def _():
        h_scr[...] = h0_ref[...]
        hs_scr[...] = h0s_ref[...]

    x = u_ref[0]
    halves = [_block_transpose([x[:, (S5_GBLK * c + s) * LANES:(S5_GBLK * c + s + 1) * LANES] for s in range(S5_GBLK)])
              for c in range(lc // S5_GBLK)]
    ubs = []
    for j in range(S5_GBLK):
        ubs.append(jnp.concatenate([hv[j] for hv in halves], axis=1).astype(BF16))
        s_scr[j] = jnp.dot(ubs[j], wst_ref[j], preferred_element_type=F32)

    def step(r0, carry):
        new = []
        for j in range(S5_GBLK):
            h, hs = carry[2 * j], carry[2 * j + 1]
            hin_scr[j, pl.ds(r0, rows), :] = h
            s = s_scr[j, pl.ds(r0, rows), :]
            a1, a2, a2s = a_ref[j, 0:1, :], a_ref[j, 1:2, :], a_ref[j, 2:3, :]
            new.append(a1 * h + a2 * hs + s[:, :LANES])
            new.append(a1 * hs + a2s * h + s[:, LANES:])
        return tuple(new)

    carry = tuple(ref[j] for j in range(S5_GBLK) for ref in (h_scr, hs_scr))
    if chunks == 1:
        carry = step(0, carry)
    else:
        carry = lax.fori_loop(0, chunks, lambda n, c: step(pl.multiple_of(n * rows, rows), c), carry)
    for j in range(S5_GBLK):
        h_scr[j] = carry[2 * j]
        hs_scr[j] = carry[2 * j + 1]
    hf_ref[...] = h_scr[...]

    ys = [jnp.dot(ubs[j], toep_ref[j], preferred_element_type=F32)
          + jnp.dot(hin_scr[j].astype(BF16), wout_ref[j], preferred_element_type=F32) for j in range(S5_GBLK)]
    for c in range(lc // S5_GBLK):
        token_major = _block_transpose([ys[j][:, c * LANES:(c + 1) * LANES] for j in range(S5_GBLK)])
        for t in range(S5_GBLK):
            y_ref[0, :, (S5_GBLK * c + t) * LANES:(S5_GBLK * c + t + 1) * LANES] = token_major[t]


def _s5(ub, h0, ops, lc, rows, chunks):
    toep, wst, wout, acoef = ops
    nb, nr, width = ub.shape
    k = lc * S5_CH
    rt = rows * chunks
    blk = lambda a, b: pl.BlockSpec((S5_GBLK, a, b), lambda gb, i: (gb, 0, 0))
    return pl.pallas_call(
        functools.partial(_s5_kernel, lc=lc, rows=rows, chunks=chunks),
        grid=(nb, nr // rt),
        in_specs=[pl.BlockSpec((1, rt, width), lambda gb, i: (gb, i, 0)),
                  blk(rows, LANES), blk(rows, LANES), blk(k, 2 * LANES), blk(k, k), blk(LANES, k), blk(3, LANES)],
        out_specs=[pl.BlockSpec((1, rt, width), lambda gb, i: (gb, i, 0)), blk(rows, LANES)],
        out_shape=[jax.ShapeDtypeStruct((nb, nr, width), F32), jax.ShapeDtypeStruct((S5_GROUPS, rows, LANES), F32)],
        scratch_shapes=[pltpu.VMEM((S5_GBLK, rt, 2 * LANES), F32), pltpu.VMEM((S5_GBLK, rt, LANES), F32),
                        pltpu.VMEM((S5_GBLK, rows, LANES), F32), pltpu.VMEM((S5_GBLK, rows, LANES), F32)],
        compiler_params=_cparams("parallel", "arbitrary"),
        name="s5_scan",
    )(ub, h0, jnp.roll(h0, S5_STATE, axis=-1), wst, toep, wout, acoef)


def _ssd_kernel(xbc_ref, z_ref, dt_ref, conv0_ref, h0_ref, cw_ref, cb_ref, dtb_ref, alog_ref, dexp_ref, nw_ref,
                y_ref, convn_ref, hn_ref, buf_scr, dt_scr, h_scr, *, q_in, n_chunks):
    q = SSD_CHUNK
    pad = q - q_in
    first = CONV_HALO + pad
    c = pl.program_id(1)

    @pl.when(c == 0)
    def _():
        buf_scr[0:first, :] = jnp.zeros((first, M2_CONV_DIM), F32)
        buf_scr[first - (M2_CONV - 1):first, :] = conv0_ref[0]
        h_scr[...] = h0_ref[0]

    buf_scr[first:CONV_HALO + q, :] = xbc_ref[...]
    conv = cb_ref[...]
    for kk in range(M2_CONV):
        lo = CONV_HALO - (M2_CONV - 1) + kk
        conv = conv + cw_ref[kk:kk + 1, :] * buf_scr[lo:lo + q, :]
    convn_ref[0] = buf_scr[CONV_HALO + q - (M2_CONV - 1):CONV_HALO + q, :]
    if n_chunks > 1:
        buf_scr[0:CONV_HALO, :] = buf_scr[q:q + CONV_HALO, :]
    act = conv * _sigmoid(conv)
    xs = act[:, :M2_WIDTH]

    dtv = dt_ref[...] + dtb_ref[...]
    dt_real = jnp.maximum(dtv, 0.0) + jnp.log1p(jnp.exp(-jnp.abs(dtv)))
    if pad:
        dt_scr[0:pad, :] = jnp.zeros((pad, LANES), F32)
        dt_scr[pad:q, :] = dt_real
        dt = dt_scr[...]
    else:
        dt = dt_real
    a_neg = -jnp.exp(alog_ref[...])
    ri = lax.broadcasted_iota(jnp.int32, (q, q), 0)
    ci = lax.broadcasted_iota(jnp.int32, (q, q), 1)
    tril = ri >= ci
    acs = jnp.dot(tril.astype(F32), dt * a_neg, precision=HIGHEST, preferred_element_type=F32)
    acs_t = acs.T
    dt_t = dt.T
    acs_last = acs[q - 1:q, :]
    eacs = jnp.exp(acs)
    wdec = jnp.exp(acs_last - acs) * dt
    dec_last = jnp.exp(acs_last)
    low = lax.broadcasted_iota(jnp.int32, (q, LANES), 1) < M2_HEAD_DIM
    low1 = low[0:1, :]

    def pair_cols(m, h0):
        return jnp.where(low[0:m.shape[0], :], m[:, h0:h0 + 1], m[:, h0 + 1:h0 + 2])

    y_parts = []
    hpg = M2_HEADS // M2_GROUPS
    gw = hpg * M2_HEAD_DIM
    for g in range(M2_GROUPS):
        bg = act[:, M2_WIDTH + g * M2_D_STATE:M2_WIDTH + (g + 1) * M2_D_STATE]
        cg = act[:, M2_WIDTH + (M2_GROUPS + g) * M2_D_STATE:M2_WIDTH + (M2_GROUPS + g + 1) * M2_D_STATE]
        cgb = cg.astype(BF16)
        cb = lax.dot_general(cgb, bg.astype(BF16), (((1,), (1,)), ((), ())), preferred_element_type=F32)
        hg = h_scr[:, g * gw:(g + 1) * gw]
        yoff = jnp.dot(cgb, hg.astype(BF16), preferred_element_type=F32)
        xw_parts = []
        dec_parts = []
        for pp in range(hpg // 2):
            h0 = g * hpg + 2 * pp
            lanes = slice(h0 * M2_HEAD_DIM, (h0 + 2) * M2_HEAD_DIM)
            w_pair = []
            for h in (h0, h0 + 1):
                seg = acs[:, h:h + 1] - acs_t[h:h + 1, :]
                dec = jnp.exp(jnp.where(tril, seg, -jnp.inf))
                w_pair.append((cb * dec * dt_t[h:h + 1, :]).astype(BF16))
            xp = xs[:, lanes]
            xbd = jnp.concatenate([jnp.where(low, xp, 0.0), jnp.where(low, 0.0, xp)], axis=0).astype(BF16)
            yd = jnp.dot(jnp.concatenate(w_pair, axis=1), xbd, preferred_element_type=F32)
            yo = yoff[:, 2 * pp * M2_HEAD_DIM:(2 * pp + 2) * M2_HEAD_DIM]
            y_parts.append(yd + yo * pair_cols(eacs, h0) + dexp_ref[:, lanes] * xp)
            xw_parts.append(xp * pair_cols(wdec, h0))
            dec_parts.append(jnp.where(low1, dec_last[:, h0:h0 + 1], dec_last[:, h0 + 1:h0 + 2]))
        xw = jnp.concatenate(xw_parts, axis=1).astype(BF16)
        st = jnp.dot(bg.T.astype(BF16), xw, preferred_element_type=F32)
        h_scr[:, g * gw:(g + 1) * gw] = hg * jnp.concatenate(dec_parts, axis=1) + st

    @pl.when(c == n_chunks - 1)
    def _():
        hn_ref[0] = h_scr[...]

    y = jnp.concatenate(y_parts, axis=1)[pad:, :]
    zz = z_ref[...]
    y = y * (zz * _sigmoid(zz))
    outs = []
    for g in range(M2_GROUPS):
        yg = y[:, g * gw:(g + 1) * gw]
        outs.append(yg * lax.rsqrt(jnp.mean(yg * yg, axis=-1, keepdims=True) + EPS))
    y_ref[...] = jnp.concatenate(outs, axis=1) * nw_ref[...]


def _ssd(xbc, z, dt, conv0, h0, prm, n_seq, n_chunks, q_in, shared_init):
    assert q_in == SSD_CHUNK or n_chunks == 1
    assert q_in >= CONV_HALO and q_in % CONV_HALO == 0
    t = xbc.shape[0]
    row = lambda b, c: (b * n_chunks + c, 0)
    fix = lambda b, c: (0, 0)
    init = (lambda b, c: (0, 0, 0)) if shared_init else (lambda b, c: (b, 0, 0))
    per_seq = lambda b, c: (b, 0, 0)
    return pl.pallas_call(
        functools.partial(_ssd_kernel, q_in=q_in, n_chunks=n_chunks),
        grid=(n_seq, n_chunks),
        in_specs=[pl.BlockSpec((q_in, M2_CONV_DIM), row), pl.BlockSpec((q_in, M2_WIDTH), row),
                  pl.BlockSpec((q_in, LANES), row),
                  pl.BlockSpec((1, M2_CONV - 1, M2_CONV_DIM), init),
                  pl.BlockSpec((1, M2_D_STATE, M2_WIDTH), init),
                  pl.BlockSpec((M2_CONV, M2_CONV_DIM), fix), pl.BlockSpec((1, M2_CONV_DIM), fix),
                  pl.BlockSpec((1, LANES), fix), pl.BlockSpec((1, LANES), fix),
                  pl.BlockSpec((1, M2_WIDTH), fix), pl.BlockSpec((1, M2_WIDTH), fix)],
        out_specs=[pl.BlockSpec((q_in, M2_WIDTH), row),
                   pl.BlockSpec((1, M2_CONV - 1, M2_CONV_DIM), per_seq),
                   pl.BlockSpec((1, M2_D_STATE, M2_WIDTH), per_seq)],
        out_shape=[jax.ShapeDtypeStruct((t, M2_WIDTH), F32),
                   jax.ShapeDtypeStruct((n_seq, M2_CONV - 1, M2_CONV_DIM), F32),
                   jax.ShapeDtypeStruct((n_seq, M2_D_STATE, M2_WIDTH), F32)],
        scratch_shapes=[pltpu.VMEM((CONV_HALO + SSD_CHUNK, M2_CONV_DIM), F32),
                        pltpu.VMEM((SSD_CHUNK, LANES), F32),
                        pltpu.VMEM((M2_D_STATE, M2_WIDTH), F32)],
        compiler_params=_cparams("parallel", "arbitrary"),
        name="ssd",
    )(xbc, z, dt, conv0, h0, *prm)


def _ssd_packed_kernel(xbc_ref, hist_ref, z_ref, dt_ref, h0_ref, cw_ref, cb_ref, dtb_ref, alog_ref, dexp_ref, nw_ref,
                       y_ref, hn_ref, buf_scr, *, seq_len):
    q = SSD_CHUNK
    n = M2_D_STATE
    ns = q // seq_len
    shift = seq_len.bit_length() - 1
    hpg = M2_HEADS // M2_GROUPS
    gw = hpg * M2_HEAD_DIM
    cdim = gw + 2 * n
    x = xbc_ref[0]
    buf_scr[0:CONV_HALO, :] = jnp.zeros((CONV_HALO, cdim), F32)
    buf_scr[CONV_HALO:CONV_HALO + q, :] = x
    pos = lax.broadcasted_iota(jnp.int32, (q, cdim), 0) & (seq_len - 1)
    conv = cb_ref[0] + cw_ref[0, M2_CONV - 1:M2_CONV, :] * x
    for k in range(1, M2_CONV):
        prev = jnp.where(pos >= k, buf_scr[CONV_HALO - k:CONV_HALO - k + q, :], hist_ref[k - 1, 0])
        conv = conv + cw_ref[0, M2_CONV - 1 - k:M2_CONV - k, :] * prev
    act = conv * _sigmoid(conv)
    xs = act[:, :gw]
    bg = act[:, gw:gw + n]
    cg = act[:, gw + n:]

    dtv = dt_ref[0] + dtb_ref[0]
    dt = jnp.maximum(dtv, 0.0) + jnp.log1p(jnp.exp(-jnp.abs(dtv)))
    a_neg = -jnp.exp(alog_ref[0])
    ri = lax.broadcasted_iota(jnp.int32, (q, q), 0)
    ci = lax.broadcasted_iota(jnp.int32, (q, q), 1)
    same = lax.shift_right_logical(ri, shift) == lax.shift_right_logical(ci, shift)
    causal = (ri >= ci) & same
    dta = dt * a_neg
    acs = jnp.dot(causal.astype(F32), dta, precision=HIGHEST, preferred_element_type=F32)
    tot = jnp.dot(same.astype(F32), dta, precision=HIGHEST, preferred_element_type=F32)
    acs_t = acs.T
    dt_t = dt.T
    eacs = jnp.exp(acs)
    wdec = jnp.exp(tot - acs) * dt
    dec_tot = jnp.exp(tot)
    low = lax.broadcasted_iota(jnp.int32, (q, LANES), 1) < M2_HEAD_DIM

    def pair_cols(m, h0):
        return jnp.where(low, m[:, h0:h0 + 1], m[:, h0 + 1:h0 + 2])

    cgb = cg.astype(BF16)
    cb = lax.dot_general(cgb, bg.astype(BF16), (((1,), (1,)), ((), ())), preferred_element_type=F32)
    row_seq = lax.shift_right_logical(lax.broadcasted_iota(jnp.int32, (q, n), 0), shift)
    c_blocks = jnp.concatenate([jnp.where(row_seq == s, cg, 0.0) for s in range(ns)], axis=1).astype(BF16)
    h_all = h0_ref[...].reshape(ns * n, gw)
    yoff = jnp.dot(c_blocks, h_all.astype(BF16), preferred_element_type=F32)
    y_parts, xw_parts, dec_parts = [], [], []
    for pp in range(hpg // 2):
        h0 = 2 * pp
        lanes = slice(h0 * M2_HEAD_DIM, (h0 + 2) * M2_HEAD_DIM)
        w_pair = []
        for h in (h0, h0 + 1):
            seg = acs[:, h:h + 1] - acs_t[h:h + 1, :]
            dec = jnp.exp(jnp.where(causal, seg, -jnp.inf))
            w_pair.append((cb * dec * dt_t[h:h + 1, :]).astype(BF16))
        xp = xs[:, lanes]
        xbd = jnp.concatenate([jnp.where(low, xp, 0.0), jnp.where(low, 0.0, xp)], axis=0).astype(BF16)
        yd = jnp.dot(jnp.concatenate(w_pair, axis=1), xbd, preferred_element_type=F32)
        y_parts.append(yd + yoff[:, lanes] * pair_cols(eacs, h0) + dexp_ref[0, :, lanes] * xp)
        xw_parts.append(xp * pair_cols(wdec, h0))
        dec_parts.append(pair_cols(dec_tot, h0))
    xw = jnp.concatenate(xw_parts, axis=1).astype(BF16)
    col_seq = lax.shift_right_logical(lax.broadcasted_iota(jnp.int32, (n, q), 1), shift)
    bgt = bg.T
    b_blocks = jnp.concatenate([jnp.where(col_seq == s, bgt, 0.0) for s in range(ns)], axis=0).astype(BF16)
    st = jnp.dot(b_blocks, xw, preferred_element_type=F32)
    dec_rows = jnp.concatenate(dec_parts, axis=1)
    for s in range(ns):
        hn_ref[s] = h0_ref[s] * dec_rows[s * seq_len:s * seq_len + 1, :] + st[s * n:(s + 1) * n, :]

    zz = z_ref[...]
    y = jnp.concatenate(y_parts, axis=1) * (zz * _sigmoid(zz))
    y_ref[...] = y * lax.rsqrt(jnp.mean(y * y, axis=-1, keepdims=True) + EPS) * nw_ref[0]


def _by_group(a):
    gw = M2_WIDTH // M2_GROUPS
    n = M2_D_STATE
    return jnp.stack([jnp.concatenate([a[..., g * gw:(g + 1) * gw],
                                       a[..., M2_WIDTH + g * n:M2_WIDTH + (g + 1) * n],
                                       a[..., M2_WIDTH + (M2_GROUPS + g) * n:M2_WIDTH + (M2_GROUPS + g + 1) * n]],
                                      axis=-1) for g in range(M2_GROUPS)])


def _ssd_packed(xbc, z, dt, conv0, h0, prm, n_seq, seq_len):
    cw, cb, dtb, alog, dexp, nw = prm
    t = xbc.shape[0]
    hpg = M2_HEADS // M2_GROUPS
    gw = hpg * M2_HEAD_DIM
    cdim = gw + 2 * M2_D_STATE
    heads = lambda a: jnp.stack([jnp.pad(a[..., g * hpg:(g + 1) * hpg], [(0, 0)] * (a.ndim - 1) + [(0, LANES - hpg)])
                                 for g in range(M2_GROUPS)])
    halves = lambda a: jnp.stack([a[..., g * gw:(g + 1) * gw] for g in range(M2_GROUPS)])
    hist = jnp.stack([_by_group(jnp.pad(conv0[:, M2_CONV - 1 - k:, :], ((0, 0), (0, seq_len - k), (0, 0)))
                                .reshape(t, M2_CONV_DIM)) for k in range(1, M2_CONV)])
    per_g = lambda r, c: pl.BlockSpec((1, r, c), lambda i, g: (g, 0, 0))
    rows_g = lambda c: pl.BlockSpec((1, SSD_CHUNK, c), lambda i, g: (g, i, 0))
    ns = SSD_CHUNK // seq_len
    state = pl.BlockSpec((ns, M2_D_STATE, gw), lambda i, g: (i, 0, g))
    y, hn = pl.pallas_call(
        functools.partial(_ssd_packed_kernel, seq_len=seq_len),
        grid=(t // SSD_CHUNK, M2_GROUPS),
        in_specs=[rows_g(cdim),
                  pl.BlockSpec((M2_CONV - 1, 1, SSD_CHUNK, cdim), lambda i, g: (0, g, i, 0)),
                  pl.BlockSpec((SSD_CHUNK, gw), lambda i, g: (i, g)),
                  rows_g(LANES), state,
                  per_g(M2_CONV, cdim), per_g(1, cdim), per_g(1, LANES), per_g(1, LANES), per_g(1, gw), per_g(1, gw)],
        out_specs=[pl.BlockSpec((SSD_CHUNK, gw), lambda i, g: (i, g)), state],
        out_shape=[jax.ShapeDtypeStruct((t, M2_WIDTH), F32), jax.ShapeDtypeStruct((n_seq, M2_D_STATE, M2_WIDTH), F32)],
        scratch_shapes=[pltpu.VMEM((CONV_HALO + SSD_CHUNK, cdim), F32)],
        compiler_params=_cparams("parallel", "parallel"),
        name="ssd_packed",
    )(_by_group(xbc), hist, z, heads(dt), h0, _by_group(cw), _by_group(cb), heads(dtb), heads(alog),
      halves(dexp), halves(nw))
    conv_new = xbc.reshape(n_seq, seq_len, M2_CONV_DIM)[:, seq_len - (M2_CONV - 1):, :]
    return y, conv_new, hn


def _post_kernel(y5_ref, u_ref, ym_ref, x_ref, d_ref, wglu_ref, s5nw_ref, wo1_ref, wo2_ref, fnw_ref, wqt_ref,
                 sk_ref, x1_ref, xt_ref, st_ref):
    g = _gelu_tanh(_load_blocked(y5_ref) + d_ref[...] * _load_blocked(u_ref))
    o = g * _sigmoid(jnp.dot(g.astype(BF16), wglu_ref[...], preferred_element_type=F32))
    y5n = _rms(o, s5nw_ref[...])
    x1 = (x_ref[...] + jnp.dot(y5n.astype(BF16), wo1_ref[...], preferred_element_type=F32)
          + jnp.dot(ym_ref[...].astype(BF16), wo2_ref[...], preferred_element_type=F32))
    x1_ref[...] = x1
    hn_t = _rms(x1, fnw_ref[...]).T.astype(BF16)
    xt_ref[...] = hn_t
    q_t = jnp.dot(wqt_ref[...], hn_t, preferred_element_type=F32)
    for k in range(2 * PEER_HEADS):
        qk = q_t[k * PEER_HALF:(k + 1) * PEER_HALF, :].astype(BF16)
        st_ref[k] = jnp.dot(sk_ref[k], qk, preferred_element_type=F32) * LOG2E


def _post(y5, u, ym, x, prm, tm, s5_block=None):
    t = x.shape[0]
    row = lambda i: (i, 0)
    s5_spec = pl.BlockSpec((tm, D_MODEL), row) if s5_block is None else pl.BlockSpec(*s5_block)
    fix = lambda i: (0, 0)
    nk = 2 * PEER_HEADS
    once = functools.partial(pl.BlockSpec, pipeline_mode=pl.Buffered(1))
    return pl.pallas_call(
        _post_kernel,
        grid=(t // tm,),
        in_specs=[s5_spec, s5_spec, pl.BlockSpec((tm, D_MODEL), row), pl.BlockSpec((tm, D_MODEL), row)]
        + [pl.BlockSpec((1, D_MODEL), fix), once((S5_WIDTH, S5_WIDTH), fix), pl.BlockSpec((1, D_MODEL), fix),
           once((S5_WIDTH, D_MODEL), fix), once((M2_WIDTH, D_MODEL), fix),
           pl.BlockSpec((1, D_MODEL), fix), once((nk * PEER_HALF, D_MODEL), fix),
           once((nk, PEER_N_KEYS, PEER_HALF), lambda i: (0, 0, 0))],
        out_specs=[pl.BlockSpec((tm, D_MODEL), row), pl.BlockSpec((D_MODEL, tm), lambda i: (0, i)),
                   pl.BlockSpec((nk, PEER_N_KEYS, tm), lambda i: (0, 0, i))],
        out_shape=[jax.ShapeDtypeStruct((t, D_MODEL), F32), jax.ShapeDtypeStruct((D_MODEL, t), BF16),
                   jax.ShapeDtypeStruct((nk, PEER_N_KEYS, t), F32)],
        compiler_params=_cparams("parallel"),
        name="post",
    )(y5, u, ym, x, *prm)


def _sorting_network(n):
    pairs = []
    p = 1
    while p < n:
        k = p
        while k >= 1:
            for j in range(k % p, n - k, 2 * k):
                for i in range(min(k, n - j - k)):
                    if (i + j) // (2 * p) == (i + j + k) // (2 * p):
                        pairs.append((i + j, i + j + k))
            k //= 2
        p *= 2
    return pairs


SUBLANES = 8
_NET16 = _sorting_network(PEER_N_KEYS // SUBLANES)
LOG2E = math.log2(math.e)


def _top16(tiles):
    n = len(tiles)
    tiles = list(tiles)

    def exchange(i, j):
        hi, lo = tiles[i], tiles[j]
        if lo is None:
            return
        if hi is None:
            tiles[i], tiles[j] = lo, None
        else:
            tiles[i], tiles[j] = jnp.maximum(hi, lo), jnp.minimum(hi, lo)

    def larger(a, b):
        return b if a is None else a if b is None else jnp.maximum(a, b)

    for i, j in _NET16:
        exchange(i, j)
    shift = SUBLANES // 2
    while shift >= 1:
        other = [None if t is None else pltpu.roll(t, shift, axis=0) for t in tiles]
        tiles = [larger(tiles[i], other[n - 1 - i]) for i in range(n)]
        dist = n // 2
        while dist >= 1:
            for i in range(n):
                if i & dist == 0:
                    exchange(i, i + dist)
            dist //= 2
        shift //= 2
    return tiles


def _top_values(s):
    n = PEER_N_KEYS // SUBLANES
    tiles = _top16([s[i * SUBLANES:(i + 1) * SUBLANES, :] for i in range(n)])
    rows = lax.broadcasted_iota(jnp.int32, (PEER_TOPK, s.shape[1]), 0)
    out = jnp.zeros((PEER_TOPK, s.shape[1]), F32)
    for r in range(PEER_TOPK):
        out = jnp.where(rows == r, jnp.concatenate([tiles[r], tiles[r]], axis=0), out)
    return out


def _pair_candidates(v1, v2):
    r8 = lax.broadcasted_iota(jnp.int32, (SUBLANES, v1.shape[1]), 0)
    r16 = lax.broadcasted_iota(jnp.int32, (PEER_TOPK, v1.shape[1]), 0)
    neg = -jnp.inf
    lo2 = v2[0:SUBLANES, :]
    return jnp.concatenate([
        v1[0:1, :] + v2,
        v1[1:2, :] + lo2,
        jnp.where(r16 >= 2, v1 + v2[0:1, :], neg),
        jnp.where(r8 >= 2, v1[0:SUBLANES, :] + v2[1:2, :], neg),
        jnp.where((r8 >= 2) & (r8 <= 4), v1[2:3, :] + lo2, neg),
        jnp.where((r8 >= 2) & (r8 <= 3), v1[3:4, :] + lo2, neg),
        jnp.where(r8 == 2, v1[4:5, :] + lo2, neg),
    ], axis=0)


def _route_kernel(st_ref, s1n_ref, s2n_ref, thr_ref):
    def head(h, carry):
        s1 = st_ref[2 * h]
        s2 = st_ref[2 * h + 1]
        v1 = _top_values(s1)
        v2 = _top_values(s2)
        cand = _pair_candidates(v1, v2)
        n_cand = cand.shape[0] // SUBLANES
        pad = [None] * (PEER_N_KEYS // SUBLANES - n_cand)
        ranked = _top16([cand[i * SUBLANES:(i + 1) * SUBLANES, :] for i in range(n_cand)] + pad)
        theta = ranked[PEER_TOPK - 1][0:1, :]
        sel = cand >= theta
        m = v1[0:1, :] + v2[0:1, :]
        zsum = jnp.sum(jnp.where(sel, jnp.exp2(cand - m), 0.0), axis=0, keepdims=True)
        off = m + jnp.log(zsum) * LOG2E + 1.0
        s1n_ref[h] = s1 - off
        s2n_ref[h] = s2
        candn = _pair_candidates(v1 - off, v2)
        thr_ref[h] = jnp.min(jnp.where(sel, candn, jnp.inf), axis=0, keepdims=True)
        return carry

    lax.fori_loop(0, PEER_HEADS, head, 0)


def _route(st, tl):
    nk, keys, t = st.shape
    spec = lambda n: pl.BlockSpec((n, keys, tl), lambda i: (0, 0, i))
    return pl.pallas_call(
        _route_kernel,
        grid=(t // tl,),
        in_specs=[spec(nk)],
        out_specs=[spec(PEER_HEADS), spec(PEER_HEADS), pl.BlockSpec((PEER_HEADS, 1, tl), lambda i: (0, 0, i))],
        out_shape=[jax.ShapeDtypeStruct((PEER_HEADS, keys, t), F32), jax.ShapeDtypeStruct((PEER_HEADS, keys, t), F32),
                   jax.ShapeDtypeStruct((PEER_HEADS, 1, t), F32)],
        compiler_params=_cparams("parallel"),
        name="route",
    )(st)


PEER_E1_BLK = 16
PEER_CHUNK_KEYS = (4, 4, 4, 4)
assert sum(PEER_CHUNK_KEYS) == PEER_E1_BLK


def _peer_kernel(u_ref, vt_ref, xt_ref, s2n_ref, s1n_ref, thr_ref, x1_ref, fnw_ref, y_ref, acc_scr, *, tm):
    k = pl.program_id(1)
    c0 = math.sqrt(2.0 / math.pi)
    c1 = c0 * 0.044715
    first_key = [sum(PEER_CHUNK_KEYS[:q]) for q in range(len(PEER_CHUNK_KEYS))]

    def experts(q):
        return slice(first_key[q] * PEER_N_KEYS, (first_key[q] + PEER_CHUNK_KEYS[q]) * PEER_N_KEYS)

    def scores(q):
        return jnp.dot(u_ref[experts(q), :], xt_ref[...], preferred_element_type=F32)

    def gated(q, a):
        wg_rows = []
        for i in range(PEER_CHUNK_KEYS[q]):
            r = first_key[q] + i
            rows = slice(i * PEER_N_KEYS, (i + 1) * PEER_N_KEYS)
            wg_cols = []
            for c in range(tm // LANES):
                cols = slice(c * LANES, (c + 1) * LANES)
                w = None
                for h in range(PEER_HEADS):
                    arg = s2n_ref[h, :, cols] + s1n_ref[h, r:r + 1, cols]
                    term = jnp.where(arg >= thr_ref[h, :, cols], jnp.exp2(arg), 0.0)
                    w = term if w is None else w + term
                x = a[rows, cols]
                g = x + x * jnp.tanh(x * (c0 + c1 * (x * x)))
                wg_cols.append((w * g).astype(BF16))
            wg_rows.append(jnp.concatenate(wg_cols, axis=1))
        return jnp.concatenate(wg_rows, axis=0)

    def mixed(q, wg):
        return jnp.dot(vt_ref[0, :, experts(q)], wg, preferred_element_type=F32)

    n_chunks = len(PEER_CHUNK_KEYS)
    total = None
    a_next = scores(0)
    wg_prev = None
    for q in range(n_chunks):
        a_cur = a_next
        if q + 1 < n_chunks:
            a_next = scores(q + 1)
        wg = gated(q, a_cur)
        if wg_prev is not None:
            d = mixed(q - 1, wg_prev)
            total = d if total is None else total + d
        wg_prev = wg
    total = total + mixed(n_chunks - 1, wg_prev)

    @pl.when(k == 0)
    def _():
        acc_scr[...] = total

    @pl.when(k > 0)
    def _():
        acc_scr[...] += total

    @pl.when(k == pl.num_programs(1) - 1)
    def _():
        y_ref[...] = _rms(x1_ref[...] + acc_scr[...].T, fnw_ref[...])


def _peer(u_bf, vt_bf, xt, route, x1, fnw, tm):
    s1n, s2n, thr = route
    t = x1.shape[0]
    eb = PEER_E1_BLK * PEER_N_KEYS
    per_key = pl.BlockSpec((PEER_HEADS, PEER_N_KEYS, tm), lambda j, k: (0, 0, j))
    per_blk = pl.BlockSpec((PEER_HEADS, PEER_E1_BLK, tm), lambda j, k: (0, k, j))
    return pl.pallas_call(
        functools.partial(_peer_kernel, tm=tm),
        grid=(t // tm, PEER_N_KEYS // PEER_E1_BLK),
        in_specs=[pl.BlockSpec((eb, D_MODEL), lambda j, k: (k, 0)),
                  pl.BlockSpec((1, D_MODEL, eb), lambda j, k: (k, 0, 0)),
                  pl.BlockSpec((D_MODEL, tm), lambda j, k: (0, j)),
                  per_key, per_blk, pl.BlockSpec((PEER_HEADS, 1, tm), lambda j, k: (0, 0, j)),
                  pl.BlockSpec((tm, D_MODEL), lambda j, k: (j, 0)),
                  pl.BlockSpec((1, D_MODEL), lambda j, k: (0, 0))],
        out_specs=pl.BlockSpec((tm, D_MODEL), lambda j, k: (j, 0)),
        out_shape=jax.ShapeDtypeStruct((t, D_MODEL), F32),
        scratch_shapes=[pltpu.VMEM((D_MODEL, tm), F32)],
        compiler_params=_cparams("parallel", "arbitrary"),
        name="peer",
    )(u_bf, vt_bf, xt, s2n, s1n, thr, x1, fnw)


TOKEN_TILE = 512


def _tokens_tail(x, post_prm, peer_prm, y5, u, ym, s5_block):
    x1, xt, st = _post(y5, u, ym, x, post_prm, tm=TOKEN_TILE, s5_block=s5_block)
    u_bf, vt_bf, fnw = peer_prm
    return _peer(u_bf, vt_bf, xt, _route(st, tl=2 * LANES), x1, fnw, tm=TOKEN_TILE)


def kernel(x_prompt, x_sample, state_s5_re, state_s5_im, state_ssm, state_conv, meta_tokens, norm_mix_w, w_in,
           s5_lambda_re, s5_lambda_im, s5_log_step, s5_b_re, s5_b_im, s5_c_re, s5_c_im, s5_d, s5_w_glu, s5_norm_w,
           m2_conv_w, m2_conv_b, m2_dt_bias, m2_a_log, m2_d, m2_norm_w, w_out, norm_ffn_w, peer_w_q, peer_sub_keys,
           peer_u, peer_v, final_norm_w):
    bp, sp, _ = x_prompt.shape
    bs, ss, _ = x_sample.shape
    g, p = S5_GROUPS, S5_STATE

    w = w_in[0]
    o1, o2, o3 = S5_WIDTH, S5_WIDTH + M2_WIDTH, S5_WIDTH + M2_WIDTH + M2_CONV_DIM
    wu, wz, wx = w[:, :o1].astype(BF16), w[:, o1:o2].astype(BF16), w[:, o2:o3].astype(BF16)
    wd = jnp.pad(w[:, o3:], ((0, 0), (0, LANES - M2_HEADS))).astype(BF16)
    nmw = norm_mix_w[0][None, :]
    pad_h = lambda v: jnp.pad(v, (0, LANES - M2_HEADS))[None, :]
    ssd_prm = (m2_conv_w[0], m2_conv_b[0][None, :], pad_h(m2_dt_bias[0]), pad_h(m2_a_log[0]),
               jnp.repeat(m2_d[0], M2_HEAD_DIM)[None, :], m2_norm_w[0][None, :])
    post_prm = (s5_d[0][None, :], s5_w_glu[0].astype(BF16), s5_norm_w[0][None, :],
                w_out[0][:S5_WIDTH].astype(BF16), w_out[0][S5_WIDTH:].astype(BF16), norm_ffn_w[0][None, :],
                peer_w_q[0].T.astype(BF16),
                peer_sub_keys[0].reshape(2 * PEER_HEADS, PEER_N_KEYS, PEER_HALF).astype(BF16))
    eb = PEER_E1_BLK * PEER_N_KEYS
    vt_blocks = peer_v[0].astype(BF16).reshape(PEER_EXPERTS // eb, eb, D_MODEL).swapaxes(1, 2)
    peer_prm = (peer_u[0].astype(BF16), vt_blocks, final_norm_w[None, :])
    s5_args = (s5_lambda_re[0], s5_lambda_im[0], s5_log_step[0], s5_b_re[0], s5_b_im[0], s5_c_re[0], s5_c_im[0])
    ops16 = _s5_prep(*s5_args, lc=N_META)
    ops8 = _s5_prep(*s5_args, lc=ss)

    lc = N_META
    um, _, xbcm, dtm = _in_proj(meta_tokens, nmw, wu, wz, wx, wd, tm=N_META)
    nb = S5_WIDTH // LANES
    urm = jnp.broadcast_to(um.reshape(lc, nb, LANES).transpose(1, 0, 2).reshape(nb, 1, lc * LANES),
                           (nb, SUBLANES, lc * LANES))
    _, h5m = _s5(urm, jnp.zeros((g, SUBLANES, 2 * p), F32), ops16, lc=lc, rows=SUBLANES, chunks=1)
    _, convm, hm = _ssd(xbcm, jnp.zeros((N_META, M2_WIDTH), F32), dtm,
                        jnp.zeros((1, M2_CONV - 1, M2_CONV_DIM), F32), jnp.zeros((1, M2_D_STATE, M2_WIDTH), F32),
                        ssd_prm, n_seq=1, n_chunks=1, q_in=N_META, shared_init=True)

    xp = x_prompt.reshape(bp * sp, D_MODEL)
    nc = sp // lc
    cpt = TOKEN_TILE // lc
    tps = sp // TOKEN_TILE
    p_shape = (nb, nc, bp, lc, LANES)
    p_block = ((nb, cpt, 1, lc, LANES), lambda i: (0, i % tps, i // tps, 0, 0))
    up, zp, xbcp, dtp = _in_proj(xp, nmw, wu, wz, wx, wd, tm=TOKEN_TILE, u_layout=(p_shape,) + p_block)
    y5p, h5p = _s5(up.reshape(nb, nc * bp, lc * LANES), h5m, ops16, lc=lc, rows=bp, chunks=S5_ROW_TILE // bp)
    ymp, convp, hp = _ssd(xbcp, zp, dtp, convm, hm, ssd_prm, n_seq=bp, n_chunks=sp // SSD_CHUNK, q_in=SSD_CHUNK,
                          shared_init=True)
    y_prompt = _tokens_tail(xp, post_prm, peer_prm, y5p.reshape(p_shape), up, ymp, p_block).reshape(bp, sp, D_MODEL)

    xs = x_sample.reshape(bs * ss, D_MODEL)
    s_shape = (nb, bs, ss, LANES)
    s_block = ((nb, TOKEN_TILE // ss, ss, LANES), lambda i: (0, i, 0, 0))
    us, zs, xbcs, dts = _in_proj(xs, nmw, wu, wz, wx, wd, tm=TOKEN_TILE, u_layout=(s_shape,) + s_block)
    h5s0 = jnp.concatenate([state_s5_re[0], state_s5_im[0]], axis=-1).transpose(1, 0, 2)
    y5s, h5s = _s5(us.reshape(nb, bs, ss * LANES), h5s0, ops8, lc=ss, rows=bs, chunks=1)
    hs0 = state_ssm[0].reshape(bs, M2_WIDTH, M2_D_STATE).transpose(0, 2, 1)
    yms, convs, hs = _ssd_packed(xbcs, zs, dts, state_conv[0], hs0, ssd_prm, n_seq=bs, seq_len=ss)
    y_sample = _tokens_tail(xs, post_prm, peer_prm, y5s.reshape(s_shape), us, yms, s_block).reshape(bs, ss, D_MODEL)

    def s5_state(hf):
        hf = hf.transpose(1, 0, 2)
        return hf[None, :, :, :p], hf[None, :, :, p:]

    def ssm_state(ht):
        return ht.transpose(0, 2, 1).reshape(1, ht.shape[0], M2_HEADS, M2_HEAD_DIM, M2_D_STATE)

    p5r, p5i = s5_state(h5p)
    s5r, s5i = s5_state(h5s)
    return (y_prompt, y_sample, p5r, p5i, ssm_state(hp), convp[None], s5r, s5i, ssm_state(hs), convs[None])
```

```python
import functools
import math

import jax
import jax.numpy as jnp
from jax import lax
from jax.experimental import pallas as pl
from jax.experimental.pallas import tpu as pltpu

F32 = jnp.float32
BF16 = jnp.bfloat16
HIGHEST = lax.Precision.HIGHEST

D_MODEL = 1024
N_META = 16
S5_WIDTH = 1024
S5_CH = 16
S5_GROUPS = S5_WIDTH // S5_CH
S5_STATE = 64
M2_WIDTH = 1024
M2_HEAD_DIM = 64
M2_HEADS = M2_WIDTH // M2_HEAD_DIM
M2_GROUPS = 2
M2_D_STATE = 128
M2_CONV = 4
M2_CONV_DIM = M2_WIDTH + 2 * M2_GROUPS * M2_D_STATE
PEER_HEADS = 8
PEER_N_KEYS = 128
PEER_EXPERTS = PEER_N_KEYS * PEER_N_KEYS
PEER_HALF = 128
PEER_TOPK = 16
EPS = 1e-6

LANES = 128
SSD_CHUNK = 128
CONV_HALO = 8
VMEM_LIMIT = 56 * 1024 * 1024


def _cparams(*sem):
    return pltpu.CompilerParams(dimension_semantics=sem, vmem_limit_bytes=VMEM_LIMIT)


def _sigmoid(x):
    return 1.0 / (1.0 + jnp.exp(-x))


def _gelu_tanh(x):
    c = math.sqrt(2.0 / math.pi)
    return 0.5 * x * (1.0 + jnp.tanh(c * (x + 0.044715 * (x * x * x))))


def _rms(x, w):
    return x * lax.rsqrt(jnp.mean(x * x, axis=-1, keepdims=True) + EPS) * w


def _bdot(a, b):
    return jnp.dot(a.astype(BF16), b.astype(BF16), preferred_element_type=F32)


def _store_blocked(ref, val):
    if len(ref.shape) == 2:
        ref[...] = val
        return
    for gb in range(ref.shape[0]):
        piece = val[:, gb * LANES:(gb + 1) * LANES]
        if len(ref.shape) == 5:
            ref[gb, :, 0] = piece.reshape(ref.shape[1], ref.shape[3], LANES)
        else:
            ref[gb] = piece.reshape(ref.shape[1:])


def _load_blocked(ref):
    if len(ref.shape) == 2:
        return ref[...]
    parts = []
    for gb in range(ref.shape[0]):
        piece = ref[gb, :, 0] if len(ref.shape) == 5 else ref[gb]
        parts.append(piece.reshape(piece.shape[0] * piece.shape[1], LANES))
    return jnp.concatenate(parts, axis=1)


def _in_proj_kernel(x_ref, nw_ref, wu_ref, wz_ref, wx_ref, wd_ref, u_ref, z_ref, xbc_ref, dt_ref):
    hb = _rms(x_ref[...], nw_ref[...]).astype(BF16)
    _store_blocked(u_ref, jnp.dot(hb, wu_ref[...], preferred_element_type=F32))
    z_ref[...] = jnp.dot(hb, wz_ref[...], preferred_element_type=F32)
    xbc_ref[...] = jnp.dot(hb, wx_ref[...], preferred_element_type=F32)
    dt_ref[...] = jnp.dot(hb, wd_ref[...], preferred_element_type=F32)


def _in_proj(x, nw, wu, wz, wx, wd, tm, u_layout=None):
    t = x.shape[0]
    row = lambda i: (i, 0)
    fix = lambda i: (0, 0)
    widths = (S5_WIDTH, M2_WIDTH, M2_CONV_DIM, LANES)
    out_specs = [pl.BlockSpec((tm, w), row) for w in widths]
    out_shape = [jax.ShapeDtypeStruct((t, w), F32) for w in widths]
    if u_layout is not None:
        out_specs[0] = pl.BlockSpec(u_layout[1], u_layout[2])
        out_shape[0] = jax.ShapeDtypeStruct(u_layout[0], F32)
    return pl.pallas_call(
        _in_proj_kernel,
        grid=(t // tm,),
        in_specs=[pl.BlockSpec((tm, D_MODEL), row), pl.BlockSpec((1, D_MODEL), fix)]
        + [pl.BlockSpec((D_MODEL, w), fix) for w in widths],
        out_specs=out_specs,
        out_shape=out_shape,
        compiler_params=_cparams("parallel"),
        name="in_proj",
    )(x, nw, wu, wz, wx, wd)


def _s5_prep_kernel(lrc_ref, lic_ref, lrr_ref, lir_ref, ls_ref, btr_ref, bti_ref, ctr_ref, cti_ref,
                    kt_ref, wout_ref, wsr_ref, wsi_ref, al_ref, *, lc):
    for j in range(S5_GBLK):
        _s5_prep_group(j, lc, lrc_ref, lic_ref, lrr_ref, lir_ref, ls_ref, btr_ref, bti_ref, ctr_ref, cti_ref,
                       kt_ref, wout_ref, wsr_ref, wsi_ref, al_ref)


def _s5_prep_group(j, lc, lrc_ref, lic_ref, lrr_ref, lir_ref, ls_ref, btr_ref, bti_ref, ctr_ref, cti_ref,
                   kt_ref, wout_ref, wsr_ref, wsi_ref, al_ref):
    k = lc * S5_CH
    step = jnp.exp(ls_ref[j])

    def disc(lr, li):
        lg = lr * step
        th = li * step
        mag = jnp.exp(lg)
        ab_re = mag * jnp.cos(th)
        ab_im = mag * jnp.sin(th)
        den = lr * lr + li * li
        f_re = ((ab_re - 1.0) * lr + ab_im * li) / den
        f_im = (ab_im * lr - (ab_re - 1.0) * li) / den
        return ab_re, ab_im, f_re, f_im

    def powers(n, a_re_, a_im_):
        p_re_ = jnp.ones(n.shape, F32)
        p_im_ = jnp.zeros(n.shape, F32)
        b_re_, b_im_ = a_re_, a_im_
        bit = 1
        while bit <= lc:
            use = (n & bit) != 0
            t_re = p_re_ * b_re_ - p_im_ * b_im_
            t_im = p_re_ * b_im_ + p_im_ * b_re_
            p_re_ = jnp.where(use, t_re, p_re_)
            p_im_ = jnp.where(use, t_im, p_im_)
            b_re_, b_im_ = b_re_ * b_re_ - b_im_ * b_im_, 2.0 * (b_re_ * b_im_)
            bit *= 2
        return p_re_, p_im_

    ac_re, ac_im, f_re, f_im = disc(lrc_ref[j], lic_ref[j])
    tau = lax.shift_right_logical(lax.broadcasted_iota(jnp.int32, (S5_STATE, k), 1), 4)
    c_re = ctr_ref[j]
    c_im = cti_ref[j]
    p_re, p_im = powers(tau, ac_re, ac_im)
    fa_re = f_re * p_re - f_im * p_im
    fa_im = f_re * p_im + f_im * p_re
    fca_re = fa_re * c_re - fa_im * c_im
    fca_im = fa_re * c_im + fa_im * c_re
    kt_ref[j] = (jnp.dot(btr_ref[j], fca_re, precision=HIGHEST, preferred_element_type=F32)
                 - jnp.dot(bti_ref[j], fca_im, precision=HIGHEST, preferred_element_type=F32))
    q_re, q_im = powers(tau + 1, ac_re, ac_im)
    wout_ref[j, 0:S5_STATE, :] = q_re * c_re - q_im * c_im
    wout_ref[j, S5_STATE:2 * S5_STATE, :] = -(q_re * c_im + q_im * c_re)

    ar_re, ar_im, fr_re, fr_im = disc(lrr_ref[j], lir_ref[j])
    kk = lax.broadcasted_iota(jnp.int32, (lc, S5_STATE), 0)
    r_re, r_im = powers(kk, ar_re, ar_im)
    g_re = fr_re * r_re - fr_im * r_im
    g_im = fr_re * r_im + fr_im * r_re
    bt_re = btr_ref[j]
    bt_im = bti_ref[j]
    for s in range(lc):
        kpow = lc - 1 - s
        w_re = g_re[kpow:kpow + 1, :]
        w_im = g_im[kpow:kpow + 1, :]
        wsr_ref[j, s * S5_CH:(s + 1) * S5_CH, :] = bt_re * w_re - bt_im * w_im
        wsi_ref[j, s * S5_CH:(s + 1) * S5_CH, :] = bt_re * w_im + bt_im * w_re
    a_re, a_im = powers(jnp.full((1, S5_STATE), lc, jnp.int32), ar_re, ar_im)
    al_ref[j, 0:1, :] = a_re
    al_ref[j, 1:2, :] = a_im


def _s5_prep(lam_re, lam_im, log_step, b_re, b_im, c_re, c_im, lc):
    g, p, k = S5_GROUPS, S5_STATE, lc * S5_CH
    spec3 = lambda a, b: pl.BlockSpec((S5_GBLK, a, b), lambda i: (i, 0, 0))
    ins = [lam_re.reshape(g, p, 1), lam_im.reshape(g, p, 1), lam_re.reshape(g, 1, p), lam_im.reshape(g, 1, p),
           log_step.reshape(g, 1, 1),
           jnp.swapaxes(b_re, 1, 2), jnp.swapaxes(b_im, 1, 2),
           jnp.tile(jnp.swapaxes(c_re, 1, 2), (1, 1, lc)), jnp.tile(jnp.swapaxes(c_im, 1, 2), (1, 1, lc))]
    kt, wout, wsr, wsi, al = pl.pallas_call(
        functools.partial(_s5_prep_kernel, lc=lc),
        grid=(g // S5_GBLK,),
        in_specs=[spec3(p, 1), spec3(p, 1), spec3(1, p), spec3(1, p), spec3(1, 1),
                  spec3(S5_CH, p), spec3(S5_CH, p), spec3(p, k), spec3(p, k)],
        out_specs=[spec3(S5_CH, k), spec3(2 * p, k), spec3(k, p), spec3(k, p), spec3(2, p)],
        out_shape=[jax.ShapeDtypeStruct((g, S5_CH, k), F32), jax.ShapeDtypeStruct((g, 2 * p, k), F32),
                   jax.ShapeDtypeStruct((g, k, p), F32), jax.ShapeDtypeStruct((g, k, p), F32),
                   jax.ShapeDtypeStruct((g, 2, p), F32)],
        compiler_params=_cparams("parallel"),
        name="s5_prep",
    )(*ins)
    lag = jnp.arange(lc)[None, :] - jnp.arange(lc)[:, None]
    kt4 = kt.reshape(g, S5_CH, lc, S5_CH)
    toep = kt4[:, :, jnp.maximum(lag, 0), :]
    toep = jnp.where((lag >= 0)[None, None, :, :, None], toep, 0.0)
    toep = toep.transpose(0, 2, 1, 3, 4).reshape(g, k, k)
    wst = jnp.concatenate([wsr, wsi, wsi, wsr], axis=-1)
    a_re, a_im = al[:, 0:1, :], al[:, 1:2, :]
    acoef = jnp.concatenate([jnp.concatenate([a_re, a_re], -1), jnp.concatenate([-a_im, a_im], -1),
                             jnp.concatenate([a_im, -a_im], -1)], axis=1)
    return toep.astype(BF16), wst.astype(BF16), wout.astype(BF16), acoef


S5_GBLK = LANES // S5_CH
S5_ROW_TILE = 256


def _block_transpose(cols):
    n = len(cols)
    blk = lax.shift_right_logical(lax.broadcasted_iota(jnp.int32, (1, LANES), 1), 4)
    rolled = []
    for d in range(n):
        m = cols[d]
        for b in range(1, n):
            m = jnp.where(blk == b, cols[(b + d) % n], m)
        rolled.append(m if d == 0 else pltpu.roll(m, d * S5_CH, axis=1))
    outs = []
    for j in range(n):
        o = rolled[(-j) % n]
        for s in range(1, n):
            o = jnp.where(blk == s, rolled[(s - j) % n], o)
        outs.append(o)
    return outs


def _s5_kernel(u_ref, h0_ref, h0s_ref, wst_ref, toep_ref, wout_ref, a_ref, y_ref, hf_ref,
               s_scr, hin_scr, h_scr, hs_scr, *, lc, rows, chunks, tall):
    i = pl.program_id(1)

    @pl.when(i == 0)
    def _():
        h_scr[...] = h0_ref[...]
        hs_scr[...] = h0s_ref[...]

    rt = rows * chunks
    if tall:
        token = lambda s: u_ref[0, pl.ds(s, rt, stride=lc), :]
    else:
        x = u_ref[0]
        token = lambda s: x[:, s * LANES:(s + 1) * LANES]
    halves = [_block_transpose([token(S5_GBLK * c + s) for s in range(S5_GBLK)]) for c in range(lc // S5_GBLK)]
    ubs = []
    for j in range(S5_GBLK):
        ubs.append(jnp.concatenate([hv[j] for hv in halves], axis=1).astype(BF16))
        s_scr[j] = jnp.dot(ubs[j], wst_ref[j], preferred_element_type=F32)

    def step(r0, carry):
        new = []
        for j in range(S5_GBLK):
            h, hs = carry[2 * j], carry[2 * j + 1]
            hin_scr[j, pl.ds(r0, rows), :] = h
            s = s_scr[j, pl.ds(r0, rows), :]
            a1, a2, a2s = a_ref[j, 0:1, :], a_ref[j, 1:2, :], a_ref[j, 2:3, :]
            new.append(a1 * h + a2 * hs + s[:, :LANES])
            new.append(a1 * hs + a2s * h + s[:, LANES:])
        return tuple(new)

    carry = tuple(ref[j] for j in range(S5_GBLK) for ref in (h_scr, hs_scr))
    if chunks == 1:
        carry = step(0, carry)
    else:
        carry = lax.fori_loop(0, chunks, lambda n, c: step(pl.multiple_of(n * rows, rows), c), carry)
    for j in range(S5_GBLK):
        h_scr[j] = carry[2 * j]
        hs_scr[j] = carry[2 * j + 1]
    hf_ref[...] = h_scr[...]

    ys = [jnp.dot(ubs[j], toep_ref[j], preferred_element_type=F32)
          + jnp.dot(hin_scr[j].astype(BF16), wout_ref[j], preferred_element_type=F32) for j in range(S5_GBLK)]
    for c in range(lc // S5_GBLK):
        token_major = _block_transpose([ys[j][:, c * LANES:(c + 1) * LANES] for j in range(S5_GBLK)])
        for t in range(S5_GBLK):
            if tall:
                y_ref[0, pl.ds(S5_GBLK * c + t, rt, stride=lc), :] = token_major[t]
            else:
                y_ref[0, :, (S5_GBLK * c + t) * LANES:(S5_GBLK * c + t + 1) * LANES] = token_major[t]


def _s5(ub, h0, ops, lc, rows, chunks):
    toep, wst, wout, acoef = ops
    tall = ub.shape[2] == LANES
    nb = ub.shape[0]
    nr = ub.shape[1] // lc if tall else ub.shape[1]
    k = lc * S5_CH
    rt = rows * chunks
    io_block = (1, rt * lc, LANES) if tall else (1, rt, lc * LANES)
    blk = lambda a, b: pl.BlockSpec((S5_GBLK, a, b), lambda gb, i: (gb, 0, 0))
    return pl.pallas_call(
        functools.partial(_s5_kernel, lc=lc, rows=rows, chunks=chunks, tall=tall),
        grid=(nb, nr // rt),
        in_specs=[pl.BlockSpec(io_block, lambda gb, i: (gb, i, 0)),
                  blk(rows, LANES), blk(rows, LANES), blk(k, 2 * LANES), blk(k, k), blk(LANES, k), blk(3, LANES)],
        out_specs=[pl.BlockSpec(io_block, lambda gb, i: (gb, i, 0)), blk(rows, LANES)],
        out_shape=[jax.ShapeDtypeStruct(ub.shape, F32), jax.ShapeDtypeStruct((S5_GROUPS, rows, LANES), F32)],
        scratch_shapes=[pltpu.VMEM((S5_GBLK, rt, 2 * LANES), F32), pltpu.VMEM((S5_GBLK, rt, LANES), F32),
                        pltpu.VMEM((S5_GBLK, rows, LANES), F32), pltpu.VMEM((S5_GBLK, rows, LANES), F32)],
        compiler_params=_cparams("parallel", "arbitrary"),
        name="s5_scan",
    )(ub, h0, jnp.roll(h0, S5_STATE, axis=-1), wst, toep, wout, acoef)


def _ssd_kernel(xbc_ref, z_ref, dt_ref, conv0_ref, h0_ref, cw_ref, cb_ref, dtb_ref, alog_ref, dexp_ref, nw_ref,
                y_ref, convn_ref, hn_ref, buf_scr, dt_scr, h_scr, *, q_in, n_chunks):
    q = SSD_CHUNK
    pad = q - q_in
    first = CONV_HALO + pad
    c = pl.program_id(1)

    @pl.when(c == 0)
    def _():
        buf_scr[0:first, :] = jnp.zeros((first, M2_CONV_DIM), F32)
        buf_scr[first - (M2_CONV - 1):first, :] = conv0_ref[0]
        h_scr[...] = h0_ref[0]

    buf_scr[first:CONV_HALO + q, :] = xbc_ref[...]
    conv = cb_ref[...]
    for kk in range(M2_CONV):
        lo = CONV_HALO - (M2_CONV - 1) + kk
        conv = conv + cw_ref[kk:kk + 1, :] * buf_scr[lo:lo + q, :]
    convn_ref[0] = buf_scr[CONV_HALO + q - (M2_CONV - 1):CONV_HALO + q, :]
    if n_chunks > 1:
        buf_scr[0:CONV_HALO, :] = buf_scr[q:q + CONV_HALO, :]
    act = conv * _sigmoid(conv)
    xs = act[:, :M2_WIDTH]

    dtv = dt_ref[...] + dtb_ref[...]
    dt_real = jnp.maximum(dtv, 0.0) + jnp.log1p(jnp.exp(-jnp.abs(dtv)))
    if pad:
        dt_scr[0:pad, :] = jnp.zeros((pad, LANES), F32)
        dt_scr[pad:q, :] = dt_real
        dt = dt_scr[...]
    else:
        dt = dt_real
    a_neg = -jnp.exp(alog_ref[...])
    ri = lax.broadcasted_iota(jnp.int32, (q, q), 0)
    ci = lax.broadcasted_iota(jnp.int32, (q, q), 1)
    tril = ri >= ci
    acs = jnp.dot(tril.astype(F32), dt * a_neg, precision=HIGHEST, preferred_element_type=F32)
    acs_t = acs.T
    dt_t = dt.T
    acs_last = acs[q - 1:q, :]
    eacs = jnp.exp(acs)
    wdec = jnp.exp(acs_last - acs) * dt
    dec_last = jnp.exp(acs_last)
    low = lax.broadcasted_iota(jnp.int32, (q, LANES), 1) < M2_HEAD_DIM
    low1 = low[0:1, :]

    def pair_cols(m, h0):
        return jnp.where(low[0:m.shape[0], :], m[:, h0:h0 + 1], m[:, h0 + 1:h0 + 2])

    y_parts = []
    hpg = M2_HEADS // M2_GROUPS
    gw = hpg * M2_HEAD_DIM
    for g in range(M2_GROUPS):
        bg = act[:, M2_WIDTH + g * M2_D_STATE:M2_WIDTH + (g + 1) * M2_D_STATE]
        cg = act[:, M2_WIDTH + (M2_GROUPS + g) * M2_D_STATE:M2_WIDTH + (M2_GROUPS + g + 1) * M2_D_STATE]
        cgb = cg.astype(BF16)
        cb = lax.dot_general(cgb, bg.astype(BF16), (((1,), (1,)), ((), ())), preferred_element_type=F32)
        hg = h_scr[:, g * gw:(g + 1) * gw]
        yoff = jnp.dot(cgb, hg.astype(BF16), preferred_element_type=F32)
        xw_parts = []
        dec_parts = []
        for pp in range(hpg // 2):
            h0 = g * hpg + 2 * pp
            lanes = slice(h0 * M2_HEAD_DIM, (h0 + 2) * M2_HEAD_DIM)
            w_pair = []
            for h in (h0, h0 + 1):
                seg = acs[:, h:h + 1] - acs_t[h:h + 1, :]
                dec = jnp.exp(jnp.where(tril, seg, -jnp.inf))
                w_pair.append((cb * dec * dt_t[h:h + 1, :]).astype(BF16))
            xp = xs[:, lanes]
            xbd = jnp.concatenate([jnp.where(low, xp, 0.0), jnp.where(low, 0.0, xp)], axis=0).astype(BF16)
            yd = jnp.dot(jnp.concatenate(w_pair, axis=1), xbd, preferred_element_type=F32)
            yo = yoff[:, 2 * pp * M2_HEAD_DIM:(2 * pp + 2) * M2_HEAD_DIM]
            y_parts.append(yd + yo * pair_cols(eacs, h0) + dexp_ref[:, lanes] * xp)
            xw_parts.append(xp * pair_cols(wdec, h0))
            dec_parts.append(jnp.where(low1, dec_last[:, h0:h0 + 1], dec_last[:, h0 + 1:h0 + 2]))
        xw = jnp.concatenate(xw_parts, axis=1).astype(BF16)
        st = jnp.dot(bg.T.astype(BF16), xw, preferred_element_type=F32)
        h_scr[:, g * gw:(g + 1) * gw] = hg * jnp.concatenate(dec_parts, axis=1) + st

    @pl.when(c == n_chunks - 1)
    def _():
        hn_ref[0] = h_scr[...]

    y = jnp.concatenate(y_parts, axis=1)[pad:, :]
    zz = z_ref[...]
    y = y * (zz * _sigmoid(zz))
    outs = []
    for g in range(M2_GROUPS):
        yg = y[:, g * gw:(g + 1) * gw]
        outs.append(yg * lax.rsqrt(jnp.mean(yg * yg, axis=-1, keepdims=True) + EPS))
    y_ref[...] = jnp.concatenate(outs, axis=1) * nw_ref[...]


def _ssd(xbc, z, dt, conv0, h0, prm, n_seq, n_chunks, q_in, shared_init):
    assert q_in == SSD_CHUNK or n_chunks == 1
    assert q_in >= CONV_HALO and q_in % CONV_HALO == 0
    t = xbc.shape[0]
    row = lambda b, c: (b * n_chunks + c, 0)
    fix = lambda b, c: (0, 0)
    init = (lambda b, c: (0, 0, 0)) if shared_init else (lambda b, c: (b, 0, 0))
    per_seq = lambda b, c: (b, 0, 0)
    return pl.pallas_call(
        functools.partial(_ssd_kernel, q_in=q_in, n_chunks=n_chunks),
        grid=(n_seq, n_chunks),
        in_specs=[pl.BlockSpec((q_in, M2_CONV_DIM), row), pl.BlockSpec((q_in, M2_WIDTH), row),
                  pl.BlockSpec((q_in, LANES), row),
                  pl.BlockSpec((1, M2_CONV - 1, M2_CONV_DIM), init),
                  pl.BlockSpec((1, M2_D_STATE, M2_WIDTH), init),
                  pl.BlockSpec((M2_CONV, M2_CONV_DIM), fix), pl.BlockSpec((1, M2_CONV_DIM), fix),
                  pl.BlockSpec((1, LANES), fix), pl.BlockSpec((1, LANES), fix),
                  pl.BlockSpec((1, M2_WIDTH), fix), pl.BlockSpec((1, M2_WIDTH), fix)],
        out_specs=[pl.BlockSpec((q_in, M2_WIDTH), row),
                   pl.BlockSpec((1, M2_CONV - 1, M2_CONV_DIM), per_seq),
                   pl.BlockSpec((1, M2_D_STATE, M2_WIDTH), per_seq)],
        out_shape=[jax.ShapeDtypeStruct((t, M2_WIDTH), F32),
                   jax.ShapeDtypeStruct((n_seq, M2_CONV - 1, M2_CONV_DIM), F32),
                   jax.ShapeDtypeStruct((n_seq, M2_D_STATE, M2_WIDTH), F32)],
        scratch_shapes=[pltpu.VMEM((CONV_HALO + SSD_CHUNK, M2_CONV_DIM), F32),
                        pltpu.VMEM((SSD_CHUNK, LANES), F32),
                        pltpu.VMEM((M2_D_STATE, M2_WIDTH), F32)],
        compiler_params=_cparams("parallel", "arbitrary"),
        name="ssd",
    )(xbc, z, dt, conv0, h0, *prm)


def _ssd_packed_kernel(xbc_ref, hist_ref, z_ref, dt_ref, h0_ref, cw_ref, cb_ref, dtb_ref, alog_ref, dexp_ref, nw_ref,
                       y_ref, hn_ref, buf_scr, *, seq_len):
    q = SSD_CHUNK
    n = M2_D_STATE
    ns = q // seq_len
    shift = seq_len.bit_length() - 1
    hpg = M2_HEADS // M2_GROUPS
    gw = hpg * M2_HEAD_DIM
    cdim = gw + 2 * n
    x = xbc_ref[0]
    buf_scr[0:CONV_HALO, :] = jnp.zeros((CONV_HALO, cdim), F32)
    buf_scr[CONV_HALO:CONV_HALO + q, :] = x
    pos = lax.broadcasted_iota(jnp.int32, (q, cdim), 0) & (seq_len - 1)
    conv = cb_ref[0] + cw_ref[0, M2_CONV - 1:M2_CONV, :] * x
    for k in range(1, M2_CONV):
        prev = jnp.where(pos >= k, buf_scr[CONV_HALO - k:CONV_HALO - k + q, :], hist_ref[k - 1, 0])
        conv = conv + cw_ref[0, M2_CONV - 1 - k:M2_CONV - k, :] * prev
    act = conv * _sigmoid(conv)
    xs = act[:, :gw]
    bg = act[:, gw:gw + n]
    cg = act[:, gw + n:]

    dtv = dt_ref[0] + dtb_ref[0]
    dt = jnp.maximum(dtv, 0.0) + jnp.log1p(jnp.exp(-jnp.abs(dtv)))
    a_neg = -jnp.exp(alog_ref[0])
    ri = lax.broadcasted_iota(jnp.int32, (q, q), 0)
    ci = lax.broadcasted_iota(jnp.int32, (q, q), 1)
    same = lax.shift_right_logical(ri, shift) == lax.shift_right_logical(ci, shift)
    causal = (ri >= ci) & same
    dta = dt * a_neg
    acs = jnp.dot(causal.astype(F32), dta, precision=HIGHEST, preferred_element_type=F32)
    tot = jnp.dot(same.astype(F32), dta, precision=HIGHEST, preferred_element_type=F32)
    acs_t = acs.T
    dt_t = dt.T
    eacs = jnp.exp(acs)
    wdec = jnp.exp(tot - acs) * dt
    dec_tot = jnp.exp(tot)
    low = lax.broadcasted_iota(jnp.int32, (q, LANES), 1) < M2_HEAD_DIM

    def pair_cols(m, h0):
        return jnp.where(low, m[:, h0:h0 + 1], m[:, h0 + 1:h0 + 2])

    cgb = cg.astype(BF16)
    cb = lax.dot_general(cgb, bg.astype(BF16), (((1,), (1,)), ((), ())), preferred_element_type=F32)
    row_seq = lax.shift_right_logical(lax.broadcasted_iota(jnp.int32, (q, n), 0), shift)
    c_blocks = jnp.concatenate([jnp.where(row_seq == s, cg, 0.0) for s in range(ns)], axis=1).astype(BF16)
    h_all = h0_ref[...].reshape(ns * n, gw)
    yoff = jnp.dot(c_blocks, h_all.astype(BF16), preferred_element_type=F32)
    y_parts, xw_parts, dec_parts = [], [], []
    for pp in range(hpg // 2):
        h0 = 2 * pp
        lanes = slice(h0 * M2_HEAD_DIM, (h0 + 2) * M2_HEAD_DIM)
        w_pair = []
        for h in (h0, h0 + 1):
            seg = acs[:, h:h + 1] - acs_t[h:h + 1, :]
            dec = jnp.exp(jnp.where(causal, seg, -jnp.inf))
            w_pair.append((cb * dec * dt_t[h:h + 1, :]).astype(BF16))
        xp = xs[:, lanes]
        xbd = jnp.concatenate([jnp.where(low, xp, 0.0), jnp.where(low, 0.0, xp)], axis=0).astype(BF16)
        yd = jnp.dot(jnp.concatenate(w_pair, axis=1), xbd, preferred_element_type=F32)
        y_parts.append(yd + yoff[:, lanes] * pair_cols(eacs, h0) + dexp_ref[0, :, lanes] * xp)
        xw_parts.append(xp * pair_cols(wdec, h0))
        dec_parts.append(pair_cols(dec_tot, h0))
    xw = jnp.concatenate(xw_parts, axis=1).astype(BF16)
    col_seq = lax.shift_right_logical(lax.broadcasted_iota(jnp.int32, (n, q), 1), shift)
    bgt = bg.T
    b_blocks = jnp.concatenate([jnp.where(col_seq == s, bgt, 0.0) for s in range(ns)], axis=0).astype(BF16)
    st = jnp.dot(b_blocks, xw, preferred_element_type=F32)
    dec_rows = jnp.concatenate(dec_parts, axis=1)
    for s in range(ns):
        hn_ref[s] = h0_ref[s] * dec_rows[s * seq_len:s * seq_len + 1, :] + st[s * n:(s + 1) * n, :]

    zz = z_ref[...]
    y = jnp.concatenate(y_parts, axis=1) * (zz * _sigmoid(zz))
    y_ref[...] = y * lax.rsqrt(jnp.mean(y * y, axis=-1, keepdims=True) + EPS) * nw_ref[0]


def _by_group(a):
    gw = M2_WIDTH // M2_GROUPS
    n = M2_D_STATE
    return jnp.stack([jnp.concatenate([a[..., g * gw:(g + 1) * gw],
                                       a[..., M2_WIDTH + g * n:M2_WIDTH + (g + 1) * n],
                                       a[..., M2_WIDTH + (M2_GROUPS + g) * n:M2_WIDTH + (M2_GROUPS + g + 1) * n]],
                                      axis=-1) for g in range(M2_GROUPS)])


def _ssd_packed(xbc, z, dt, conv0, h0, prm, n_seq, seq_len):
    cw, cb, dtb, alog, dexp, nw = prm
    t = xbc.shape[0]
    hpg = M2_HEADS // M2_GROUPS
    gw = hpg * M2_HEAD_DIM
    cdim = gw + 2 * M2_D_STATE
    heads = lambda a: jnp.stack([jnp.pad(a[..., g * hpg:(g + 1) * hpg], [(0, 0)] * (a.ndim - 1) + [(0, LANES - hpg)])
                                 for g in range(M2_GROUPS)])
    halves = lambda a: jnp.stack([a[..., g * gw:(g + 1) * gw] for g in range(M2_GROUPS)])
    hist = jnp.stack([_by_group(jnp.pad(conv0[:, M2_CONV - 1 - k:, :], ((0, 0), (0, seq_len - k), (0, 0)))
                                .reshape(t, M2_CONV_DIM)) for k in range(1, M2_CONV)])
    per_g = lambda r, c: pl.BlockSpec((1, r, c), lambda i, g: (g, 0, 0))
    rows_g = lambda c: pl.BlockSpec((1, SSD_CHUNK, c), lambda i, g: (g, i, 0))
    ns = SSD_CHUNK // seq_len
    state = pl.BlockSpec((ns, M2_D_STATE, gw), lambda i, g: (i, 0, g))
    y, hn = pl.pallas_call(
        functools.partial(_ssd_packed_kernel, seq_len=seq_len),
        grid=(t // SSD_CHUNK, M2_GROUPS),
        in_specs=[rows_g(cdim),
                  pl.BlockSpec((M2_CONV - 1, 1, SSD_CHUNK, cdim), lambda i, g: (0, g, i, 0)),
                  pl.BlockSpec((SSD_CHUNK, gw), lambda i, g: (i, g)),
                  rows_g(LANES), state,
                  per_g(M2_CONV, cdim), per_g(1, cdim), per_g(1, LANES), per_g(1, LANES), per_g(1, gw), per_g(1, gw)],
        out_specs=[pl.BlockSpec((SSD_CHUNK, gw), lambda i, g: (i, g)), state],
        out_shape=[jax.ShapeDtypeStruct((t, M2_WIDTH), F32), jax.ShapeDtypeStruct((n_seq, M2_D_STATE, M2_WIDTH), F32)],
        scratch_shapes=[pltpu.VMEM((CONV_HALO + SSD_CHUNK, cdim), F32)],
        compiler_params=_cparams("parallel", "parallel"),
        name="ssd_packed",
    )(_by_group(xbc), hist, z, heads(dt), h0, _by_group(cw), _by_group(cb), heads(dtb), heads(alog),
      halves(dexp), halves(nw))
    conv_new = xbc.reshape(n_seq, seq_len, M2_CONV_DIM)[:, seq_len - (M2_CONV - 1):, :]
    return y, conv_new, hn


def _post_kernel(y5_ref, u_ref, ym_ref, x_ref, d_ref, wglu_ref, s5nw_ref, wo1_ref, wo2_ref, fnw_ref, wqt_ref,
                 sk_ref, x1_ref, xt_ref, st_ref):
    g = _gelu_tanh(_load_blocked(y5_ref) + d_ref[...] * _load_blocked(u_ref))
    o = g * _sigmoid(jnp.dot(g.astype(BF16), wglu_ref[...], preferred_element_type=F32))
    y5n = _rms(o, s5nw_ref[...])
    x1 = (x_ref[...] + jnp.dot(y5n.astype(BF16), wo1_ref[...], preferred_element_type=F32)
          + jnp.dot(ym_ref[...].astype(BF16), wo2_ref[...], preferred_element_type=F32))
    x1_ref[...] = x1
    hn_t = _rms(x1, fnw_ref[...]).T.astype(BF16)
    xt_ref[...] = hn_t
    q_t = jnp.dot(wqt_ref[...], hn_t, preferred_element_type=F32)
    for k in range(2 * PEER_HEADS):
        qk = q_t[k * PEER_HALF:(k + 1) * PEER_HALF, :].astype(BF16)
        st_ref[k] = jnp.dot(sk_ref[k], qk, preferred_element_type=F32) * LOG2E


def _post(y5, u, ym, x, prm, tm, s5_block=None):
    t = x.shape[0]
    row = lambda i: (i, 0)
    s5_spec = pl.BlockSpec((tm, D_MODEL), row) if s5_block is None else pl.BlockSpec(*s5_block)
    fix = lambda i: (0, 0)
    nk = 2 * PEER_HEADS
    once = functools.partial(pl.BlockSpec, pipeline_mode=pl.Buffered(1))
    return pl.pallas_call(
        _post_kernel,
        grid=(t // tm,),
        in_specs=[s5_spec, s5_spec, pl.BlockSpec((tm, D_MODEL), row), pl.BlockSpec((tm, D_MODEL), row)]
        + [pl.BlockSpec((1, D_MODEL), fix), once((S5_WIDTH, S5_WIDTH), fix), pl.BlockSpec((1, D_MODEL), fix),
           once((S5_WIDTH, D_MODEL), fix), once((M2_WIDTH, D_MODEL), fix),
           pl.BlockSpec((1, D_MODEL), fix), once((nk * PEER_HALF, D_MODEL), fix),
           once((nk, PEER_N_KEYS, PEER_HALF), lambda i: (0, 0, 0))],
        out_specs=[pl.BlockSpec((tm, D_MODEL), row), pl.BlockSpec((D_MODEL, tm), lambda i: (0, i)),
                   pl.BlockSpec((nk, PEER_N_KEYS, tm), lambda i: (0, 0, i))],
        out_shape=[jax.ShapeDtypeStruct((t, D_MODEL), F32), jax.ShapeDtypeStruct((D_MODEL, t), BF16),
                   jax.ShapeDtypeStruct((nk, PEER_N_KEYS, t), F32)],
        compiler_params=_cparams("parallel"),
        name="post",
    )(y5, u, ym, x, *prm)


def _sorting_network(n):
    pairs = []
    p = 1
    while p < n:
        k = p
        while k >= 1:
            for j in range(k % p, n - k, 2 * k):
                for i in range(min(k, n - j - k)):
                    if (i + j) // (2 * p) == (i + j + k) // (2 * p):
                        pairs.append((i + j, i + j + k))
            k //= 2
        p *= 2
    return pairs


SUBLANES = 8
_NET16 = _sorting_network(PEER_N_KEYS // SUBLANES)
LOG2E = math.log2(math.e)


def _top16(tiles):
    n = len(tiles)
    tiles = list(tiles)

    def exchange(i, j):
        hi, lo = tiles[i], tiles[j]
        if lo is None:
            return
        if hi is None:
            tiles[i], tiles[j] = lo, None
        else:
            tiles[i], tiles[j] = jnp.maximum(hi, lo), jnp.minimum(hi, lo)

    def larger(a, b):
        return b if a is None else a if b is None else jnp.maximum(a, b)

    for i, j in _NET16:
        exchange(i, j)
    shift = SUBLANES // 2
    while shift >= 1:
        other = [None if t is None else pltpu.roll(t, shift, axis=0) for t in tiles]
        tiles = [larger(tiles[i], other[n - 1 - i]) for i in range(n)]
        dist = n // 2
        while dist >= 1:
            for i in range(n):
                if i & dist == 0:
                    exchange(i, i + dist)
            dist //= 2
        shift //= 2
    return tiles


def _top_values(s):
    n = PEER_N_KEYS // SUBLANES
    tiles = _top16([s[i * SUBLANES:(i + 1) * SUBLANES, :] for i in range(n)])
    rows = lax.broadcasted_iota(jnp.int32, (PEER_TOPK, s.shape[1]), 0)
    out = jnp.zeros((PEER_TOPK, s.shape[1]), F32)
    for r in range(PEER_TOPK):
        out = jnp.where(rows == r, jnp.concatenate([tiles[r], tiles[r]], axis=0), out)
    return out


def _pair_candidates(v1, v2):
    r8 = lax.broadcasted_iota(jnp.int32, (SUBLANES, v1.shape[1]), 0)
    r16 = lax.broadcasted_iota(jnp.int32, (PEER_TOPK, v1.shape[1]), 0)
    neg = -jnp.inf
    lo2 = v2[0:SUBLANES, :]
    return jnp.concatenate([
        v1[0:1, :] + v2,
        v1[1:2, :] + lo2,
        jnp.where(r16 >= 2, v1 + v2[0:1, :], neg),
        jnp.where(r8 >= 2, v1[0:SUBLANES, :] + v2[1:2, :], neg),
        jnp.where((r8 >= 2) & (r8 <= 4), v1[2:3, :] + lo2, neg),
        jnp.where((r8 >= 2) & (r8 <= 3), v1[3:4, :] + lo2, neg),
        jnp.where(r8 == 2, v1[4:5, :] + lo2, neg),
    ], axis=0)


def _route_kernel(st_ref, s1n_ref, s2n_ref, thr_ref):
    def head(h, carry):
        s1 = st_ref[2 * h]
        s2 = st_ref[2 * h + 1]
        v1 = _top_values(s1)
        v2 = _top_values(s2)
        cand = _pair_candidates(v1, v2)
        n_cand = cand.shape[0] // SUBLANES
        pad = [None] * (PEER_N_KEYS // SUBLANES - n_cand)
        ranked = _top16([cand[i * SUBLANES:(i + 1) * SUBLANES, :] for i in range(n_cand)] + pad)
        theta = ranked[PEER_TOPK - 1][0:1, :]
        sel = cand >= theta
        m = v1[0:1, :] + v2[0:1, :]
        zsum = jnp.sum(jnp.where(sel, jnp.exp2(cand - m), 0.0), axis=0, keepdims=True)
        off = m + jnp.log(zsum) * LOG2E + 1.0
        s1n_ref[h] = s1 - off
        s2n_ref[h] = s2
        candn = _pair_candidates(v1 - off, v2)
        thr_ref[h] = jnp.min(jnp.where(sel, candn, jnp.inf), axis=0, keepdims=True)
        return carry

    lax.fori_loop(0, PEER_HEADS, head, 0)


def _route(st, tl):
    nk, keys, t = st.shape
    spec = lambda n: pl.BlockSpec((n, keys, tl), lambda i: (0, 0, i))
    return pl.pallas_call(
        _route_kernel,
        grid=(t // tl,),
        in_specs=[spec(nk)],
        out_specs=[spec(PEER_HEADS), spec(PEER_HEADS), pl.BlockSpec((PEER_HEADS, 1, tl), lambda i: (0, 0, i))],
        out_shape=[jax.ShapeDtypeStruct((PEER_HEADS, keys, t), F32), jax.ShapeDtypeStruct((PEER_HEADS, keys, t), F32),
                   jax.ShapeDtypeStruct((PEER_HEADS, 1, t), F32)],
        compiler_params=_cparams("parallel"),
        name="route",
    )(st)


PEER_E1_BLK = 16
PEER_CHUNK_KEYS = (4, 4, 4, 4)
assert sum(PEER_CHUNK_KEYS) == PEER_E1_BLK


def _peer_kernel(u_ref, vt_ref, xt_ref, s2n_ref, s1n_ref, thr_ref, x1_ref, fnw_ref, y_ref, acc_scr, *, tm):
    k = pl.program_id(1)
    c0 = math.sqrt(2.0 / math.pi)
    c1 = c0 * 0.044715
    first_key = [sum(PEER_CHUNK_KEYS[:q]) for q in range(len(PEER_CHUNK_KEYS))]

    def experts(q):
        return slice(first_key[q] * PEER_N_KEYS, (first_key[q] + PEER_CHUNK_KEYS[q]) * PEER_N_KEYS)

    def scores(q):
        return jnp.dot(u_ref[experts(q), :], xt_ref[...], preferred_element_type=F32)

    def gated(q, a):
        wg_rows = []
        for i in range(PEER_CHUNK_KEYS[q]):
            r = first_key[q] + i
            rows = slice(i * PEER_N_KEYS, (i + 1) * PEER_N_KEYS)
            wg_cols = []
            for c in range(tm // LANES):
                cols = slice(c * LANES, (c + 1) * LANES)
                w = None
                for h in range(PEER_HEADS):
                    arg = s2n_ref[h, :, cols] + s1n_ref[h, r:r + 1, cols]
                    term = jnp.where(arg >= thr_ref[h, :, cols], jnp.exp2(arg), 0.0)
                    w = term if w is None else w + term
                x = a[rows, cols]
                g = x + x * jnp.tanh(x * (c0 + c1 * (x * x)))
                wg_cols.append((w * g).astype(BF16))
            wg_rows.append(jnp.concatenate(wg_cols, axis=1))
        return jnp.concatenate(wg_rows, axis=0)

    def mixed(q, wg):
        return jnp.dot(vt_ref[0, :, experts(q)], wg, preferred_element_type=F32)

    n_chunks = len(PEER_CHUNK_KEYS)
    total = None
    a_next = scores(0)
    wg_prev = None
    for q in range(n_chunks):
        a_cur = a_next
        if q + 1 < n_chunks:
            a_next = scores(q + 1)
        wg = gated(q, a_cur)
        if wg_prev is not None:
            d = mixed(q - 1, wg_prev)
            total = d if total is None else total + d
        wg_prev = wg
    total = total + mixed(n_chunks - 1, wg_prev)

    @pl.when(k == 0)
    def _():
        acc_scr[...] = total

    @pl.when(k > 0)
    def _():
        acc_scr[...] += total

    @pl.when(k == pl.num_programs(1) - 1)
    def _():
        y_ref[...] = _rms(x1_ref[...] + acc_scr[...].T, fnw_ref[...])


def _peer(u_bf, vt_bf, xt, route, x1, fnw, tm):
    s1n, s2n, thr = route
    t = x1.shape[0]
    eb = PEER_E1_BLK * PEER_N_KEYS
    per_key = pl.BlockSpec((PEER_HEADS, PEER_N_KEYS, tm), lambda j, k: (0, 0, j))
    per_blk = pl.BlockSpec((PEER_HEADS, PEER_E1_BLK, tm), lambda j, k: (0, k, j))
    return pl.pallas_call(
        functools.partial(_peer_kernel, tm=tm),
        grid=(t // tm, PEER_N_KEYS // PEER_E1_BLK),
        in_specs=[pl.BlockSpec((eb, D_MODEL), lambda j, k: (k, 0)),
                  pl.BlockSpec((1, D_MODEL, eb), lambda j, k: (k, 0, 0)),
                  pl.BlockSpec((D_MODEL, tm), lambda j, k: (0, j)),
                  per_key, per_blk, pl.BlockSpec((PEER_HEADS, 1, tm), lambda j, k: (0, 0, j)),
                  pl.BlockSpec((tm, D_MODEL), lambda j, k: (j, 0)),
                  pl.BlockSpec((1, D_MODEL), lambda j, k: (0, 0))],
        out_specs=pl.BlockSpec((tm, D_MODEL), lambda j, k: (j, 0)),
        out_shape=jax.ShapeDtypeStruct((t, D_MODEL), F32),
        scratch_shapes=[pltpu.VMEM((D_MODEL, tm), F32)],
        compiler_params=_cparams("parallel", "arbitrary"),
        name="peer",
    )(u_bf, vt_bf, xt, s2n, s1n, thr, x1, fnw)


TOKEN_TILE = 512


def _tokens_tail(x, post_prm, peer_prm, y5, u, ym, s5_block):
    x1, xt, st = _post(y5, u, ym, x, post_prm, tm=TOKEN_TILE, s5_block=s5_block)
    u_bf, vt_bf, fnw = peer_prm
    return _peer(u_bf, vt_bf, xt, _route(st, tl=2 * LANES), x1, fnw, tm=TOKEN_TILE)


def kernel(x_prompt, x_sample, state_s5_re, state_s5_im, state_ssm, state_conv, meta_tokens, norm_mix_w, w_in,
           s5_lambda_re, s5_lambda_im, s5_log_step, s5_b_re, s5_b_im, s5_c_re, s5_c_im, s5_d, s5_w_glu, s5_norm_w,
           m2_conv_w, m2_conv_b, m2_dt_bias, m2_a_log, m2_d, m2_norm_w, w_out, norm_ffn_w, peer_w_q, peer_sub_keys,
           peer_u, peer_v, final_norm_w):
    bp, sp, _ = x_prompt.shape
    bs, ss, _ = x_sample.shape
    g, p = S5_GROUPS, S5_STATE

    w = w_in[0]
    o1, o2, o3 = S5_WIDTH, S5_WIDTH + M2_WIDTH, S5_WIDTH + M2_WIDTH + M2_CONV_DIM
    wu, wz, wx = w[:, :o1].astype(BF16), w[:, o1:o2].astype(BF16), w[:, o2:o3].astype(BF16)
    wd = jnp.pad(w[:, o3:], ((0, 0), (0, LANES - M2_HEADS))).astype(BF16)
    nmw = norm_mix_w[0][None, :]
    pad_h = lambda v: jnp.pad(v, (0, LANES - M2_HEADS))[None, :]
    ssd_prm = (m2_conv_w[0], m2_conv_b[0][None, :], pad_h(m2_dt_bias[0]), pad_h(m2_a_log[0]),
               jnp.repeat(m2_d[0], M2_HEAD_DIM)[None, :], m2_norm_w[0][None, :])
    post_prm = (s5_d[0][None, :], s5_w_glu[0].astype(BF16), s5_norm_w[0][None, :],
                w_out[0][:S5_WIDTH].astype(BF16), w_out[0][S5_WIDTH:].astype(BF16), norm_ffn_w[0][None, :],
                peer_w_q[0].T.astype(BF16),
                peer_sub_keys[0].reshape(2 * PEER_HEADS, PEER_N_KEYS, PEER_HALF).astype(BF16))
    eb = PEER_E1_BLK * PEER_N_KEYS
    vt_blocks = peer_v[0].astype(BF16).reshape(PEER_EXPERTS // eb, eb, D_MODEL).swapaxes(1, 2)
    peer_prm = (peer_u[0].astype(BF16), vt_blocks, final_norm_w[None, :])
    s5_args = (s5_lambda_re[0], s5_lambda_im[0], s5_log_step[0], s5_b_re[0], s5_b_im[0], s5_c_re[0], s5_c_im[0])
    ops16 = _s5_prep(*s5_args, lc=N_META)
    ops8 = _s5_prep(*s5_args, lc=ss)

    lc = N_META
    um, _, xbcm, dtm = _in_proj(meta_tokens, nmw, wu, wz, wx, wd, tm=N_META)
    nb = S5_WIDTH // LANES
    urm = jnp.broadcast_to(um.reshape(lc, nb, LANES).transpose(1, 0, 2).reshape(nb, 1, lc * LANES),
                           (nb, SUBLANES, lc * LANES))
    _, h5m = _s5(urm, jnp.zeros((g, SUBLANES, 2 * p), F32), ops16, lc=lc, rows=SUBLANES, chunks=1)
    _, convm, hm = _ssd(xbcm, jnp.zeros((N_META, M2_WIDTH), F32), dtm,
                        jnp.zeros((1, M2_CONV - 1, M2_CONV_DIM), F32), jnp.zeros((1, M2_D_STATE, M2_WIDTH), F32),
                        ssd_prm, n_seq=1, n_chunks=1, q_in=N_META, shared_init=True)

    xp = x_prompt.reshape(bp * sp, D_MODEL)
    nc = sp // lc
    cpt = TOKEN_TILE // lc
    tps = sp // TOKEN_TILE
    p_shape = (nb, nc, bp, lc, LANES)
    p_block = ((nb, cpt, 1, lc, LANES), lambda i: (0, i % tps, i // tps, 0, 0))
    up, zp, xbcp, dtp = _in_proj(xp, nmw, wu, wz, wx, wd, tm=TOKEN_TILE, u_layout=(p_shape,) + p_block)
    y5p, h5p = _s5(up.reshape(nb, nc * bp * lc, LANES), h5m, ops16, lc=lc, rows=bp, chunks=S5_ROW_TILE // bp)
    ymp, convp, hp = _ssd(xbcp, zp, dtp, convm, hm, ssd_prm, n_seq=bp, n_chunks=sp // SSD_CHUNK, q_in=SSD_CHUNK,
                          shared_init=True)
    y_prompt = _tokens_tail(xp, post_prm, peer_prm, y5p.reshape(p_shape), up, ymp, p_block).reshape(bp, sp, D_MODEL)

    xs = x_sample.reshape(bs * ss, D_MODEL)
    s_shape = (nb, bs, ss, LANES)
    s_block = ((nb, TOKEN_TILE // ss, ss, LANES), lambda i: (0, i, 0, 0))
    us, zs, xbcs, dts = _in_proj(xs, nmw, wu, wz, wx, wd, tm=TOKEN_TILE, u_layout=(s_shape,) + s_block)
    h5s0 = jnp.concatenate([state_s5_re[0], state_s5_im[0]], axis=-1).transpose(1, 0, 2)
    y5s, h5s = _s5(us.reshape(nb, bs * ss, LANES), h5s0, ops8, lc=ss, rows=bs, chunks=1)
    hs0 = state_ssm[0].reshape(bs, M2_WIDTH, M2_D_STATE).transpose(0, 2, 1)
    yms, convs, hs = _ssd_packed(xbcs, zs, dts, state_conv[0], hs0, ssd_prm, n_seq=bs, seq_len=ss)
    y_sample = _tokens_tail(xs, post_prm, peer_prm, y5s.reshape(s_shape), us, yms, s_block).reshape(bs, ss, D_MODEL)

    def s5_state(hf):
        hf = hf.transpose(1, 0, 2)
        return hf[None, :, :, :p], hf[None, :, :, p:]

    def ssm_state(ht):
        return ht.transpose(0, 2, 1).reshape(1, ht.shape[0], M2_HEADS, M2_HEAD_DIM, M2_D_STATE)

    p5r, p5i = s5_state(h5p)
    s5r, s5i = s5_state(h5s)
    return (y_prompt, y_sample, p5r, p5i, ssm_state(hp), convp[None], s5r, s5i, ssm_state(hs), convs[None])
```

```python
import functools
import math

import jax
import jax.numpy as jnp
from jax import lax
from jax.experimental import pallas as pl
from jax.experimental.pallas import tpu as pltpu

F32 = jnp.float32
BF16 = jnp.bfloat16
HIGHEST = lax.Precision.HIGHEST

D_MODEL = 1024
N_META = 16
S5_WIDTH = 1024
S5_CH = 16
S5_GROUPS = S5_WIDTH // S5_CH
S5_STATE = 64
M2_WIDTH = 1024
M2_HEAD_DIM = 64
M2_HEADS = M2_WIDTH // M2_HEAD_DIM
M2_GROUPS = 2
M2_D_STATE = 128
M2_CONV = 4
M2_CONV_DIM = M2_WIDTH + 2 * M2_GROUPS * M2_D_STATE
PEER_HEADS = 8
PEER_N_KEYS = 128
PEER_EXPERTS = PEER_N_KEYS * PEER_N_KEYS
PEER_HALF = 128
PEER_TOPK = 16
EPS = 1e-6

LANES = 128
SSD_CHUNK = 128
CONV_HALO = 8
VMEM_LIMIT = 56 * 1024 * 1024


def _cparams(*sem):
    return pltpu.CompilerParams(dimension_semantics=sem, vmem_limit_bytes=VMEM_LIMIT)


def _sigmoid(x):
    return 1.0 / (1.0 + jnp.exp(-x))


def _gelu_tanh(x):
    c = math.sqrt(2.0 / math.pi)
    return 0.5 * x * (1.0 + jnp.tanh(c * (x + 0.044715 * (x * x * x))))


def _rms(x, w):
    return x * lax.rsqrt(jnp.mean(x * x, axis=-1, keepdims=True) + EPS) * w


def _bdot(a, b):
    return jnp.dot(a.astype(BF16), b.astype(BF16), preferred_element_type=F32)


def _store_blocked(ref, val):
    if len(ref.shape) == 2:
        ref[...] = val
        return
    for gb in range(ref.shape[0]):
        piece = val[:, gb * LANES:(gb + 1) * LANES]
        if len(ref.shape) == 5:
            ref[gb, :, 0] = piece.reshape(ref.shape[1], ref.shape[3], LANES)
        else:
            ref[gb] = piece.reshape(ref.shape[1:])


def _load_blocked(ref):
    if len(ref.shape) == 2:
        return ref[...]
    parts = []
    for gb in range(ref.shape[0]):
        piece = ref[gb, :, 0] if len(ref.shape) == 5 else ref[gb]
        parts.append(piece.reshape(piece.shape[0] * piece.shape[1], LANES))
    return jnp.concatenate(parts, axis=1)


def _in_proj_kernel(x_ref, nw_ref, wu_ref, wz_ref, wx_ref, wd_ref, u_ref, z_ref, xbc_ref, dt_ref):
    hb = _rms(x_ref[...], nw_ref[...]).astype(BF16)
    _store_blocked(u_ref, jnp.dot(hb, wu_ref[...], preferred_element_type=F32))
    z_ref[...] = jnp.dot(hb, wz_ref[...], preferred_element_type=F32)
    xbc_ref[...] = jnp.dot(hb, wx_ref[...], preferred_element_type=F32)
    dt_ref[...] = jnp.dot(hb, wd_ref[...], preferred_element_type=F32)


def _in_proj(x, nw, wu, wz, wx, wd, tm, u_layout=None):
    t = x.shape[0]
    row = lambda i: (i, 0)
    fix = lambda i: (0, 0)
    widths = (S5_WIDTH, M2_WIDTH, M2_CONV_DIM, LANES)
    out_specs = [pl.BlockSpec((tm, w), row) for w in widths]
    out_shape = [jax.ShapeDtypeStruct((t, w), F32) for w in widths]
    if u_layout is not None:
        out_specs[0] = pl.BlockSpec(u_layout[1], u_layout[2])
        out_shape[0] = jax.ShapeDtypeStruct(u_layout[0], F32)
    return pl.pallas_call(
        _in_proj_kernel,
        grid=(t // tm,),
        in_specs=[pl.BlockSpec((tm, D_MODEL), row), pl.BlockSpec((1, D_MODEL), fix)]
        + [pl.BlockSpec((D_MODEL, w), fix) for w in widths],
        out_specs=out_specs,
        out_shape=out_shape,
        compiler_params=_cparams("parallel"),
        name="in_proj",
    )(x, nw, wu, wz, wx, wd)


def _s5_prep_kernel(lrc_ref, lic_ref, lrr_ref, lir_ref, ls_ref, btr_ref, bti_ref, ctr_ref, cti_ref,
                    kt_ref, wout_ref, wsr_ref, wsi_ref, al_ref, *, lc):
    for j in range(S5_GBLK):
        _s5_prep_group(j, lc, lrc_ref, lic_ref, lrr_ref, lir_ref, ls_ref, btr_ref, bti_ref, ctr_ref, cti_ref,
                       kt_ref, wout_ref, wsr_ref, wsi_ref, al_ref)


def _s5_prep_group(j, lc, lrc_ref, lic_ref, lrr_ref, lir_ref, ls_ref, btr_ref, bti_ref, ctr_ref, cti_ref,
                   kt_ref, wout_ref, wsr_ref, wsi_ref, al_ref):
    k = lc * S5_CH
    step = jnp.exp(ls_ref[j])

    def disc(lr, li):
        lg = lr * step
        th = li * step
        mag = jnp.exp(lg)
        ab_re = mag * jnp.cos(th)
        ab_im = mag * jnp.sin(th)
        den = lr * lr + li * li
        f_re = ((ab_re - 1.0) * lr + ab_im * li) / den
        f_im = (ab_im * lr - (ab_re - 1.0) * li) / den
        return ab_re, ab_im, f_re, f_im

    def powers(n, a_re_, a_im_):
        p_re_ = jnp.ones(n.shape, F32)
        p_im_ = jnp.zeros(n.shape, F32)
        b_re_, b_im_ = a_re_, a_im_
        bit = 1
        while bit <= lc:
            use = (n & bit) != 0
            t_re = p_re_ * b_re_ - p_im_ * b_im_
            t_im = p_re_ * b_im_ + p_im_ * b_re_
            p_re_ = jnp.where(use, t_re, p_re_)
            p_im_ = jnp.where(use, t_im, p_im_)
            b_re_, b_im_ = b_re_ * b_re_ - b_im_ * b_im_, 2.0 * (b_re_ * b_im_)
            bit *= 2
        return p_re_, p_im_

    ac_re, ac_im, f_re, f_im = disc(lrc_ref[j], lic_ref[j])
    tau = lax.shift_right_logical(lax.broadcasted_iota(jnp.int32, (S5_STATE, k), 1), 4)
    c_re = ctr_ref[j]
    c_im = cti_ref[j]
    p_re, p_im = powers(tau, ac_re, ac_im)
    fa_re = f_re * p_re - f_im * p_im
    fa_im = f_re * p_im + f_im * p_re
    fca_re = fa_re * c_re - fa_im * c_im
    fca_im = fa_re * c_im + fa_im * c_re
    kt_ref[j] = (jnp.dot(btr_ref[j], fca_re, precision=HIGHEST, preferred_element_type=F32)
                 - jnp.dot(bti_ref[j], fca_im, precision=HIGHEST, preferred_element_type=F32))
    q_re, q_im = powers(tau + 1, ac_re, ac_im)
    wout_ref[j, 0:S5_STATE, :] = q_re * c_re - q_im * c_im
    wout_ref[j, S5_STATE:2 * S5_STATE, :] = -(q_re * c_im + q_im * c_re)

    ar_re, ar_im, fr_re, fr_im = disc(lrr_ref[j], lir_ref[j])
    kk = lax.broadcasted_iota(jnp.int32, (lc, S5_STATE), 0)
    r_re, r_im = powers(kk, ar_re, ar_im)
    g_re = fr_re * r_re - fr_im * r_im
    g_im = fr_re * r_im + fr_im * r_re
    bt_re = btr_ref[j]
    bt_im = bti_ref[j]
    for s in range(lc):
        kpow = lc - 1 - s
        w_re = g_re[kpow:kpow + 1, :]
        w_im = g_im[kpow:kpow + 1, :]
        wsr_ref[j, s * S5_CH:(s + 1) * S5_CH, :] = bt_re * w_re - bt_im * w_im
        wsi_ref[j, s * S5_CH:(s + 1) * S5_CH, :] = bt_re * w_im + bt_im * w_re
    a_re, a_im = powers(jnp.full((1, S5_STATE), lc, jnp.int32), ar_re, ar_im)
    al_ref[j, 0:1, :] = a_re
    al_ref[j, 1:2, :] = a_im


def _s5_prep(lam_re, lam_im, log_step, b_re, b_im, c_re, c_im, lc):
    g, p, k = S5_GROUPS, S5_STATE, lc * S5_CH
    spec3 = lambda a, b: pl.BlockSpec((S5_GBLK, a, b), lambda i: (i, 0, 0))
    ins = [lam_re.reshape(g, p, 1), lam_im.reshape(g, p, 1), lam_re.reshape(g, 1, p), lam_im.reshape(g, 1, p),
           log_step.reshape(g, 1, 1),
           jnp.swapaxes(b_re, 1, 2), jnp.swapaxes(b_im, 1, 2),
           jnp.tile(jnp.swapaxes(c_re, 1, 2), (1, 1, lc)), jnp.tile(jnp.swapaxes(c_im, 1, 2), (1, 1, lc))]
    kt, wout, wsr, wsi, al = pl.pallas_call(
        functools.partial(_s5_prep_kernel, lc=lc),
        grid=(g // S5_GBLK,),
        in_specs=[spec3(p, 1), spec3(p, 1), spec3(1, p), spec3(1, p), spec3(1, 1),
                  spec3(S5_CH, p), spec3(S5_CH, p), spec3(p, k), spec3(p, k)],
        out_specs=[spec3(S5_CH, k), spec3(2 * p, k), spec3(k, p), spec3(k, p), spec3(2, p)],
        out_shape=[jax.ShapeDtypeStruct((g, S5_CH, k), F32), jax.ShapeDtypeStruct((g, 2 * p, k), F32),
                   jax.ShapeDtypeStruct((g, k, p), F32), jax.ShapeDtypeStruct((g, k, p), F32),
                   jax.ShapeDtypeStruct((g, 2, p), F32)],
        compiler_params=_cparams("parallel"),
        name="s5_prep",
    )(*ins)
    lag = jnp.arange(lc)[None, :] - jnp.arange(lc)[:, None]
    kt4 = kt.reshape(g, S5_CH, lc, S5_CH)
    toep = kt4[:, :, jnp.maximum(lag, 0), :]
    toep = jnp.where((lag >= 0)[None, None, :, :, None], toep, 0.0)
    toep = toep.transpose(0, 2, 1, 3, 4).reshape(g, k, k)
    wst = jnp.concatenate([wsr, wsi, wsi, wsr], axis=-1)
    a_re, a_im = al[:, 0:1, :], al[:, 1:2, :]
    acoef = jnp.concatenate([jnp.concatenate([a_re, a_re], -1), jnp.concatenate([-a_im, a_im], -1),
                             jnp.concatenate([a_im, -a_im], -1)], axis=1)
    return toep.astype(BF16), wst.astype(BF16), wout.astype(BF16), acoef


S5_GBLK = LANES // S5_CH
S5_ROW_TILE = 256


def _block_transpose(cols):
    n = len(cols)
    blk = lax.shift_right_logical(lax.broadcasted_iota(jnp.int32, (1, LANES), 1), 4)
    rolled = []
    for d in range(n):
        m = cols[d]
        for b in range(1, n):
            m = jnp.where(blk == b, cols[(b + d) % n], m)
        rolled.append(m if d == 0 else pltpu.roll(m, d * S5_CH, axis=1))
    outs = []
    for j in range(n):
        o = rolled[(-j) % n]
        for s in range(1, n):
            o = jnp.where(blk == s, rolled[(s - j) % n], o)
        outs.append(o)
    return outs


def _s5_kernel(u_ref, h0_ref, h0s_ref, wst_ref, toep_ref, wout_ref, a_ref, y_ref, hf_ref,
               s_scr, hin_scr, h_scr, hs_scr, *, lc, rows, chunks, tall):
    i = pl.program_id(1)

    @pl.when(i == 0)
    def _():
        h_scr[...] = h0_ref[...]
        hs_scr[...] = h0s_ref[...]

    rt = rows * chunks
    if tall:
        token = lambda s: u_ref[0, pl.ds(s, rt, stride=lc), :]
    else:
        x = u_ref[0]
        token = lambda s: x[:, s * LANES:(s + 1) * LANES]
    halves = [_block_transpose([token(S5_GBLK * c + s) for s in range(S5_GBLK)]) for c in range(lc // S5_GBLK)]
    ubs = []
    for j in range(S5_GBLK):
        ubs.append(jnp.concatenate([hv[j] for hv in halves], axis=1).astype(BF16))
        s_scr[j] = jnp.dot(ubs[j], wst_ref[j], preferred_element_type=F32)

    def step(r0, carry):
        new = []
        for j in range(S5_GBLK):
            h, hs = carry[2 * j], carry[2 * j + 1]
            hin_scr[j, pl.ds(r0, rows), :] = h
            s = s_scr[j, pl.ds(r0, rows), :]
            a1, a2, a2s = a_ref[j, 0:1, :], a_ref[j, 1:2, :], a_ref[j, 2:3, :]
            new.append(a1 * h + a2 * hs + s[:, :LANES])
            new.append(a1 * hs + a2s * h + s[:, LANES:])
        return tuple(new)

    carry = tuple(ref[j] for j in range(S5_GBLK) for ref in (h_scr, hs_scr))
    if chunks == 1:
        carry = step(0, carry)
    else:
        carry = lax.fori_loop(0, chunks, lambda n, c: step(pl.multiple_of(n * rows, rows), c), carry)
    for j in range(S5_GBLK):
        h_scr[j] = carry[2 * j]
        hs_scr[j] = carry[2 * j + 1]
    hf_ref[...] = h_scr[...]

    ys = [jnp.dot(ubs[j], toep_ref[j], preferred_element_type=F32)
          + jnp.dot(hin_scr[j].astype(BF16), wout_ref[j], preferred_element_type=F32) for j in range(S5_GBLK)]
    for c in range(lc // S5_GBLK):
        token_major = _block_transpose([ys[j][:, c * LANES:(c + 1) * LANES] for j in range(S5_GBLK)])
        for t in range(S5_GBLK):
            if tall:
                y_ref[0, pl.ds(S5_GBLK * c + t, rt, stride=lc), :] = token_major[t]
            else:
                y_ref[0, :, (S5_GBLK * c + t) * LANES:(S5_GBLK * c + t + 1) * LANES] = token_major[t]


def _s5(ub, h0, ops, lc, rows, chunks):
    toep, wst, wout, acoef = ops
    tall = ub.shape[2] == LANES
    nb = ub.shape[0]
    nr = ub.shape[1] // lc if tall else ub.shape[1]
    k = lc * S5_CH
    rt = rows * chunks
    io_block = (1, rt * lc, LANES) if tall else (1, rt, lc * LANES)
    blk = lambda a, b: pl.BlockSpec((S5_GBLK, a, b), lambda gb, i: (gb, 0, 0))
    return pl.pallas_call(
        functools.partial(_s5_kernel, lc=lc, rows=rows, chunks=chunks, tall=tall),
        grid=(nb, nr // rt),
        in_specs=[pl.BlockSpec(io_block, lambda gb, i: (gb, i, 0)),
                  blk(rows, LANES), blk(rows, LANES), blk(k, 2 * LANES), blk(k, k), blk(LANES, k), blk(3, LANES)],
        out_specs=[pl.BlockSpec(io_block, lambda gb, i: (gb, i, 0)), blk(rows, LANES)],
        out_shape=[jax.ShapeDtypeStruct(ub.shape, F32), jax.ShapeDtypeStruct((S5_GROUPS, rows, LANES), F32)],
        scratch_shapes=[pltpu.VMEM((S5_GBLK, rt, 2 * LANES), F32), pltpu.VMEM((S5_GBLK, rt, LANES), F32),
                        pltpu.VMEM((S5_GBLK, rows, LANES), F32), pltpu.VMEM((S5_GBLK, rows, LANES), F32)],
        compiler_params=_cparams("parallel", "arbitrary"),
        name="s5_scan",
    )(ub, h0, jnp.roll(h0, S5_STATE, axis=-1), wst, toep, wout, acoef)


def _ssd_kernel(xbc_ref, z_ref, dt_ref, conv0_ref, h0_ref, cw_ref, cb_ref, dtb_ref, alog_ref, dexp_ref, nw_ref,
                y_ref, convn_ref, hn_ref, buf_scr, dt_scr, h_scr, *, q_in, n_chunks):
    q = SSD_CHUNK
    pad = q - q_in
    first = CONV_HALO + pad
    c = pl.program_id(1)

    @pl.when(c == 0)
    def _():
        buf_scr[0:first, :] = jnp.zeros((first, M2_CONV_DIM), F32)
        buf_scr[first - (M2_CONV - 1):first, :] = conv0_ref[0]
        h_scr[...] = h0_ref[0]

    buf_scr[first:CONV_HALO + q, :] = xbc_ref[...]
    conv = cb_ref[...]
    for kk in range(M2_CONV):
        lo = CONV_HALO - (M2_CONV - 1) + kk
        conv = conv + cw_ref[kk:kk + 1, :] * buf_scr[lo:lo + q, :]
    convn_ref[0] = buf_scr[CONV_HALO + q - (M2_CONV - 1):CONV_HALO + q, :]
    if n_chunks > 1:
        buf_scr[0:CONV_HALO, :] = buf_scr[q:q + CONV_HALO, :]
    act = conv * _sigmoid(conv)
    xs = act[:, :M2_WIDTH]

    dtv = dt_ref[...] + dtb_ref[...]
    dt_real = jnp.maximum(dtv, 0.0) + jnp.log1p(jnp.exp(-jnp.abs(dtv)))
    if pad:
        dt_scr[0:pad, :] = jnp.zeros((pad, LANES), F32)
        dt_scr[pad:q, :] = dt_real
        dt = dt_scr[...]
    else:
        dt = dt_real
    a_neg = -jnp.exp(alog_ref[...])
    ri = lax.broadcasted_iota(jnp.int32, (q, q), 0)
    ci = lax.broadcasted_iota(jnp.int32, (q, q), 1)
    tril = ri >= ci
    acs = jnp.dot(tril.astype(F32), dt * a_neg, precision=HIGHEST, preferred_element_type=F32)
    acs_t = acs.T
    dt_t = dt.T
    acs_last = acs[q - 1:q, :]
    eacs = jnp.exp(acs)
    wdec = jnp.exp(acs_last - acs) * dt
    dec_last = jnp.exp(acs_last)
    low = lax.broadcasted_iota(jnp.int32, (q, LANES), 1) < M2_HEAD_DIM
    low1 = low[0:1, :]

    def pair_cols(m, h0):
        return jnp.where(low[0:m.shape[0], :], m[:, h0:h0 + 1], m[:, h0 + 1:h0 + 2])

    y_parts = []
    hpg = M2_HEADS // M2_GROUPS
    gw = hpg * M2_HEAD_DIM
    for g in range(M2_GROUPS):
        bg = act[:, M2_WIDTH + g * M2_D_STATE:M2_WIDTH + (g + 1) * M2_D_STATE]
        cg = act[:, M2_WIDTH + (M2_GROUPS + g) * M2_D_STATE:M2_WIDTH + (M2_GROUPS + g + 1) * M2_D_STATE]
        cgb = cg.astype(BF16)
        cb = lax.dot_general(cgb, bg.astype(BF16), (((1,), (1,)), ((), ())), preferred_element_type=F32)
        hg = h_scr[:, g * gw:(g + 1) * gw]
        yoff = jnp.dot(cgb, hg.astype(BF16), preferred_element_type=F32)
        xw_parts = []
        dec_parts = []
        for pp in range(hpg // 2):
            h0 = g * hpg + 2 * pp
            lanes = slice(h0 * M2_HEAD_DIM, (h0 + 2) * M2_HEAD_DIM)
            w_pair = []
            for h in (h0, h0 + 1):
                seg = acs[:, h:h + 1] - acs_t[h:h + 1, :]
                dec = jnp.exp(jnp.where(tril, seg, -jnp.inf))
                w_pair.append((cb * dec * dt_t[h:h + 1, :]).astype(BF16))
            xp = xs[:, lanes]
            xbd = jnp.concatenate([jnp.where(low, xp, 0.0), jnp.where(low, 0.0, xp)], axis=0).astype(BF16)
            yd = jnp.dot(jnp.concatenate(w_pair, axis=1), xbd, preferred_element_type=F32)
            yo = yoff[:, 2 * pp * M2_HEAD_DIM:(2 * pp + 2) * M2_HEAD_DIM]
            y_parts.append(yd + yo * pair_cols(eacs, h0) + dexp_ref[:, lanes] * xp)
            xw_parts.append(xp * pair_cols(wdec, h0))
            dec_parts.append(jnp.where(low1, dec_last[:, h0:h0 + 1], dec_last[:, h0 + 1:h0 + 2]))
        xw = jnp.concatenate(xw_parts, axis=1).astype(BF16)
        st = jnp.dot(bg.T.astype(BF16), xw, preferred_element_type=F32)
        h_scr[:, g * gw:(g + 1) * gw] = hg * jnp.concatenate(dec_parts, axis=1) + st

    @pl.when(c == n_chunks - 1)
    def _():
        hn_ref[0] = h_scr[...]

    y = jnp.concatenate(y_parts, axis=1)[pad:, :]
    zz = z_ref[...]
    y = y * (zz * _sigmoid(zz))
    outs = []
    for g in range(M2_GROUPS):
        yg = y[:, g * gw:(g + 1) * gw]
        outs.append(yg * lax.rsqrt(jnp.mean(yg * yg, axis=-1, keepdims=True) + EPS))
    y_ref[...] = jnp.concatenate(outs, axis=1) * nw_ref[...]


def _ssd(xbc, z, dt, conv0, h0, prm, n_seq, n_chunks, q_in, shared_init):
    assert q_in == SSD_CHUNK or n_chunks == 1
    assert q_in >= CONV_HALO and q_in % CONV_HALO == 0
    t = xbc.shape[0]
    row = lambda b, c: (b * n_chunks + c, 0)
    fix = lambda b, c: (0, 0)
    init = (lambda b, c: (0, 0, 0)) if shared_init else (lambda b, c: (b, 0, 0))
    per_seq = lambda b, c: (b, 0, 0)
    return pl.pallas_call(
        functools.partial(_ssd_kernel, q_in=q_in, n_chunks=n_chunks),
        grid=(n_seq, n_chunks),
        in_specs=[pl.BlockSpec((q_in, M2_CONV_DIM), row), pl.BlockSpec((q_in, M2_WIDTH), row),
                  pl.BlockSpec((q_in, LANES), row),
                  pl.BlockSpec((1, M2_CONV - 1, M2_CONV_DIM), init),
                  pl.BlockSpec((1, M2_D_STATE, M2_WIDTH), init),
                  pl.BlockSpec((M2_CONV, M2_CONV_DIM), fix), pl.BlockSpec((1, M2_CONV_DIM), fix),
                  pl.BlockSpec((1, LANES), fix), pl.BlockSpec((1, LANES), fix),
                  pl.BlockSpec((1, M2_WIDTH), fix), pl.BlockSpec((1, M2_WIDTH), fix)],
        out_specs=[pl.BlockSpec((q_in, M2_WIDTH), row),
                   pl.BlockSpec((1, M2_CONV - 1, M2_CONV_DIM), per_seq),
                   pl.BlockSpec((1, M2_D_STATE, M2_WIDTH), per_seq)],
        out_shape=[jax.ShapeDtypeStruct((t, M2_WIDTH), F32),
                   jax.ShapeDtypeStruct((n_seq, M2_CONV - 1, M2_CONV_DIM), F32),
                   jax.ShapeDtypeStruct((n_seq, M2_D_STATE, M2_WIDTH), F32)],
        scratch_shapes=[pltpu.VMEM((CONV_HALO + SSD_CHUNK, M2_CONV_DIM), F32),
                        pltpu.VMEM((SSD_CHUNK, LANES), F32),
                        pltpu.VMEM((M2_D_STATE, M2_WIDTH), F32)],
        compiler_params=_cparams("parallel", "arbitrary"),
        name="ssd",
    )(xbc, z, dt, conv0, h0, *prm)


def _ssd_packed_kernel(xbc_ref, hist_ref, z_ref, dt_ref, h0_ref, cw_ref, cb_ref, dtb_ref, alog_ref, dexp_ref, nw_ref,
                       y_ref, hn_ref, buf_scr, *, seq_len):
    q = SSD_CHUNK
    n = M2_D_STATE
    ns = q // seq_len
    shift = seq_len.bit_length() - 1
    hpg = M2_HEADS // M2_GROUPS
    gw = hpg * M2_HEAD_DIM
    cdim = gw + 2 * n
    x = xbc_ref[0]
    buf_scr[0:CONV_HALO, :] = jnp.zeros((CONV_HALO, cdim), F32)
    buf_scr[CONV_HALO:CONV_HALO + q, :] = x
    pos = lax.broadcasted_iota(jnp.int32, (q, cdim), 0) & (seq_len - 1)
    conv = cb_ref[0] + cw_ref[0, M2_CONV - 1:M2_CONV, :] * x
    for k in range(1, M2_CONV):
        prev = jnp.where(pos >= k, buf_scr[CONV_HALO - k:CONV_HALO - k + q, :], hist_ref[k - 1, 0])
        conv = conv + cw_ref[0, M2_CONV - 1 - k:M2_CONV - k, :] * prev
    act = conv * _sigmoid(conv)
    xs = act[:, :gw]
    bg = act[:, gw:gw + n]
    cg = act[:, gw + n:]

    dtv = dt_ref[0] + dtb_ref[0]
    dt = jnp.maximum(dtv, 0.0) + jnp.log1p(jnp.exp(-jnp.abs(dtv)))
    a_neg = -jnp.exp(alog_ref[0])
    ri = lax.broadcasted_iota(jnp.int32, (q, q), 0)
    ci = lax.broadcasted_iota(jnp.int32, (q, q), 1)
    same = lax.shift_right_logical(ri, shift) == lax.shift_right_logical(ci, shift)
    causal = (ri >= ci) & same
    dta = dt * a_neg
    acs = jnp.dot(causal.astype(F32), dta, precision=HIGHEST, preferred_element_type=F32)
    tot = jnp.dot(same.astype(F32), dta, precision=HIGHEST, preferred_element_type=F32)
    acs_t = acs.T
    dt_t = dt.T
    eacs = jnp.exp(acs)
    wdec = jnp.exp(tot - acs) * dt
    dec_tot = jnp.exp(tot)
    low = lax.broadcasted_iota(jnp.int32, (q, LANES), 1) < M2_HEAD_DIM

    def pair_cols(m, h0):
        return jnp.where(low, m[:, h0:h0 + 1], m[:, h0 + 1:h0 + 2])

    cgb = cg.astype(BF16)
    cb = lax.dot_general(cgb, bg.astype(BF16), (((1,), (1,)), ((), ())), preferred_element_type=F32)
    row_seq = lax.shift_right_logical(lax.broadcasted_iota(jnp.int32, (q, n), 0), shift)
    c_blocks = jnp.concatenate([jnp.where(row_seq == s, cg, 0.0) for s in range(ns)], axis=1).astype(BF16)
    h_all = h0_ref[...].reshape(ns * n, gw)
    yoff = jnp.dot(c_blocks, h_all.astype(BF16), preferred_element_type=F32)
    y_parts, xw_parts, dec_parts = [], [], []
    for pp in range(hpg // 2):
        h0 = 2 * pp
        lanes = slice(h0 * M2_HEAD_DIM, (h0 + 2) * M2_HEAD_DIM)
        w_pair = []
        for h in (h0, h0 + 1):
            seg = acs[:, h:h + 1] - acs_t[h:h + 1, :]
            dec = jnp.exp(jnp.where(causal, seg, -jnp.inf))
            w_pair.append((cb * dec * dt_t[h:h + 1, :]).astype(BF16))
        xp = xs[:, lanes]
        xbd = jnp.concatenate([jnp.where(low, xp, 0.0), jnp.where(low, 0.0, xp)], axis=0).astype(BF16)
        yd = jnp.dot(jnp.concatenate(w_pair, axis=1), xbd, preferred_element_type=F32)
        y_parts.append(yd + yoff[:, lanes] * pair_cols(eacs, h0) + dexp_ref[0, :, lanes] * xp)
        xw_parts.append(xp * pair_cols(wdec, h0))
        dec_parts.append(pair_cols(dec_tot, h0))
    xw = jnp.concatenate(xw_parts, axis=1).astype(BF16)
    col_seq = lax.shift_right_logical(lax.broadcasted_iota(jnp.int32, (n, q), 1), shift)
    bgt = bg.T
    b_blocks = jnp.concatenate([jnp.where(col_seq == s, bgt, 0.0) for s in range(ns)], axis=0).astype(BF16)
    st = jnp.dot(b_blocks, xw, preferred_element_type=F32)
    dec_rows = jnp.concatenate(dec_parts, axis=1)
    for s in range(ns):
        hn_ref[s] = h0_ref[s] * dec_rows[s * seq_len:s * seq_len + 1, :] + st[s * n:(s + 1) * n, :]

    zz = z_ref[...]
    y = jnp.concatenate(y_parts, axis=1) * (zz * _sigmoid(zz))
    y_ref[...] = y * lax.rsqrt(jnp.mean(y * y, axis=-1, keepdims=True) + EPS) * nw_ref[0]


def _by_group(a):
    gw = M2_WIDTH // M2_GROUPS
    n = M2_D_STATE
    return jnp.stack([jnp.concatenate([a[..., g * gw:(g + 1) * gw],
                                       a[..., M2_WIDTH + g * n:M2_WIDTH + (g + 1) * n],
                                       a[..., M2_WIDTH + (M2_GROUPS + g) * n:M2_WIDTH + (M2_GROUPS + g + 1) * n]],
                                      axis=-1) for g in range(M2_GROUPS)])


def _ssd_packed(xbc, z, dt, conv0, h0, prm, n_seq, seq_len):
    cw, cb, dtb, alog, dexp, nw = prm
    t = xbc.shape[0]
    hpg = M2_HEADS // M2_GROUPS
    gw = hpg * M2_HEAD_DIM
    cdim = gw + 2 * M2_D_STATE
    heads = lambda a: jnp.stack([jnp.pad(a[..., g * hpg:(g + 1) * hpg], [(0, 0)] * (a.ndim - 1) + [(0, LANES - hpg)])
                                 for g in range(M2_GROUPS)])
    halves = lambda a: jnp.stack([a[..., g * gw:(g + 1) * gw] for g in range(M2_GROUPS)])
    hist = jnp.stack([_by_group(jnp.pad(conv0[:, M2_CONV - 1 - k:, :], ((0, 0), (0, seq_len - k), (0, 0)))
                                .reshape(t, M2_CONV_DIM)) for k in range(1, M2_CONV)])
    per_g = lambda r, c: pl.BlockSpec((1, r, c), lambda i, g: (g, 0, 0))
    rows_g = lambda c: pl.BlockSpec((1, SSD_CHUNK, c), lambda i, g: (g, i, 0))
    ns = SSD_CHUNK // seq_len
    state = pl.BlockSpec((ns, M2_D_STATE, gw), lambda i, g: (i, 0, g))
    y, hn = pl.pallas_call(
        functools.partial(_ssd_packed_kernel, seq_len=seq_len),
        grid=(t // SSD_CHUNK, M2_GROUPS),
        in_specs=[rows_g(cdim),
                  pl.BlockSpec((M2_CONV - 1, 1, SSD_CHUNK, cdim), lambda i, g: (0, g, i, 0)),
                  pl.BlockSpec((SSD_CHUNK, gw), lambda i, g: (i, g)),
                  rows_g(LANES), state,
                  per_g(M2_CONV, cdim), per_g(1, cdim), per_g(1, LANES), per_g(1, LANES), per_g(1, gw), per_g(1, gw)],
        out_specs=[pl.BlockSpec((SSD_CHUNK, gw), lambda i, g: (i, g)), state],
        out_shape=[jax.ShapeDtypeStruct((t, M2_WIDTH), F32), jax.ShapeDtypeStruct((n_seq, M2_D_STATE, M2_WIDTH), F32)],
        scratch_shapes=[pltpu.VMEM((CONV_HALO + SSD_CHUNK, cdim), F32)],
        compiler_params=_cparams("parallel", "parallel"),
        name="ssd_packed",
    )(_by_group(xbc), hist, z, heads(dt), h0, _by_group(cw), _by_group(cb), heads(dtb), heads(alog),
      halves(dexp), halves(nw))
    conv_new = xbc.reshape(n_seq, seq_len, M2_CONV_DIM)[:, seq_len - (M2_CONV - 1):, :]
    return y, conv_new, hn


def _post_kernel(y5_ref, u_ref, ym_ref, x_ref, d_ref, wglu_ref, s5nw_ref, wo1_ref, wo2_ref, fnw_ref, wqt_ref,
                 sk_ref, x1_ref, xt_ref, st_ref):
    g = _gelu_tanh(_load_blocked(y5_ref) + d_ref[...] * _load_blocked(u_ref))
    o = g * _sigmoid(jnp.dot(g.astype(BF16), wglu_ref[...], preferred_element_type=F32))
    y5n = _rms(o, s5nw_ref[...])
    x1 = (x_ref[...] + jnp.dot(y5n.astype(BF16), wo1_ref[...], preferred_element_type=F32)
          + jnp.dot(ym_ref[...].astype(BF16), wo2_ref[...], preferred_element_type=F32))
    x1_ref[...] = x1
    hn_t = _rms(x1, fnw_ref[...]).T.astype(BF16)
    xt_ref[...] = hn_t
    q_t = jnp.dot(wqt_ref[...], hn_t, preferred_element_type=F32)
    for k in range(2 * PEER_HEADS):
        qk = q_t[k * PEER_HALF:(k + 1) * PEER_HALF, :].astype(BF16)
        st_ref[k] = jnp.dot(sk_ref[k], qk, preferred_element_type=F32) * LOG2E


def _post(y5, u, ym, x, prm, tm, s5_block=None):
    t = x.shape[0]
    row = lambda i: (i, 0)
    s5_spec = pl.BlockSpec((tm, D_MODEL), row) if s5_block is None else pl.BlockSpec(*s5_block)
    fix = lambda i: (0, 0)
    nk = 2 * PEER_HEADS
    once = functools.partial(pl.BlockSpec, pipeline_mode=pl.Buffered(1))
    return pl.pallas_call(
        _post_kernel,
        grid=(t // tm,),
        in_specs=[s5_spec, s5_spec, pl.BlockSpec((tm, D_MODEL), row), pl.BlockSpec((tm, D_MODEL), row)]
        + [pl.BlockSpec((1, D_MODEL), fix), once((S5_WIDTH, S5_WIDTH), fix), pl.BlockSpec((1, D_MODEL), fix),
           once((S5_WIDTH, D_MODEL), fix), once((M2_WIDTH, D_MODEL), fix),
           pl.BlockSpec((1, D_MODEL), fix), once((nk * PEER_HALF, D_MODEL), fix),
           once((nk, PEER_N_KEYS, PEER_HALF), lambda i: (0, 0, 0))],
        out_specs=[pl.BlockSpec((tm, D_MODEL), row), pl.BlockSpec((D_MODEL, tm), lambda i: (0, i)),
                   pl.BlockSpec((nk, PEER_N_KEYS, tm), lambda i: (0, 0, i))],
        out_shape=[jax.ShapeDtypeStruct((t, D_MODEL), F32), jax.ShapeDtypeStruct((D_MODEL, t), BF16),
                   jax.ShapeDtypeStruct((nk, PEER_N_KEYS, t), F32)],
        compiler_params=_cparams("parallel"),
        name="post",
    )(y5, u, ym, x, *prm)


def _sorting_network(n):
    pairs = []
    p = 1
    while p < n:
        k = p
        while k >= 1:
            for j in range(k % p, n - k, 2 * k):
                for i in range(min(k, n - j - k)):
                    if (i + j) // (2 * p) == (i + j + k) // (2 * p):
                        pairs.append((i + j, i + j + k))
            k //= 2
        p *= 2
    return pairs


SUBLANES = 8
_NET16 = _sorting_network(PEER_N_KEYS // SUBLANES)
LOG2E = math.log2(math.e)


def _top16(tiles):
    n = len(tiles)
    tiles = list(tiles)

    def exchange(i, j):
        hi, lo = tiles[i], tiles[j]
        if lo is None:
            return
        if hi is None:
            tiles[i], tiles[j] = lo, None
        else:
            tiles[i], tiles[j] = jnp.maximum(hi, lo), jnp.minimum(hi, lo)

    def larger(a, b):
        return b if a is None else a if b is None else jnp.maximum(a, b)

    for i, j in _NET16:
        exchange(i, j)
    shift = SUBLANES // 2
    while shift >= 1:
        other = [None if t is None else pltpu.roll(t, shift, axis=0) for t in tiles]
        tiles = [larger(tiles[i], other[n - 1 - i]) for i in range(n)]
        dist = n // 2
        while dist >= 1:
            for i in range(n):
                if i & dist == 0:
                    exchange(i, i + dist)
            dist //= 2
        shift //= 2
    return tiles


def _top_values(s):
    n = PEER_N_KEYS // SUBLANES
    tiles = _top16([s[i * SUBLANES:(i + 1) * SUBLANES, :] for i in range(n)])
    rows = lax.broadcasted_iota(jnp.int32, (PEER_TOPK, s.shape[1]), 0)
    out = jnp.zeros((PEER_TOPK, s.shape[1]), F32)
    for r in range(PEER_TOPK):
        out = jnp.where(rows == r, jnp.concatenate([tiles[r], tiles[r]], axis=0), out)
    return out


def _pair_candidates(v1, v2):
    r8 = lax.broadcasted_iota(jnp.int32, (SUBLANES, v1.shape[1]), 0)
    r16 = lax.broadcasted_iota(jnp.int32, (PEER_TOPK, v1.shape[1]), 0)
    neg = -jnp.inf
    lo2 = v2[0:SUBLANES, :]
    return jnp.concatenate([
        v1[0:1, :] + v2,
        v1[1:2, :] + lo2,
        jnp.where(r16 >= 2, v1 + v2[0:1, :], neg),
        jnp.where(r8 >= 2, v1[0:SUBLANES, :] + v2[1:2, :], neg),
        jnp.where((r8 >= 2) & (r8 <= 4), v1[2:3, :] + lo2, neg),
        jnp.where((r8 >= 2) & (r8 <= 3), v1[3:4, :] + lo2, neg),
        jnp.where(r8 == 2, v1[4:5, :] + lo2, neg),
    ], axis=0)


def _route_kernel(st_ref, s1n_ref, thr_ref):
    def head(h, carry):
        s1 = st_ref[2 * h]
        s2 = st_ref[2 * h + 1]
        v1 = _top_values(s1)
        v2 = _top_values(s2)
        cand = _pair_candidates(v1, v2)
        n_cand = cand.shape[0] // SUBLANES
        pad = [None] * (PEER_N_KEYS // SUBLANES - n_cand)
        ranked = _top16([cand[i * SUBLANES:(i + 1) * SUBLANES, :] for i in range(n_cand)] + pad)
        theta = ranked[PEER_TOPK - 1][0:1, :]
        sel = cand >= theta
        m = v1[0:1, :] + v2[0:1, :]
        zsum = jnp.sum(jnp.where(sel, jnp.exp2(cand - m), 0.0), axis=0, keepdims=True)
        off = m + jnp.log(zsum) * LOG2E + 1.0
        s1n_ref[h] = s1 - off
        candn = _pair_candidates(v1 - off, v2)
        thr_ref[h] = jnp.min(jnp.where(sel, candn, jnp.inf), axis=0, keepdims=True)
        return carry

    lax.fori_loop(0, PEER_HEADS, head, 0)


def _route(st, tl):
    nk, keys, t = st.shape
    spec = lambda n: pl.BlockSpec((n, keys, tl), lambda i: (0, 0, i))
    return pl.pallas_call(
        _route_kernel,
        grid=(t // tl,),
        in_specs=[spec(nk)],
        out_specs=[spec(PEER_HEADS), pl.BlockSpec((PEER_HEADS, 1, tl), lambda i: (0, 0, i))],
        out_shape=[jax.ShapeDtypeStruct((PEER_HEADS, keys, t), F32), jax.ShapeDtypeStruct((PEER_HEADS, 1, t), F32)],
        compiler_params=_cparams("parallel"),
        name="route",
    )(st)


PEER_E1_BLK = 16
PEER_CHUNK_KEYS = (4, 4, 4, 4)
assert sum(PEER_CHUNK_KEYS) == PEER_E1_BLK


def _peer_kernel(u_ref, vt_ref, xt_ref, s2n_ref, s1n_ref, thr_ref, x1_ref, fnw_ref, y_ref, acc_scr, *, tm):
    k = pl.program_id(1)
    c0 = math.sqrt(2.0 / math.pi)
    c1 = c0 * 0.044715
    first_key = [sum(PEER_CHUNK_KEYS[:q]) for q in range(len(PEER_CHUNK_KEYS))]

    def experts(q):
        return slice(first_key[q] * PEER_N_KEYS, (first_key[q] + PEER_CHUNK_KEYS[q]) * PEER_N_KEYS)

    def scores(q):
        return jnp.dot(u_ref[experts(q), :], xt_ref[...], preferred_element_type=F32)

    def gated(q, a):
        wg_rows = []
        for i in range(PEER_CHUNK_KEYS[q]):
            r = first_key[q] + i
            rows = slice(i * PEER_N_KEYS, (i + 1) * PEER_N_KEYS)
            wg_cols = []
            for c in range(tm // LANES):
                cols = slice(c * LANES, (c + 1) * LANES)
                w = None
                for h in range(PEER_HEADS):
                    arg = s2n_ref[h, :, cols] + s1n_ref[h, r:r + 1, cols]
                    term = jnp.where(arg >= thr_ref[h, :, cols], jnp.exp2(arg), 0.0)
                    w = term if w is None else w + term
                x = a[rows, cols]
                g = x + x * jnp.tanh(x * (c0 + c1 * (x * x)))
                wg_cols.append((w * g).astype(BF16))
            wg_rows.append(jnp.concatenate(wg_cols, axis=1))
        return jnp.concatenate(wg_rows, axis=0)

    def mixed(q, wg):
        return jnp.dot(vt_ref[0, :, experts(q)], wg, preferred_element_type=F32)

    n_chunks = len(PEER_CHUNK_KEYS)
    total = None
    a_next = scores(0)
    wg_prev = None
    for q in range(n_chunks):
        a_cur = a_next
        if q + 1 < n_chunks:
            a_next = scores(q + 1)
        wg = gated(q, a_cur)
        if wg_prev is not None:
            d = mixed(q - 1, wg_prev)
            total = d if total is None else total + d
        wg_prev = wg
    total = total + mixed(n_chunks - 1, wg_prev)

    @pl.when(k == 0)
    def _():
        acc_scr[...] = total

    @pl.when(k > 0)
    def _():
        acc_scr[...] += total

    @pl.when(k == pl.num_programs(1) - 1)
    def _():
        y_ref[...] = _rms(x1_ref[...] + acc_scr[...].T, fnw_ref[...])


def _peer(u_bf, vt_bf, xt, st, route, x1, fnw, tm):
    s1n, thr = route
    t = x1.shape[0]
    eb = PEER_E1_BLK * PEER_N_KEYS
    s2 = st.reshape(PEER_HEADS, 2, PEER_N_KEYS, t)
    per_key = pl.BlockSpec((PEER_HEADS, None, PEER_N_KEYS, tm), lambda j, k: (0, 1, 0, j))
    per_blk = pl.BlockSpec((PEER_HEADS, PEER_E1_BLK, tm), lambda j, k: (0, k, j))
    return pl.pallas_call(
        functools.partial(_peer_kernel, tm=tm),
        grid=(t // tm, PEER_N_KEYS // PEER_E1_BLK),
        in_specs=[pl.BlockSpec((eb, D_MODEL), lambda j, k: (k, 0)),
                  pl.BlockSpec((1, D_MODEL, eb), lambda j, k: (k, 0, 0)),
                  pl.BlockSpec((D_MODEL, tm), lambda j, k: (0, j)),
                  per_key, per_blk, pl.BlockSpec((PEER_HEADS, 1, tm), lambda j, k: (0, 0, j)),
                  pl.BlockSpec((tm, D_MODEL), lambda j, k: (j, 0)),
                  pl.BlockSpec((1, D_MODEL), lambda j, k: (0, 0))],
        out_specs=pl.BlockSpec((tm, D_MODEL), lambda j, k: (j, 0)),
        out_shape=jax.ShapeDtypeStruct((t, D_MODEL), F32),
        scratch_shapes=[pltpu.VMEM((D_MODEL, tm), F32)],
        compiler_params=_cparams("parallel", "arbitrary"),
        name="peer",
    )(u_bf, vt_bf, xt, s2, s1n, thr, x1, fnw)


TOKEN_TILE = 512


def _tokens_tail(x, post_prm, peer_prm, y5, u, ym, s5_block):
    x1, xt, st = _post(y5, u, ym, x, post_prm, tm=TOKEN_TILE, s5_block=s5_block)
    u_bf, vt_bf, fnw = peer_prm
    return _peer(u_bf, vt_bf, xt, st, _route(st, tl=2 * LANES), x1, fnw, tm=TOKEN_TILE)


def kernel(x_prompt, x_sample, state_s5_re, state_s5_im, state_ssm, state_conv, meta_tokens, norm_mix_w, w_in,
           s5_lambda_re, s5_lambda_im, s5_log_step, s5_b_re, s5_b_im, s5_c_re, s5_c_im, s5_d, s5_w_glu, s5_norm_w,
           m2_conv_w, m2_conv_b, m2_dt_bias, m2_a_log, m2_d, m2_norm_w, w_out, norm_ffn_w, peer_w_q, peer_sub_keys,
           peer_u, peer_v, final_norm_w):
    bp, sp, _ = x_prompt.shape
    bs, ss, _ = x_sample.shape
    g, p = S5_GROUPS, S5_STATE

    w = w_in[0]
    o1, o2, o3 = S5_WIDTH, S5_WIDTH + M2_WIDTH, S5_WIDTH + M2_WIDTH + M2_CONV_DIM
    wu, wz, wx = w[:, :o1].astype(BF16), w[:, o1:o2].astype(BF16), w[:, o2:o3].astype(BF16)
    wd = jnp.pad(w[:, o3:], ((0, 0), (0, LANES - M2_HEADS))).astype(BF16)
    nmw = norm_mix_w[0][None, :]
    pad_h = lambda v: jnp.pad(v, (0, LANES - M2_HEADS))[None, :]
    ssd_prm = (m2_conv_w[0], m2_conv_b[0][None, :], pad_h(m2_dt_bias[0]), pad_h(m2_a_log[0]),
               jnp.repeat(m2_d[0], M2_HEAD_DIM)[None, :], m2_norm_w[0][None, :])
    post_prm = (s5_d[0][None, :], s5_w_glu[0].astype(BF16), s5_norm_w[0][None, :],
                w_out[0][:S5_WIDTH].astype(BF16), w_out[0][S5_WIDTH:].astype(BF16), norm_ffn_w[0][None, :],
                peer_w_q[0].T.astype(BF16),
                peer_sub_keys[0].reshape(2 * PEER_HEADS, PEER_N_KEYS, PEER_HALF).astype(BF16))
    eb = PEER_E1_BLK * PEER_N_KEYS
    vt_blocks = peer_v[0].astype(BF16).reshape(PEER_EXPERTS // eb, eb, D_MODEL).swapaxes(1, 2)
    peer_prm = (peer_u[0].astype(BF16), vt_blocks, final_norm_w[None, :])
    s5_args = (s5_lambda_re[0], s5_lambda_im[0], s5_log_step[0], s5_b_re[0], s5_b_im[0], s5_c_re[0], s5_c_im[0])
    ops16 = _s5_prep(*s5_args, lc=N_META)
    ops8 = _s5_prep(*s5_args, lc=ss)

    lc = N_META
    um, _, xbcm, dtm = _in_proj(meta_tokens, nmw, wu, wz, wx, wd, tm=N_META)
    nb = S5_WIDTH // LANES
    urm = jnp.broadcast_to(um.reshape(lc, nb, LANES).transpose(1, 0, 2).reshape(nb, 1, lc * LANES),
                           (nb, SUBLANES, lc * LANES))
    _, h5m = _s5(urm, jnp.zeros((g, SUBLANES, 2 * p), F32), ops16, lc=lc, rows=SUBLANES, chunks=1)
    _, convm, hm = _ssd(xbcm, jnp.zeros((N_META, M2_WIDTH), F32), dtm,
                        jnp.zeros((1, M2_CONV - 1, M2_CONV_DIM), F32), jnp.zeros((1, M2_D_STATE, M2_WIDTH), F32),
                        ssd_prm, n_seq=1, n_chunks=1, q_in=N_META, shared_init=True)

    xp = x_prompt.reshape(bp * sp, D_MODEL)
    nc = sp // lc
    cpt = TOKEN_TILE // lc
    tps = sp // TOKEN_TILE
    p_shape = (nb, nc, bp, lc, LANES)
    p_block = ((nb, cpt, 1, lc, LANES), lambda i: (0, i % tps, i // tps, 0, 0))
    up, zp, xbcp, dtp = _in_proj(xp, nmw, wu, wz, wx, wd, tm=TOKEN_TILE, u_layout=(p_shape,) + p_block)
    y5p, h5p = _s5(up.reshape(nb, nc * bp * lc, LANES), h5m, ops16, lc=lc, rows=bp, chunks=S5_ROW_TILE // bp)
    ymp, convp, hp = _ssd(xbcp, zp, dtp, convm, hm, ssd_prm, n_seq=bp, n_chunks=sp // SSD_CHUNK, q_in=SSD_CHUNK,
                          shared_init=True)
    y_prompt = _tokens_tail(xp, post_prm, peer_prm, y5p.reshape(p_shape), up, ymp, p_block).reshape(bp, sp, D_MODEL)

    xs = x_sample.reshape(bs * ss, D_MODEL)
    s_shape = (nb, bs, ss, LANES)
    s_block = ((nb, TOKEN_TILE // ss, ss, LANES), lambda i: (0, i, 0, 0))
    us, zs, xbcs, dts = _in_proj(xs, nmw, wu, wz, wx, wd, tm=TOKEN_TILE, u_layout=(s_shape,) + s_block)
    h5s0 = jnp.concatenate([state_s5_re[0], state_s5_im[0]], axis=-1).transpose(1, 0, 2)
    y5s, h5s = _s5(us.reshape(nb, bs * ss, LANES), h5s0, ops8, lc=ss, rows=bs, chunks=1)
    hs0 = state_ssm[0].reshape(bs, M2_WIDTH, M2_D_STATE).transpose(0, 2, 1)
    yms, convs, hs = _ssd_packed(xbcs, zs, dts, state_conv[0], hs0, ssd_prm, n_seq=bs, seq_len=ss)
    y_sample = _tokens_tail(xs, post_prm, peer_prm, y5s.reshape(s_shape), us, yms, s_block).reshape(bs, ss, D_MODEL)

    def s5_state(hf):
        hf = hf.transpose(1, 0, 2)
        return hf[None, :, :, :p], hf[None, :, :, p:]

    def ssm_state(ht):
        return ht.transpose(0, 2, 1).reshape(1, ht.shape[0], M2_HEADS, M2_HEAD_DIM, M2_D_STATE)

    p5r, p5i = s5_state(h5p)
    s5r, s5i = s5_state(h5s)
    return (y_prompt, y_sample, p5r, p5i, ssm_state(hp), convp[None], s5r, s5i, ssm_state(hs), convs[None])
```

```python
import functools
import math

import jax
import jax.numpy as jnp
from jax import lax
from jax.experimental import pallas as pl
from jax.experimental.pallas import tpu as pltpu

F32 = jnp.float32
BF16 = jnp.bfloat16
HIGHEST = lax.Precision.HIGHEST

D_MODEL = 1024
N_META = 16
S5_WIDTH = 1024
S5_CH = 16
S5_GROUPS = S5_WIDTH // S5_CH
S5_STATE = 64
M2_WIDTH = 1024
M2_HEAD_DIM = 64
M2_HEADS = M2_WIDTH // M2_HEAD_DIM
M2_GROUPS = 2
M2_D_STATE = 128
M2_CONV = 4
M2_CONV_DIM = M2_WIDTH + 2 * M2_GROUPS * M2_D_STATE
PEER_HEADS = 8
PEER_N_KEYS = 128
PEER_EXPERTS = PEER_N_KEYS * PEER_N_KEYS
PEER_HALF = 128
PEER_TOPK = 16
EPS = 1e-6

LANES = 128
SSD_CHUNK = 128
CONV_HALO = 8
VMEM_LIMIT = 56 * 1024 * 1024


def _cparams(*sem):
    return pltpu.CompilerParams(dimension_semantics=sem, vmem_limit_bytes=VMEM_LIMIT)


def _sigmoid(x):
    return 1.0 / (1.0 + jnp.exp(-x))


def _gelu_tanh(x):
    c = math.sqrt(2.0 / math.pi)
    return 0.5 * x * (1.0 + jnp.tanh(c * (x + 0.044715 * (x * x * x))))


def _rms(x, w):
    return x * lax.rsqrt(jnp.mean(x * x, axis=-1, keepdims=True) + EPS) * w


def _bdot(a, b):
    return jnp.dot(a.astype(BF16), b.astype(BF16), preferred_element_type=F32)


def _store_blocked(ref, val):
    if len(ref.shape) == 2:
        ref[...] = val
        return
    for gb in range(ref.shape[0]):
        piece = val[:, gb * LANES:(gb + 1) * LANES]
        if len(ref.shape) == 5:
            ref[gb, :, 0] = piece.reshape(ref.shape[1], ref.shape[3], LANES)
        else:
            ref[gb] = piece.reshape(ref.shape[1:])


def _load_blocked(ref):
    if len(ref.shape) == 2:
        return ref[...]
    parts = []
    for gb in range(ref.shape[0]):
        piece = ref[gb, :, 0] if len(ref.shape) == 5 else ref[gb]
        parts.append(piece.reshape(piece.shape[0] * piece.shape[1], LANES))
    return jnp.concatenate(parts, axis=1)


def _in_proj_kernel(x_ref, nw_ref, wu_ref, wz_ref, wx_ref, wd_ref, u_ref, z_ref, xbc_ref, dt_ref):
    hb = _rms(x_ref[...], nw_ref[...]).astype(BF16)
    _store_blocked(u_ref, jnp.dot(hb, wu_ref[...], preferred_element_type=F32))
    z_ref[...] = jnp.dot(hb, wz_ref[...], preferred_element_type=F32)
    xbc_ref[...] = jnp.dot(hb, wx_ref[...], preferred_element_type=F32)
    dt_ref[...] = jnp.dot(hb, wd_ref[...], preferred_element_type=F32)


def _in_proj(x, nw, wu, wz, wx, wd, tm, u_layout=None):
    t = x.shape[0]
    row = lambda i: (i, 0)
    fix = lambda i: (0, 0)
    widths = (S5_WIDTH, M2_WIDTH, M2_CONV_DIM, LANES)
    out_specs = [pl.BlockSpec((tm, w), row) for w in widths]
    out_shape = [jax.ShapeDtypeStruct((t, w), F32) for w in widths]
    if u_layout is not None:
        out_specs[0] = pl.BlockSpec(u_layout[1], u_layout[2])
        out_shape[0] = jax.ShapeDtypeStruct(u_layout[0], F32)
    return pl.pallas_call(
        _in_proj_kernel,
        grid=(t // tm,),
        in_specs=[pl.BlockSpec((tm, D_MODEL), row), pl.BlockSpec((1, D_MODEL), fix)]
        + [pl.BlockSpec((D_MODEL, w), fix) for w in widths],
        out_specs=out_specs,
        out_shape=out_shape,
        compiler_params=_cparams("parallel"),
        name="in_proj",
    )(x, nw, wu, wz, wx, wd)


def _s5_prep_kernel(lrc_ref, lic_ref, lrr_ref, lir_ref, ls_ref, btr_ref, bti_ref, ctr_ref, cti_ref,
                    kt_ref, wout_ref, wsr_ref, wsi_ref, al_ref, *, lc):
    for j in range(S5_GBLK):
        _s5_prep_group(j, lc, lrc_ref, lic_ref, lrr_ref, lir_ref, ls_ref, btr_ref, bti_ref, ctr_ref, cti_ref,
                       kt_ref, wout_ref, wsr_ref, wsi_ref, al_ref)


def _s5_prep_group(j, lc, lrc_ref, lic_ref, lrr_ref, lir_ref, ls_ref, btr_ref, bti_ref, ctr_ref, cti_ref,
                   kt_ref, wout_ref, wsr_ref, wsi_ref, al_ref):
    k = lc * S5_CH
    step = jnp.exp(ls_ref[j])

    def disc(lr, li):
        lg = lr * step
        th = li * step
        mag = jnp.exp(lg)
        ab_re = mag * jnp.cos(th)
        ab_im = mag * jnp.sin(th)
        den = lr * lr + li * li
        f_re = ((ab_re - 1.0) * lr + ab_im * li) / den
        f_im = (ab_im * lr - (ab_re - 1.0) * li) / den
        return ab_re, ab_im, f_re, f_im

    def powers(n, a_re_, a_im_):
        p_re_ = jnp.ones(n.shape, F32)
        p_im_ = jnp.zeros(n.shape, F32)
        b_re_, b_im_ = a_re_, a_im_
        bit = 1
        while bit <= lc:
            use = (n & bit) != 0
            t_re = p_re_ * b_re_ - p_im_ * b_im_
            t_im = p_re_ * b_im_ + p_im_ * b_re_
            p_re_ = jnp.where(use, t_re, p_re_)
            p_im_ = jnp.where(use, t_im, p_im_)
            b_re_, b_im_ = b_re_ * b_re_ - b_im_ * b_im_, 2.0 * (b_re_ * b_im_)
            bit *= 2
        return p_re_, p_im_

    ac_re, ac_im, f_re, f_im = disc(lrc_ref[j], lic_ref[j])
    tau = lax.shift_right_logical(lax.broadcasted_iota(jnp.int32, (S5_STATE, k), 1), 4)
    c_re = ctr_ref[j]
    c_im = cti_ref[j]
    p_re, p_im = powers(tau, ac_re, ac_im)
    fa_re = f_re * p_re - f_im * p_im
    fa_im = f_re * p_im + f_im * p_re
    fca_re = fa_re * c_re - fa_im * c_im
    fca_im = fa_re * c_im + fa_im * c_re
    kt_ref[j] = (jnp.dot(btr_ref[j], fca_re, precision=HIGHEST, preferred_element_type=F32)
                 - jnp.dot(bti_ref[j], fca_im, precision=HIGHEST, preferred_element_type=F32))
    q_re, q_im = powers(tau + 1, ac_re, ac_im)
    wout_ref[j, 0:S5_STATE, :] = q_re * c_re - q_im * c_im
    wout_ref[j, S5_STATE:2 * S5_STATE, :] = -(q_re * c_im + q_im * c_re)

    ar_re, ar_im, fr_re, fr_im = disc(lrr_ref[j], lir_ref[j])
    kk = lax.broadcasted_iota(jnp.int32, (lc, S5_STATE), 0)
    r_re, r_im = powers(kk, ar_re, ar_im)
    g_re = fr_re * r_re - fr_im * r_im
    g_im = fr_re * r_im + fr_im * r_re
    bt_re = btr_ref[j]
    bt_im = bti_ref[j]
    for s in range(lc):
        kpow = lc - 1 - s
        w_re = g_re[kpow:kpow + 1, :]
        w_im = g_im[kpow:kpow + 1, :]
        wsr_ref[j, s * S5_CH:(s + 1) * S5_CH, :] = bt_re * w_re - bt_im * w_im
        wsi_ref[j, s * S5_CH:(s + 1) * S5_CH, :] = bt_re * w_im + bt_im * w_re
    a_re, a_im = powers(jnp.full((1, S5_STATE), lc, jnp.int32), ar_re, ar_im)
    al_ref[j, 0:1, :] = a_re
    al_ref[j, 1:2, :] = a_im


def _s5_prep(lam_re, lam_im, log_step, b_re, b_im, c_re, c_im, lc):
    g, p, k = S5_GROUPS, S5_STATE, lc * S5_CH
    spec3 = lambda a, b: pl.BlockSpec((S5_GBLK, a, b), lambda i: (i, 0, 0))
    ins = [lam_re.reshape(g, p, 1), lam_im.reshape(g, p, 1), lam_re.reshape(g, 1, p), lam_im.reshape(g, 1, p),
           log_step.reshape(g, 1, 1),
           jnp.swapaxes(b_re, 1, 2), jnp.swapaxes(b_im, 1, 2),
           jnp.tile(jnp.swapaxes(c_re, 1, 2), (1, 1, lc)), jnp.tile(jnp.swapaxes(c_im, 1, 2), (1, 1, lc))]
    kt, wout, wsr, wsi, al = pl.pallas_call(
        functools.partial(_s5_prep_kernel, lc=lc),
        grid=(g // S5_GBLK,),
        in_specs=[spec3(p, 1), spec3(p, 1), spec3(1, p), spec3(1, p), spec3(1, 1),
                  spec3(S5_CH, p), spec3(S5_CH, p), spec3(p, k), spec3(p, k)],
        out_specs=[spec3(S5_CH, k), spec3(2 * p, k), spec3(k, p), spec3(k, p), spec3(2, p)],
        out_shape=[jax.ShapeDtypeStruct((g, S5_CH, k), F32), jax.ShapeDtypeStruct((g, 2 * p, k), F32),
                   jax.ShapeDtypeStruct((g, k, p), F32), jax.ShapeDtypeStruct((g, k, p), F32),
                   jax.ShapeDtypeStruct((g, 2, p), F32)],
        compiler_params=_cparams("parallel"),
        name="s5_prep",
    )(*ins)
    lag = jnp.arange(lc)[None, :] - jnp.arange(lc)[:, None]
    kt4 = kt.reshape(g, S5_CH, lc, S5_CH)
    toep = kt4[:, :, jnp.maximum(lag, 0), :]
    toep = jnp.where((lag >= 0)[None, None, :, :, None], toep, 0.0)
    toep = toep.transpose(0, 2, 1, 3, 4).reshape(g, k, k)
    wst = jnp.concatenate([wsr, wsi, wsi, wsr], axis=-1)
    a_re, a_im = al[:, 0:1, :], al[:, 1:2, :]
    acoef = jnp.concatenate([jnp.concatenate([a_re, a_re], -1), jnp.concatenate([-a_im, a_im], -1),
                             jnp.concatenate([a_im, -a_im], -1)], axis=1)
    return toep.astype(BF16), wst.astype(BF16), wout.astype(BF16), acoef


S5_GBLK = LANES // S5_CH
S5_ROW_TILE = 256


def _block_transpose(cols):
    n = len(cols)
    blk = lax.shift_right_logical(lax.broadcasted_iota(jnp.int32, (1, LANES), 1), 4)
    rolled = []
    for d in range(n):
        m = cols[d]
        for b in range(1, n):
            m = jnp.where(blk == b, cols[(b + d) % n], m)
        rolled.append(m if d == 0 else pltpu.roll(m, d * S5_CH, axis=1))
    outs = []
    for j in range(n):
        o = rolled[(-j) % n]
        for s in range(1, n):
            o = jnp.where(blk == s, rolled[(s - j) % n], o)
        outs.append(o)
    return outs


def _s5_kernel(u_ref, h0_ref, h0s_ref, wst_ref, toep_ref, wout_ref, a_ref, y_ref, hf_ref,
               s_scr, hin_scr, h_scr, hs_scr, *, lc, rows, chunks, tall):
    i = pl.program_id(1)

    @pl.when(i == 0)
    def _():
        h_scr[...] = h0_ref[...]
        hs_scr[...] = h0s_ref[...]

    rt = rows * chunks
    if tall:
        token = lambda s: u_ref[0, pl.ds(s, rt, stride=lc), :]
    else:
        x = u_ref[0]
        token = lambda s: x[:, s * LANES:(s + 1) * LANES]
    halves = [_block_transpose([token(S5_GBLK * c + s) for s in range(S5_GBLK)]) for c in range(lc // S5_GBLK)]
    ubs = []
    for j in range(S5_GBLK):
        ubs.append(jnp.concatenate([hv[j] for hv in halves], axis=1).astype(BF16))
        s_scr[j] = jnp.dot(ubs[j], wst_ref[j], preferred_element_type=F32)

    def step(r0, carry):
        new = []
        for j in range(S5_GBLK):
            h, hs = carry[2 * j], carry[2 * j + 1]
            hin_scr[j, pl.ds(r0, rows), :] = h
            s = s_scr[j, pl.ds(r0, rows), :]
            a1, a2, a2s = a_ref[j, 0:1, :], a_ref[j, 1:2, :], a_ref[j, 2:3, :]
            new.append(a1 * h + a2 * hs + s[:, :LANES])
            new.append(a1 * hs + a2s * h + s[:, LANES:])
        return tuple(new)

    carry = tuple(ref[j] for j in range(S5_GBLK) for ref in (h_scr, hs_scr))
    if chunks == 1:
        carry = step(0, carry)
    else:
        carry = lax.fori_loop(0, chunks, lambda n, c: step(pl.multiple_of(n * rows, rows), c), carry)
    for j in range(S5_GBLK):
        h_scr[j] = carry[2 * j]
        hs_scr[j] = carry[2 * j + 1]
    hf_ref[...] = h_scr[...]

    ys = [jnp.dot(ubs[j], toep_ref[j], preferred_element_type=F32)
          + jnp.dot(hin_scr[j].astype(BF16), wout_ref[j], preferred_element_type=F32) for j in range(S5_GBLK)]
    for c in range(lc // S5_GBLK):
        token_major = _block_transpose([ys[j][:, c * LANES:(c + 1) * LANES] for j in range(S5_GBLK)])
        for t in range(S5_GBLK):
            if tall:
                y_ref[0, pl.ds(S5_GBLK * c + t, rt, stride=lc), :] = token_major[t]
            else:
                y_ref[0, :, (S5_GBLK * c + t) * LANES:(S5_GBLK * c + t + 1) * LANES] = token_major[t]


def _s5(ub, h0, ops, lc, rows, chunks):
    toep, wst, wout, acoef = ops
    tall = ub.shape[2] == LANES
    nb = ub.shape[0]
    nr = ub.shape[1] // lc if tall else ub.shape[1]
    k = lc * S5_CH
    rt = rows * chunks
    io_block = (1, rt * lc, LANES) if tall else (1, rt, lc * LANES)
    blk = lambda a, b: pl.BlockSpec((S5_GBLK, a, b), lambda gb, i: (gb, 0, 0))
    return pl.pallas_call(
        functools.partial(_s5_kernel, lc=lc, rows=rows, chunks=chunks, tall=tall),
        grid=(nb, nr // rt),
        in_specs=[pl.BlockSpec(io_block, lambda gb, i: (gb, i, 0)),
                  blk(rows, LANES), blk(rows, LANES), blk(k, 2 * LANES), blk(k, k), blk(LANES, k), blk(3, LANES)],
        out_specs=[pl.BlockSpec(io_block, lambda gb, i: (gb, i, 0)), blk(rows, LANES)],
        out_shape=[jax.ShapeDtypeStruct(ub.shape, F32), jax.ShapeDtypeStruct((S5_GROUPS, rows, LANES), F32)],
        scratch_shapes=[pltpu.VMEM((S5_GBLK, rt, 2 * LANES), F32), pltpu.VMEM((S5_GBLK, rt, LANES), F32),
                        pltpu.VMEM((S5_GBLK, rows, LANES), F32), pltpu.VMEM((S5_GBLK, rows, LANES), F32)],
        compiler_params=_cparams("parallel", "arbitrary"),
        name="s5_scan",
    )(ub, h0, jnp.roll(h0, S5_STATE, axis=-1), wst, toep, wout, acoef)


def _ssd_kernel(xbc_ref, z_ref, dt_ref, conv0_ref, h0_ref, cw_ref, cb_ref, dtb_ref, alog_ref, dexp_ref, nw_ref,
                y_ref, convn_ref, hn_ref, buf_scr, dt_scr, h_scr, *, q_in, n_chunks):
    q = SSD_CHUNK
    pad = q - q_in
    first = CONV_HALO + pad
    c = pl.program_id(1)

    @pl.when(c == 0)
    def _():
        buf_scr[0:first, :] = jnp.zeros((first, M2_CONV_DIM), F32)
        buf_scr[first - (M2_CONV - 1):first, :] = conv0_ref[0]
        h_scr[...] = h0_ref[0]

    buf_scr[first:CONV_HALO + q, :] = xbc_ref[...]
    conv = cb_ref[...]
    for kk in range(M2_CONV):
        lo = CONV_HALO - (M2_CONV - 1) + kk
        conv = conv + cw_ref[kk:kk + 1, :] * buf_scr[lo:lo + q, :]
    convn_ref[0] = buf_scr[CONV_HALO + q - (M2_CONV - 1):CONV_HALO + q, :]
    if n_chunks > 1:
        buf_scr[0:CONV_HALO, :] = buf_scr[q:q + CONV_HALO, :]
    act = conv * _sigmoid(conv)
    xs = act[:, :M2_WIDTH]

    dtv = dt_ref[...] + dtb_ref[...]
    dt_real = jnp.maximum(dtv, 0.0) + jnp.log1p(jnp.exp(-jnp.abs(dtv)))
    if pad:
        dt_scr[0:pad, :] = jnp.zeros((pad, LANES), F32)
        dt_scr[pad:q, :] = dt_real
        dt = dt_scr[...]
    else:
        dt = dt_real
    a_neg = -jnp.exp(alog_ref[...]) * LOG2E
    ri = lax.broadcasted_iota(jnp.int32, (q, q), 0)
    ci = lax.broadcasted_iota(jnp.int32, (q, q), 1)
    tril = ri >= ci
    acs = jnp.dot(tril.astype(F32), dt * a_neg, precision=HIGHEST, preferred_element_type=F32)
    acs_t = acs.T
    dt_t = dt.T
    acs_last = acs[q - 1:q, :]
    eacs = jnp.exp2(acs)
    wdec = jnp.exp2(acs_last - acs) * dt
    dec_last = jnp.exp2(acs_last)
    low = lax.broadcasted_iota(jnp.int32, (q, LANES), 1) < M2_HEAD_DIM
    low1 = low[0:1, :]

    def pair_cols(m, h0):
        return jnp.where(low[0:m.shape[0], :], m[:, h0:h0 + 1], m[:, h0 + 1:h0 + 2])

    y_parts = []
    hpg = M2_HEADS // M2_GROUPS
    gw = hpg * M2_HEAD_DIM
    for g in range(M2_GROUPS):
        bg = act[:, M2_WIDTH + g * M2_D_STATE:M2_WIDTH + (g + 1) * M2_D_STATE]
        cg = act[:, M2_WIDTH + (M2_GROUPS + g) * M2_D_STATE:M2_WIDTH + (M2_GROUPS + g + 1) * M2_D_STATE]
        cgb = cg.astype(BF16)
        cb = lax.dot_general(cgb, bg.astype(BF16), (((1,), (1,)), ((), ())), preferred_element_type=F32)
        hg = h_scr[:, g * gw:(g + 1) * gw]
        yoff = jnp.dot(cgb, hg.astype(BF16), preferred_element_type=F32)
        xw_parts = []
        dec_parts = []
        for pp in range(hpg // 2):
            h0 = g * hpg + 2 * pp
            lanes = slice(h0 * M2_HEAD_DIM, (h0 + 2) * M2_HEAD_DIM)
            w_pair = []
            for h in (h0, h0 + 1):
                seg = acs[:, h:h + 1] - acs_t[h:h + 1, :]
                dec = jnp.exp2(jnp.where(tril, seg, -jnp.inf))
                w_pair.append((cb * dec * dt_t[h:h + 1, :]).astype(BF16))
            xp = xs[:, lanes]
            xbd = jnp.concatenate([jnp.where(low, xp, 0.0), jnp.where(low, 0.0, xp)], axis=0).astype(BF16)
            yd = jnp.dot(jnp.concatenate(w_pair, axis=1), xbd, preferred_element_type=F32)
            yo = yoff[:, 2 * pp * M2_HEAD_DIM:(2 * pp + 2) * M2_HEAD_DIM]
            y_parts.append(yd + yo * pair_cols(eacs, h0) + dexp_ref[:, lanes] * xp)
            xw_parts.append(xp * pair_cols(wdec, h0))
            dec_parts.append(jnp.where(low1, dec_last[:, h0:h0 + 1], dec_last[:, h0 + 1:h0 + 2]))
        xw = jnp.concatenate(xw_parts, axis=1).astype(BF16)
        st = jnp.dot(bg.T.astype(BF16), xw, preferred_element_type=F32)
        h_scr[:, g * gw:(g + 1) * gw] = hg * jnp.concatenate(dec_parts, axis=1) + st

    @pl.when(c == n_chunks - 1)
    def _():
        hn_ref[0] = h_scr[...]

    y = jnp.concatenate(y_parts, axis=1)[pad:, :]
    zz = z_ref[...]
    y = y * (zz * _sigmoid(zz))
    outs = []
    for g in range(M2_GROUPS):
        yg = y[:, g * gw:(g + 1) * gw]
        outs.append(yg * lax.rsqrt(jnp.mean(yg * yg, axis=-1, keepdims=True) + EPS))
    y_ref[...] = jnp.concatenate(outs, axis=1) * nw_ref[...]


def _ssd(xbc, z, dt, conv0, h0, prm, n_seq, n_chunks, q_in, shared_init):
    assert q_in == SSD_CHUNK or n_chunks == 1
    assert q_in >= CONV_HALO and q_in % CONV_HALO == 0
    t = xbc.shape[0]
    row = lambda b, c: (b * n_chunks + c, 0)
    fix = lambda b, c: (0, 0)
    init = (lambda b, c: (0, 0, 0)) if shared_init else (lambda b, c: (b, 0, 0))
    per_seq = lambda b, c: (b, 0, 0)
    return pl.pallas_call(
        functools.partial(_ssd_kernel, q_in=q_in, n_chunks=n_chunks),
        grid=(n_seq, n_chunks),
        in_specs=[pl.BlockSpec((q_in, M2_CONV_DIM), row), pl.BlockSpec((q_in, M2_WIDTH), row),
                  pl.BlockSpec((q_in, LANES), row),
                  pl.BlockSpec((1, M2_CONV - 1, M2_CONV_DIM), init),
                  pl.BlockSpec((1, M2_D_STATE, M2_WIDTH), init),
                  pl.BlockSpec((M2_CONV, M2_CONV_DIM), fix), pl.BlockSpec((1, M2_CONV_DIM), fix),
                  pl.BlockSpec((1, LANES), fix), pl.BlockSpec((1, LANES), fix),
                  pl.BlockSpec((1, M2_WIDTH), fix), pl.BlockSpec((1, M2_WIDTH), fix)],
        out_specs=[pl.BlockSpec((q_in, M2_WIDTH), row),
                   pl.BlockSpec((1, M2_CONV - 1, M2_CONV_DIM), per_seq),
                   pl.BlockSpec((1, M2_D_STATE, M2_WIDTH), per_seq)],
        out_shape=[jax.ShapeDtypeStruct((t, M2_WIDTH), F32),
                   jax.ShapeDtypeStruct((n_seq, M2_CONV - 1, M2_CONV_DIM), F32),
                   jax.ShapeDtypeStruct((n_seq, M2_D_STATE, M2_WIDTH), F32)],
        scratch_shapes=[pltpu.VMEM((CONV_HALO + SSD_CHUNK, M2_CONV_DIM), F32),
                        pltpu.VMEM((SSD_CHUNK, LANES), F32),
                        pltpu.VMEM((M2_D_STATE, M2_WIDTH), F32)],
        compiler_params=_cparams("parallel", "arbitrary"),
        name="ssd",
    )(xbc, z, dt, conv0, h0, *prm)


def _ssd_packed_kernel(xbc_ref, hist_ref, z_ref, dt_ref, h0_ref, cw_ref, cb_ref, dtb_ref, alog_ref, dexp_ref, nw_ref,
                       y_ref, hn_ref, buf_scr, *, seq_len):
    q = SSD_CHUNK
    n = M2_D_STATE
    ns = q // seq_len
    shift = seq_len.bit_length() - 1
    hpg = M2_HEADS // M2_GROUPS
    gw = hpg * M2_HEAD_DIM
    cdim = gw + 2 * n
    x = xbc_ref[0]
    buf_scr[0:CONV_HALO, :] = jnp.zeros((CONV_HALO, cdim), F32)
    buf_scr[CONV_HALO:CONV_HALO + q, :] = x
    pos = lax.broadcasted_iota(jnp.int32, (q, cdim), 0) & (seq_len - 1)
    conv = cb_ref[0] + cw_ref[0, M2_CONV - 1:M2_CONV, :] * x
    for k in range(1, M2_CONV):
        prev = jnp.where(pos >= k, buf_scr[CONV_HALO - k:CONV_HALO - k + q, :], hist_ref[k - 1, 0])
        conv = conv + cw_ref[0, M2_CONV - 1 - k:M2_CONV - k, :] * prev
    act = conv * _sigmoid(conv)
    xs = act[:, :gw]
    bg = act[:, gw:gw + n]
    cg = act[:, gw + n:]

    dtv = dt_ref[0] + dtb_ref[0]
    dt = jnp.maximum(dtv, 0.0) + jnp.log1p(jnp.exp(-jnp.abs(dtv)))
    a_neg = -jnp.exp(alog_ref[0])
    ri = lax.broadcasted_iota(jnp.int32, (q, q), 0)
    ci = lax.broadcasted_iota(jnp.int32, (q, q), 1)
    same = lax.shift_right_logical(ri, shift) == lax.shift_right_logical(ci, shift)
    causal = (ri >= ci) & same
    dta = dt * a_neg
    acs = jnp.dot(causal.astype(F32), dta, precision=HIGHEST, preferred_element_type=F32)
    tot = jnp.dot(same.astype(F32), dta, precision=HIGHEST, preferred_element_type=F32)
    acs_t = acs.T
    dt_t = dt.T
    eacs = jnp.exp(acs)
    wdec = jnp.exp(tot - acs) * dt
    dec_tot = jnp.exp(tot)
    low = lax.broadcasted_iota(jnp.int32, (q, LANES), 1) < M2_HEAD_DIM

    def pair_cols(m, h0):
        return jnp.where(low, m[:, h0:h0 + 1], m[:, h0 + 1:h0 + 2])

    cgb = cg.astype(BF16)
    cb = lax.dot_general(cgb, bg.astype(BF16), (((1,), (1,)), ((), ())), preferred_element_type=F32)
    row_seq = lax.shift_right_logical(lax.broadcasted_iota(jnp.int32, (q, n), 0), shift)
    c_blocks = jnp.concatenate([jnp.where(row_seq == s, cg, 0.0) for s in range(ns)], axis=1).astype(BF16)
    h_all = h0_ref[...].reshape(ns * n, gw)
    yoff = jnp.dot(c_blocks, h_all.astype(BF16), preferred_element_type=F32)
    y_parts, xw_parts, dec_parts = [], [], []
    for pp in range(hpg // 2):
        h0 = 2 * pp
        lanes = slice(h0 * M2_HEAD_DIM, (h0 + 2) * M2_HEAD_DIM)
        w_pair = []
        for h in (h0, h0 + 1):
            seg = acs[:, h:h + 1] - acs_t[h:h + 1, :]
            dec = jnp.exp(jnp.where(causal, seg, -jnp.inf))
            w_pair.append((cb * dec * dt_t[h:h + 1, :]).astype(BF16))
        xp = xs[:, lanes]
        xbd = jnp.concatenate([jnp.where(low, xp, 0.0), jnp.where(low, 0.0, xp)], axis=0).astype(BF16)
        yd = jnp.dot(jnp.concatenate(w_pair, axis=1), xbd, preferred_element_type=F32)
        y_parts.append(yd + yoff[:, lanes] * pair_cols(eacs, h0) + dexp_ref[0, :, lanes] * xp)
        xw_parts.append(xp * pair_cols(wdec, h0))
        dec_parts.append(pair_cols(dec_tot, h0))
    xw = jnp.concatenate(xw_parts, axis=1).astype(BF16)
    col_seq = lax.shift_right_logical(lax.broadcasted_iota(jnp.int32, (n, q), 1), shift)
    bgt = bg.T
    b_blocks = jnp.concatenate([jnp.where(col_seq == s, bgt, 0.0) for s in range(ns)], axis=0).astype(BF16)
    st = jnp.dot(b_blocks, xw, preferred_element_type=F32)
    dec_rows = jnp.concatenate(dec_parts, axis=1)
    for s in range(ns):
        hn_ref[s] = h0_ref[s] * dec_rows[s * seq_len:s * seq_len + 1, :] + st[s * n:(s + 1) * n, :]

    zz = z_ref[...]
    y = jnp.concatenate(y_parts, axis=1) * (zz * _sigmoid(zz))
    y_ref[...] = y * lax.rsqrt(jnp.mean(y * y, axis=-1, keepdims=True) + EPS) * nw_ref[0]


def _by_group(a):
    gw = M2_WIDTH // M2_GROUPS
    n = M2_D_STATE
    return jnp.stack([jnp.concatenate([a[..., g * gw:(g + 1) * gw],
                                       a[..., M2_WIDTH + g * n:M2_WIDTH + (g + 1) * n],
                                       a[..., M2_WIDTH + (M2_GROUPS + g) * n:M2_WIDTH + (M2_GROUPS + g + 1) * n]],
                                      axis=-1) for g in range(M2_GROUPS)])


def _ssd_packed(xbc, z, dt, conv0, h0, prm, n_seq, seq_len):
    cw, cb, dtb, alog, dexp, nw = prm
    t = xbc.shape[0]
    hpg = M2_HEADS // M2_GROUPS
    gw = hpg * M2_HEAD_DIM
    cdim = gw + 2 * M2_D_STATE
    heads = lambda a: jnp.stack([jnp.pad(a[..., g * hpg:(g + 1) * hpg], [(0, 0)] * (a.ndim - 1) + [(0, LANES - hpg)])
                                 for g in range(M2_GROUPS)])
    halves = lambda a: jnp.stack([a[..., g * gw:(g + 1) * gw] for g in range(M2_GROUPS)])
    hist = jnp.stack([_by_group(jnp.pad(conv0[:, M2_CONV - 1 - k:, :], ((0, 0), (0, seq_len - k), (0, 0)))
                                .reshape(t, M2_CONV_DIM)) for k in range(1, M2_CONV)])
    per_g = lambda r, c: pl.BlockSpec((1, r, c), lambda i, g: (g, 0, 0))
    rows_g = lambda c: pl.BlockSpec((1, SSD_CHUNK, c), lambda i, g: (g, i, 0))
    ns = SSD_CHUNK // seq_len
    state = pl.BlockSpec((ns, M2_D_STATE, gw), lambda i, g: (i, 0, g))
    y, hn = pl.pallas_call(
        functools.partial(_ssd_packed_kernel, seq_len=seq_len),
        grid=(t // SSD_CHUNK, M2_GROUPS),
        in_specs=[rows_g(cdim),
                  pl.BlockSpec((M2_CONV - 1, 1, SSD_CHUNK, cdim), lambda i, g: (0, g, i, 0)),
                  pl.BlockSpec((SSD_CHUNK, gw), lambda i, g: (i, g)),
                  rows_g(LANES), state,
                  per_g(M2_CONV, cdim), per_g(1, cdim), per_g(1, LANES), per_g(1, LANES), per_g(1, gw), per_g(1, gw)],
        out_specs=[pl.BlockSpec((SSD_CHUNK, gw), lambda i, g: (i, g)), state],
        out_shape=[jax.ShapeDtypeStruct((t, M2_WIDTH), F32), jax.ShapeDtypeStruct((n_seq, M2_D_STATE, M2_WIDTH), F32)],
        scratch_shapes=[pltpu.VMEM((CONV_HALO + SSD_CHUNK, cdim), F32)],
        compiler_params=_cparams("parallel", "parallel"),
        name="ssd_packed",
    )(_by_group(xbc), hist, z, heads(dt), h0, _by_group(cw), _by_group(cb), heads(dtb), heads(alog),
      halves(dexp), halves(nw))
    conv_new = xbc.reshape(n_seq, seq_len, M2_CONV_DIM)[:, seq_len - (M2_CONV - 1):, :]
    return y, conv_new, hn


def _post_kernel(y5_ref, u_ref, ym_ref, x_ref, d_ref, wglu_ref, s5nw_ref, wo1_ref, wo2_ref, fnw_ref, wqt_ref,
                 sk_ref, x1_ref, xt_ref, st_ref):
    g = _gelu_tanh(_load_blocked(y5_ref) + d_ref[...] * _load_blocked(u_ref))
    o = g * _sigmoid(jnp.dot(g.astype(BF16), wglu_ref[...], preferred_element_type=F32))
    y5n = _rms(o, s5nw_ref[...])
    x1 = (x_ref[...] + jnp.dot(y5n.astype(BF16), wo1_ref[...], preferred_element_type=F32)
          + jnp.dot(ym_ref[...].astype(BF16), wo2_ref[...], preferred_element_type=F32))
    x1_ref[...] = x1
    hn_t = _rms(x1, fnw_ref[...]).T.astype(BF16)
    xt_ref[...] = hn_t
    q_t = jnp.dot(wqt_ref[...], hn_t, preferred_element_type=F32)
    for k in range(2 * PEER_HEADS):
        qk = q_t[k * PEER_HALF:(k + 1) * PEER_HALF, :].astype(BF16)
        st_ref[k] = jnp.dot(sk_ref[k], qk, preferred_element_type=F32) * LOG2E


def _post(y5, u, ym, x, prm, tm, s5_block=None):
    t = x.shape[0]
    row = lambda i: (i, 0)
    s5_spec = pl.BlockSpec((tm, D_MODEL), row) if s5_block is None else pl.BlockSpec(*s5_block)
    fix = lambda i: (0, 0)
    nk = 2 * PEER_HEADS
    once = functools.partial(pl.BlockSpec, pipeline_mode=pl.Buffered(1))
    return pl.pallas_call(
        _post_kernel,
        grid=(t // tm,),
        in_specs=[s5_spec, s5_spec, pl.BlockSpec((tm, D_MODEL), row), pl.BlockSpec((tm, D_MODEL), row)]
        + [pl.BlockSpec((1, D_MODEL), fix), once((S5_WIDTH, S5_WIDTH), fix), pl.BlockSpec((1, D_MODEL), fix),
           once((S5_WIDTH, D_MODEL), fix), once((M2_WIDTH, D_MODEL), fix),
           pl.BlockSpec((1, D_MODEL), fix), once((nk * PEER_HALF, D_MODEL), fix),
           once((nk, PEER_N_KEYS, PEER_HALF), lambda i: (0, 0, 0))],
        out_specs=[pl.BlockSpec((tm, D_MODEL), row), pl.BlockSpec((D_MODEL, tm), lambda i: (0, i)),
                   pl.BlockSpec((nk, PEER_N_KEYS, tm), lambda i: (0, 0, i))],
        out_shape=[jax.ShapeDtypeStruct((t, D_MODEL), F32), jax.ShapeDtypeStruct((D_MODEL, t), BF16),
                   jax.ShapeDtypeStruct((nk, PEER_N_KEYS, t), F32)],
        compiler_params=_cparams("parallel"),
        name="post",
    )(y5, u, ym, x, *prm)


def _sorting_network(n):
    pairs = []
    p = 1
    while p < n:
        k = p
        while k >= 1:
            for j in range(k % p, n - k, 2 * k):
                for i in range(min(k, n - j - k)):
                    if (i + j) // (2 * p) == (i + j + k) // (2 * p):
                        pairs.append((i + j, i + j + k))
            k //= 2
        p *= 2
    return pairs


SUBLANES = 8
_NET16 = _sorting_network(PEER_N_KEYS // SUBLANES)
LOG2E = math.log2(math.e)


def _top16(tiles):
    n = len(tiles)
    tiles = list(tiles)

    def exchange(i, j):
        hi, lo = tiles[i], tiles[j]
        if lo is None:
            return
        if hi is None:
            tiles[i], tiles[j] = lo, None
        else:
            tiles[i], tiles[j] = jnp.maximum(hi, lo), jnp.minimum(hi, lo)

    def larger(a, b):
        return b if a is None else a if b is None else jnp.maximum(a, b)

    for i, j in _NET16:
        exchange(i, j)
    shift = SUBLANES // 2
    while shift >= 1:
        other = [None if t is None else pltpu.roll(t, shift, axis=0) for t in tiles]
        tiles = [larger(tiles[i], other[n - 1 - i]) for i in range(n)]
        dist = n // 2
        while dist >= 1:
            for i in range(n):
                if i & dist == 0:
                    exchange(i, i + dist)
            dist //= 2
        shift //= 2
    return tiles


def _top_values(s):
    n = PEER_N_KEYS // SUBLANES
    tiles = _top16([s[i * SUBLANES:(i + 1) * SUBLANES, :] for i in range(n)])
    rows = lax.broadcasted_iota(jnp.int32, (PEER_TOPK, s.shape[1]), 0)
    out = jnp.zeros((PEER_TOPK, s.shape[1]), F32)
    for r in range(PEER_TOPK):
        out = jnp.where(rows == r, jnp.concatenate([tiles[r], tiles[r]], axis=0), out)
    return out


def _pair_candidates(v1, v2):
    r8 = lax.broadcasted_iota(jnp.int32, (SUBLANES, v1.shape[1]), 0)
    r16 = lax.broadcasted_iota(jnp.int32, (PEER_TOPK, v1.shape[1]), 0)
    neg = -jnp.inf
    lo2 = v2[0:SUBLANES, :]
    return jnp.concatenate([
        v1[0:1, :] + v2,
        v1[1:2, :] + lo2,
        jnp.where(r16 >= 2, v1 + v2[0:1, :], neg),
        jnp.where(r8 >= 2, v1[0:SUBLANES, :] + v2[1:2, :], neg),
        jnp.where((r8 >= 2) & (r8 <= 4), v1[2:3, :] + lo2, neg),
        jnp.where((r8 >= 2) & (r8 <= 3), v1[3:4, :] + lo2, neg),
        jnp.where(r8 == 2, v1[4:5, :] + lo2, neg),
    ], axis=0)


def _route_kernel(st_ref, s1n_ref, thr_ref):
    def head(h, carry):
        s1 = st_ref[2 * h]
        s2 = st_ref[2 * h + 1]
        v1 = _top_values(s1)
        v2 = _top_values(s2)
        cand = _pair_candidates(v1, v2)
        n_cand = cand.shape[0] // SUBLANES
        pad = [None] * (PEER_N_KEYS // SUBLANES - n_cand)
        ranked = _top16([cand[i * SUBLANES:(i + 1) * SUBLANES, :] for i in range(n_cand)] + pad)
        theta = ranked[PEER_TOPK - 1][0:1, :]
        sel = cand >= theta
        m = v1[0:1, :] + v2[0:1, :]
        zsum = jnp.sum(jnp.where(sel, jnp.exp2(cand - m), 0.0), axis=0, keepdims=True)
        off = m + jnp.log(zsum) * LOG2E + 1.0
        s1n_ref[h] = s1 - off
        candn = _pair_candidates(v1 - off, v2)
        thr_ref[h] = jnp.min(jnp.where(sel, candn, jnp.inf), axis=0, keepdims=True)
        return carry

    lax.fori_loop(0, PEER_HEADS, head, 0)


def _route(st, tl):
    nk, keys, t = st.shape
    spec = lambda n: pl.BlockSpec((n, keys, tl), lambda i: (0, 0, i))
    return pl.pallas_call(
        _route_kernel,
        grid=(t // tl,),
        in_specs=[spec(nk)],
        out_specs=[spec(PEER_HEADS), pl.BlockSpec((PEER_HEADS, 1, tl), lambda i: (0, 0, i))],
        out_shape=[jax.ShapeDtypeStruct((PEER_HEADS, keys, t), F32), jax.ShapeDtypeStruct((PEER_HEADS, 1, t), F32)],
        compiler_params=_cparams("parallel"),
        name="route",
    )(st)


PEER_E1_BLK = 16
PEER_CHUNK_KEYS = (4, 4, 4, 4)
assert sum(PEER_CHUNK_KEYS) == PEER_E1_BLK


def _peer_kernel(u_ref, vt_ref, xt_ref, s2n_ref, s1n_ref, thr_ref, x1_ref, fnw_ref, y_ref, acc_scr, *, tm):
    k = pl.program_id(1)
    c0 = math.sqrt(2.0 / math.pi)
    c1 = c0 * 0.044715
    first_key = [sum(PEER_CHUNK_KEYS[:q]) for q in range(len(PEER_CHUNK_KEYS))]

    def experts(q):
        return slice(first_key[q] * PEER_N_KEYS, (first_key[q] + PEER_CHUNK_KEYS[q]) * PEER_N_KEYS)

    def scores(q):
        return jnp.dot(u_ref[experts(q), :], xt_ref[...], preferred_element_type=F32)

    def gated(q, a):
        wg_rows = []
        for i in range(PEER_CHUNK_KEYS[q]):
            r = first_key[q] + i
            rows = slice(i * PEER_N_KEYS, (i + 1) * PEER_N_KEYS)
            wg_cols = []
            for c in range(tm // LANES):
                cols = slice(c * LANES, (c + 1) * LANES)
                w = None
                for h in range(PEER_HEADS):
                    arg = s2n_ref[h, :, cols] + s1n_ref[h, r:r + 1, cols]
                    term = jnp.where(arg >= thr_ref[h, :, cols], jnp.exp2(arg), 0.0)
                    w = term if w is None else w + term
                x = a[rows, cols]
                g = x + x * jnp.tanh(x * (c0 + c1 * (x * x)))
                wg_cols.append((w * g).astype(BF16))
            wg_rows.append(jnp.concatenate(wg_cols, axis=1))
        return jnp.concatenate(wg_rows, axis=0)

    def mixed(q, wg):
        return jnp.dot(vt_ref[0, :, experts(q)], wg, preferred_element_type=F32)

    n_chunks = len(PEER_CHUNK_KEYS)
    total = None
    a_next = scores(0)
    wg_prev = None
    for q in range(n_chunks):
        a_cur = a_next
        if q + 1 < n_chunks:
            a_next = scores(q + 1)
        wg = gated(q, a_cur)
        if wg_prev is not None:
            d = mixed(q - 1, wg_prev)
            total = d if total is None else total + d
        wg_prev = wg
    total = total + mixed(n_chunks - 1, wg_prev)

    @pl.when(k == 0)
    def _():
        acc_scr[...] = total

    @pl.when(k > 0)
    def _():
        acc_scr[...] += total

    @pl.when(k == pl.num_programs(1) - 1)
    def _():
        y_ref[...] = _rms(x1_ref[...] + acc_scr[...].T, fnw_ref[...])


def _peer(u_bf, vt_bf, xt, st, route, x1, fnw, tm):
    s1n, thr = route
    t = x1.shape[0]
    eb = PEER_E1_BLK * PEER_N_KEYS
    s2 = st.reshape(PEER_HEADS, 2, PEER_N_KEYS, t)
    per_key = pl.BlockSpec((PEER_HEADS, None, PEER_N_KEYS, tm), lambda j, k: (0, 1, 0, j))
    per_blk = pl.BlockSpec((PEER_HEADS, PEER_E1_BLK, tm), lambda j, k: (0, k, j))
    return pl.pallas_call(
        functools.partial(_peer_kernel, tm=tm),
        grid=(t // tm, PEER_N_KEYS // PEER_E1_BLK),
        in_specs=[pl.BlockSpec((eb, D_MODEL), lambda j, k: (k, 0)),
                  pl.BlockSpec((1, D_MODEL, eb), lambda j, k: (k, 0, 0)),
                  pl.BlockSpec((D_MODEL, tm), lambda j, k: (0, j)),
                  per_key, per_blk, pl.BlockSpec((PEER_HEADS, 1, tm), lambda j, k: (0, 0, j)),
                  pl.BlockSpec((tm, D_MODEL), lambda j, k: (j, 0)),
                  pl.BlockSpec((1, D_MODEL), lambda j, k: (0, 0))],
        out_specs=pl.BlockSpec((tm, D_MODEL), lambda j, k: (j, 0)),
        out_shape=jax.ShapeDtypeStruct((t, D_MODEL), F32),
        scratch_shapes=[pltpu.VMEM((D_MODEL, tm), F32)],
        compiler_params=_cparams("parallel", "arbitrary"),
        name="peer",
    )(u_bf, vt_bf, xt, s2, s1n, thr, x1, fnw)


TOKEN_TILE = 512


def _tokens_tail(x, post_prm, peer_prm, y5, u, ym, s5_block):
    x1, xt, st = _post(y5, u, ym, x, post_prm, tm=TOKEN_TILE, s5_block=s5_block)
    u_bf, vt_bf, fnw = peer_prm
    return _peer(u_bf, vt_bf, xt, st, _route(st, tl=2 * LANES), x1, fnw, tm=TOKEN_TILE)


def kernel(x_prompt, x_sample, state_s5_re, state_s5_im, state_ssm, state_conv, meta_tokens, norm_mix_w, w_in,
           s5_lambda_re, s5_lambda_im, s5_log_step, s5_b_re, s5_b_im, s5_c_re, s5_c_im, s5_d, s5_w_glu, s5_norm_w,
           m2_conv_w, m2_conv_b, m2_dt_bias, m2_a_log, m2_d, m2_norm_w, w_out, norm_ffn_w, peer_w_q, peer_sub_keys,
           peer_u, peer_v, final_norm_w):
    bp, sp, _ = x_prompt.shape
    bs, ss, _ = x_sample.shape
    g, p = S5_GROUPS, S5_STATE

    w = w_in[0]
    o1, o2, o3 = S5_WIDTH, S5_WIDTH + M2_WIDTH, S5_WIDTH + M2_WIDTH + M2_CONV_DIM
    wu, wz, wx = w[:, :o1].astype(BF16), w[:, o1:o2].astype(BF16), w[:, o2:o3].astype(BF16)
    wd = jnp.pad(w[:, o3:], ((0, 0), (0, LANES - M2_HEADS))).astype(BF16)
    nmw = norm_mix_w[0][None, :]
    pad_h = lambda v: jnp.pad(v, (0, LANES - M2_HEADS))[None, :]
    ssd_prm = (m2_conv_w[0], m2_conv_b[0][None, :], pad_h(m2_dt_bias[0]), pad_h(m2_a_log[0]),
               jnp.repeat(m2_d[0], M2_HEAD_DIM)[None, :], m2_norm_w[0][None, :])
    post_prm = (s5_d[0][None, :], s5_w_glu[0].astype(BF16), s5_norm_w[0][None, :],
                w_out[0][:S5_WIDTH].astype(BF16), w_out[0][S5_WIDTH:].astype(BF16), norm_ffn_w[0][None, :],
                peer_w_q[0].T.astype(BF16),
                peer_sub_keys[0].reshape(2 * PEER_HEADS, PEER_N_KEYS, PEER_HALF).astype(BF16))
    eb = PEER_E1_BLK * PEER_N_KEYS
    vt_blocks = peer_v[0].astype(BF16).reshape(PEER_EXPERTS // eb, eb, D_MODEL).swapaxes(1, 2)
    peer_prm = (peer_u[0].astype(BF16), vt_blocks, final_norm_w[None, :])
    s5_args = (s5_lambda_re[0], s5_lambda_im[0], s5_log_step[0], s5_b_re[0], s5_b_im[0], s5_c_re[0], s5_c_im[0])
    ops16 = _s5_prep(*s5_args, lc=N_META)
    ops8 = _s5_prep(*s5_args, lc=ss)

    lc = N_META
    um, _, xbcm, dtm = _in_proj(meta_tokens, nmw, wu, wz, wx, wd, tm=N_META)
    nb = S5_WIDTH // LANES
    urm = jnp.broadcast_to(um.reshape(lc, nb, LANES).transpose(1, 0, 2).reshape(nb, 1, lc * LANES),
                           (nb, SUBLANES, lc * LANES))
    _, h5m = _s5(urm, jnp.zeros((g, SUBLANES, 2 * p), F32), ops16, lc=lc, rows=SUBLANES, chunks=1)
    _, convm, hm = _ssd(xbcm, jnp.zeros((N_META, M2_WIDTH), F32), dtm,
                        jnp.zeros((1, M2_CONV - 1, M2_CONV_DIM), F32), jnp.zeros((1, M2_D_STATE, M2_WIDTH), F32),
                        ssd_prm, n_seq=1, n_chunks=1, q_in=N_META, shared_init=True)

    xp = x_prompt.reshape(bp * sp, D_MODEL)
    nc = sp // lc
    cpt = TOKEN_TILE // lc
    tps = sp // TOKEN_TILE
    p_shape = (nb, nc, bp, lc, LANES)
    p_block = ((nb, cpt, 1, lc, LANES), lambda i: (0, i % tps, i // tps, 0, 0))
    up, zp, xbcp, dtp = _in_proj(xp, nmw, wu, wz, wx, wd, tm=TOKEN_TILE, u_layout=(p_shape,) + p_block)
    y5p, h5p = _s5(up.reshape(nb, nc * bp * lc, LANES), h5m, ops16, lc=lc, rows=bp, chunks=S5_ROW_TILE // bp)
    ymp, convp, hp = _ssd(xbcp, zp, dtp, convm, hm, ssd_prm, n_seq=bp, n_chunks=sp // SSD_CHUNK, q_in=SSD_CHUNK,
                          shared_init=True)
    y_prompt = _tokens_tail(xp, post_prm, peer_prm, y5p.reshape(p_shape), up, ymp, p_block).reshape(bp, sp, D_MODEL)

    xs = x_sample.reshape(bs * ss, D_MODEL)
    s_shape = (nb, bs, ss, LANES)
    s_block = ((nb, TOKEN_TILE // ss, ss, LANES), lambda i: (0, i, 0, 0))
    us, zs, xbcs, dts = _in_proj(xs, nmw, wu, wz, wx, wd, tm=TOKEN_TILE, u_layout=(s_shape,) + s_block)
    h5s0 = jnp.concatenate([state_s5_re[0], state_s5_im[0]], axis=-1).transpose(1, 0, 2)
    y5s, h5s = _s5(us.reshape(nb, bs * ss, LANES), h5s0, ops8, lc=ss, rows=bs, chunks=1)
    hs0 = state_ssm[0].reshape(bs, M2_WIDTH, M2_D_STATE).transpose(0, 2, 1)
    yms, convs, hs = _ssd_packed(xbcs, zs, dts, state_conv[0], hs0, ssd_prm, n_seq=bs, seq_len=ss)
    y_sample = _tokens_tail(xs, post_prm, peer_prm, y5s.reshape(s_shape), us, yms, s_block).reshape(bs, ss, D_MODEL)

    def s5_state(hf):
        hf = hf.transpose(1, 0, 2)
        return hf[None, :, :, :p], hf[None, :, :, p:]

    def ssm_state(ht):
        return ht.transpose(0, 2, 1).reshape(1, ht.shape[0], M2_HEADS, M2_HEAD_DIM, M2_D_STATE)

    p5r, p5i = s5_state(h5p)
    s5r, s5i = s5_state(h5s)
    return (y_prompt, y_sample, p5r, p5i, ssm_state(hp), convp[None], s5r, s5i, ssm_state(hs), convs[None])
```
